```python
import math
import jax
import jax.numpy as jnp
from jax import lax
import numpy as np

D_MODEL = 1024
BATCH = 4
SEQ = 4096
DEPTH = 2
DEC_BATCH = 32
DEC_SEQ = 4
PAST_LEN = 8192
PAGE_SIZE = 128

GROUP_W = D_MODEL // 4
LRU_W = GROUP_W
LRU_BLOCKS = 4
LRU_CONV = 4
LRU_C = 8.0
DIFF_HEADS = 4
DIFF_V = GROUP_W // DIFF_HEADS
DIFF_QK = DIFF_V // 2
DSA_HEADS = 4
DSA_HD = GROUP_W // DSA_HEADS
IDX_HEADS = 8
IDX_DIM = 32
TOPK_MAX = 256
GLA_HEADS = 4
GLA_DK = GROUP_W // 2 // GLA_HEADS
GLA_DV = GROUP_W // GLA_HEADS
GLA_RANK = 16
GLA_TAU = 16.0
GLA_CHUNK = 64
D_FF = 2816
FFN_CONV = 3
ROPE_THETA = 10000.0
EPS = 1e-6
Q_BLOCK = 128

_IN_SPLITS = (
    ("lru_x", LRU_W), ("lru_gate", LRU_W),
    ("diff_q", DIFF_HEADS * 2 * DIFF_QK), ("diff_k", DIFF_HEADS * 2 * DIFF_QK), ("diff_v", DIFF_HEADS * DIFF_V),
    ("dsa_q", DSA_HEADS * DSA_HD), ("dsa_k", DSA_HEADS * DSA_HD), ("dsa_v", DSA_HEADS * DSA_HD),
    ("idx_q", IDX_HEADS * IDX_DIM), ("idx_k", IDX_DIM), ("idx_w", IDX_HEADS),
    ("gla_q", GLA_HEADS * GLA_DK), ("gla_k", GLA_HEADS * GLA_DK), ("gla_v", GLA_HEADS * GLA_DV),
    ("gla_g", GLA_HEADS * GLA_DV), ("gla_a", GLA_RANK),
)
D_IN = sum(w for _, w in _IN_SPLITS)
STATE_KEYS = ("diff_k", "diff_v", "dsa_k", "dsa_v", "idx_k", "lru_h", "lru_conv", "gla", "ffn_conv")

kernel_name = "hymba_lru_diff_dsa_gla_decode_step"


def split_cols(proj):
    out = {}
    off = 0
    for name, w in _IN_SPLITS:
        out[name] = proj[..., off:off + w]
        off += w
    return out


def rmsnorm(x, g):
    xf = x.astype(jnp.float32)
    y = xf * lax.rsqrt(jnp.mean(xf * xf, axis=-1, keepdims=True) + EPS)
    return (y * g.astype(jnp.float32)).astype(x.dtype)


def rope(x, pos):
    half = x.shape[-1] // 2
    inv = ROPE_THETA ** (-jnp.arange(half, dtype=jnp.float32) / half)
    ang = pos.astype(jnp.float32)[:, None] * inv[None, :]
    shp = (pos.shape[0],) + (1,) * (x.ndim - 3) + (half,)
    cos = jnp.cos(ang).reshape(shp)
    sin = jnp.sin(ang).reshape(shp)
    xf = x.astype(jnp.float32)
    x1, x2 = xf[..., :half], xf[..., half:]
    return jnp.concatenate([x1 * cos - x2 * sin, x1 * sin + x2 * cos], axis=-1).astype(x.dtype)


def causal_dwconv(x, buf, w, b):
    K = w.shape[0]
    T = x.shape[1]
    xx = jnp.concatenate([buf.astype(x.dtype), x], axis=1)
    y = b + sum(xx[:, j:j + T] * w[j] for j in range(K))
    return y, xx[:, T:]


def _lin_comb(e1, e2):
    a1, b1 = e1
    a2, b2 = e2
    return a1 * a2, a2 * b1 + b2


def rg_lru(x, h0, pos, w_a, b_a, w_x, b_x, lam):
    Bn, T, W = x.shape
    xb = x.reshape(Bn, T, LRU_BLOCKS, W // LRU_BLOCKS)
    gate_a = jax.nn.sigmoid((jnp.einsum("btni,nij->btnj", xb, w_a).reshape(Bn, T, W) + b_a).astype(jnp.float32))
    gate_x = jax.nn.sigmoid((jnp.einsum("btni,nij->btnj", xb, w_x).reshape(Bn, T, W) + b_x).astype(jnp.float32))
    log_a = -LRU_C * gate_a * jax.nn.softplus(-lam.astype(jnp.float32))
    a = jnp.exp(log_a)
    mult = jnp.sqrt(-jnp.expm1(2.0 * log_a))
    mult = jnp.where((pos == 0)[None, :, None], 1.0, mult)
    u = mult * gate_x * x.astype(jnp.float32)
    u = u.at[:, 0].add(a[:, 0] * h0.astype(jnp.float32))
    _, h = lax.associative_scan(_lin_comb, (a, u), axis=1)
    return h, h[:, -1]


def over_query_blocks(fn, q_arrays, q_pos):
    T = q_pos.shape[0]
    nb = T // Q_BLOCK

    def blk(a):
        return jnp.swapaxes(a.reshape(a.shape[0], nb, Q_BLOCK, *a.shape[2:]), 0, 1)

    out = lax.map(lambda args: fn(*args), tuple(blk(a) for a in q_arrays) + (q_pos.reshape(nb, Q_BLOCK),))
    out = jnp.swapaxes(out, 0, 1)
    return out.reshape(out.shape[0], T, *out.shape[3:])


def diff_attend(q, k, v, lam, q_pos, k_pos):
    s = jnp.einsum("bqhmd,bshmd->bhmqs", q, k).astype(jnp.float32) * DIFF_QK ** -0.5
    s = jnp.where(k_pos[None, :] <= q_pos[:, None], s, -jnp.inf)
    p = jax.nn.softmax(s, axis=-1)
    w = p[:, :, 0] - lam * p[:, :, 1]
    return jnp.einsum("bhqs,bshd->bqhd", w.astype(v.dtype), v)


def take_rows(x, idx):
    return jax.vmap(lambda xx, ii: xx[ii])(x, idx)


def gather_pages(pool, page_table):
    g = pool[page_table]
    return g.reshape(g.shape[0], g.shape[1] * g.shape[2], *g.shape[3:])


def dsa_select(iq, iw, ik, q_pos, k_pos, topk):
    sc = jnp.einsum("bqhd,bsd->bqhs", iq, ik).astype(jnp.float32) * IDX_DIM ** -0.5
    score = jnp.einsum("bqh,bqhs->bqs", iw.astype(jnp.float32) * IDX_HEADS ** -0.5, jax.nn.relu(sc))
    score = jnp.where(k_pos[None, :] <= q_pos[:, None], score, -jnp.inf)
    val, idx = lax.top_k(score, topk)
    return idx, jnp.isfinite(val)


def dsa_attend(q, kg, vg, valid):
    s = jnp.einsum("bqhd,bqkhd->bhqk", q, kg).astype(jnp.float32) * DSA_HD ** -0.5
    s = jnp.where(valid[:, None], s, -jnp.inf)
    p = jax.nn.softmax(s, axis=-1)
    return jnp.einsum("bhqk,bqkhd->bqhd", p.astype(vg.dtype), vg)


def gla(q, k, v, log_a, s0):
    Bn, T, H, dk = q.shape
    f32 = jnp.float32
    C = GLA_CHUNK if T % GLA_CHUNK == 0 else T
    n = T // C

    def to_chunks(a):
        return jnp.swapaxes(a.astype(f32).reshape(Bn, n, C, *a.shape[2:]), 0, 1)

    qc, kc, vc, ac = (to_chunks(a) for a in (q * dk ** -0.5, k, v, log_a))
    causal = jnp.tril(jnp.ones((C, C), dtype=bool))[None, :, :, None, None]

    def step(S, inp):
        qi, ki, vi, ai = inp
        b = jnp.cumsum(ai, axis=1)
        o_inter = jnp.einsum("bthk,bhkv->bthv", qi * jnp.exp(b), S)
        decay = jnp.exp(jnp.where(causal, b[:, :, None] - b[:, None, :], -jnp.inf))
        att = jnp.einsum("bthk,bshk,btshk->bhts", qi, ki, decay)
        o_intra = jnp.einsum("bhts,bshv->bthv", att, vi)
        b_end = b[:, -1]
        S = jnp.exp(b_end)[..., None] * S + jnp.einsum("bshk,bshv->bhkv", ki * jnp.exp(b_end[:, None] - b), vi)
        return S, o_inter + o_intra

    s_last, o = lax.scan(step, s0.astype(f32), (qc, kc, vc, ac))
    return jnp.swapaxes(o, 0, 1).reshape(Bn, T, H, v.shape[-1]), s_last


def decoder_layer(x, c, lp, l, pos, past, page_table):
    Bn, T, _ = x.shape
    f32 = jnp.float32
    prompt = past is None
    new = {}
    mod = (jax.nn.silu(c) @ lp["w_ada"] + lp["b_ada"])[:, None, :]
    sh1, sc1, g1, sh2, sc2, g2 = jnp.split(mod, 6, axis=-1)

    h = rmsnorm(x, lp["norm1_g"]) * (1.0 + sc1) + sh1
    p = split_cols(h @ lp["w_in"])

    lru_buf = jnp.zeros((Bn, LRU_CONV - 1, LRU_W), x.dtype) if prompt else past["lru_conv"]
    lru_h0 = jnp.zeros((Bn, LRU_W), x.dtype) if prompt else past["lru_h"]
    xa, new["lru_conv"] = causal_dwconv(p["lru_x"], lru_buf, lp["lru_conv_w"], lp["lru_conv_b"])
    hs, h_last = rg_lru(xa, lru_h0, pos, lp["lru_wa"], lp["lru_ba"], lp["lru_wx"], lp["lru_bx"], lp["lru_lambda"])
    new["lru_h"] = h_last.astype(x.dtype)
    o_a = hs.astype(x.dtype) * jax.nn.gelu(p["lru_gate"])

    q_b = rope(p["diff_q"].reshape(Bn, T, DIFF_HEADS, 2, DIFF_QK), pos)
    k_b = rope(p["diff_k"].reshape(Bn, T, DIFF_HEADS, 2, DIFF_QK), pos)
    v_b = p["diff_v"].reshape(Bn, T, DIFF_HEADS, DIFF_V)
    new["diff_k"], new["diff_v"] = k_b, v_b
    lam_init = 0.8 - 0.6 * math.exp(-0.3 * l)
    lam = (jnp.exp(jnp.sum(lp["diff_lq1"].astype(f32) * lp["diff_lk1"].astype(f32)))
           - jnp.exp(jnp.sum(lp["diff_lq2"].astype(f32) * lp["diff_lk2"].astype(f32))) + lam_init)
    if prompt:
        o_b = over_query_blocks(lambda qb, pb: diff_attend(qb, k_b, v_b, lam, pb, pos), (q_b,), pos)
    else:
        kk = jnp.concatenate([gather_pages(past["diff_k"], page_table), k_b], axis=1)
        vv = jnp.concatenate([gather_pages(past["diff_v"], page_table), v_b], axis=1)
        o_b = diff_attend(q_b, kk, vv, lam, pos, jnp.arange(kk.shape[1], dtype=jnp.int32))
    o_b = (rmsnorm(o_b, lp["diff_subln_g"]) * (1.0 - lam_init)).reshape(Bn, T, GROUP_W)

    q_c = rope(p["dsa_q"].reshape(Bn, T, DSA_HEADS, DSA_HD), pos)
    k_c = rope(p["dsa_k"].reshape(Bn, T, DSA_HEADS, DSA_HD), pos)
    v_c = p["dsa_v"].reshape(Bn, T, DSA_HEADS, DSA_HD)
    iq = rope(p["idx_q"].reshape(Bn, T, IDX_HEADS, IDX_DIM), pos)
    ik = rope(p["idx_k"], pos)
    iw = p["idx_w"]
    new["dsa_k"], new["dsa_v"], new["idx_k"] = k_c, v_c, ik
    if prompt:
        topk = min(TOPK_MAX, T // 4)

        def dsa_block(qb, iqb, iwb, pb):
            idx, valid = dsa_select(iqb, iwb, ik, pb, pos, topk)
            return dsa_attend(qb, take_rows(k_c, idx), take_rows(v_c, idx), valid)

        o_c = over_query_blocks(dsa_block, (q_c, iq, iw), pos)
    else:
        past_len = page_table.shape[1] * PAGE_SIZE
        n_keys = past_len + T
        topk = min(TOPK_MAX, n_keys // 4)
        ik_all = jnp.concatenate([gather_pages(past["idx_k"], page_table), ik], axis=1)
        idx, valid = dsa_select(iq, iw, ik_all, pos, jnp.arange(n_keys, dtype=jnp.int32), topk)
        from_past = (idx < past_len)[..., None, None]
        pidx = jnp.minimum(idx, past_len - 1)
        phys = take_rows(page_table, pidx // PAGE_SIZE)
        off = pidx % PAGE_SIZE
        nidx = jnp.clip(idx - past_len, 0, T - 1)
        kg = jnp.where(from_past, past["dsa_k"][phys, off], take_rows(k_c, nidx))
        vg = jnp.where(from_past, past["dsa_v"][phys, off], take_rows(v_c, nidx))
        o_c = dsa_attend(q_c, kg, vg, valid)
    o_c = o_c.reshape(Bn, T, GROUP_W)

    q_d = p["gla_q"].reshape(Bn, T, GLA_HEADS, GLA_DK)
    k_d = p["gla_k"].reshape(Bn, T, GLA_HEADS, GLA_DK)
    v_d = p["gla_v"].reshape(Bn, T, GLA_HEADS, GLA_DV)
    log_alpha = (jax.nn.log_sigmoid((p["gla_a"] @ lp["gla_wa2"] + lp["gla_ba"]).astype(f32)) / GLA_TAU
                 ).reshape(Bn, T, GLA_HEADS, GLA_DK)
    s0 = jnp.zeros((Bn, GLA_HEADS, GLA_DK, GLA_DV), f32) if prompt else past["gla"]
    o_g, s_last = gla(q_d, k_d, v_d, log_alpha, s0)
    new["gla"] = s_last.astype(x.dtype)
    o_d = (rmsnorm(o_g.astype(x.dtype), lp["gla_norm_g"])
           * jax.nn.silu(p["gla_g"].reshape(Bn, T, GLA_HEADS, GLA_DV))).reshape(Bn, T, GROUP_W)

    mix = jnp.concatenate([o_a, o_b, o_c, o_d], axis=-1) @ lp["w_out"]
    x = x + g1 * mix

    h = rmsnorm(x, lp["norm2_g"]) * (1.0 + sc2) + sh2
    ffn_buf = jnp.zeros((Bn, FFN_CONV - 1, D_FF), x.dtype) if prompt else past["ffn_conv"]
    z, new["ffn_conv"] = causal_dwconv(h @ lp["ffn_w_gate"], ffn_buf, lp["ffn_conv_w"], lp["ffn_conv_b"])
    y = (jax.nn.silu(z) * (h @ lp["ffn_w_up"])) @ lp["ffn_w_down"]
    x = x + g2 * y
    return x, new


def setup_inputs(seed: int = 0) -> dict:
    key = jax.random.key(seed)
    keys = jax.random.split(key, 64)
    counter = [0]

    def nk():
        k = keys[counter[0]]
        counter[0] += 1
        return k

    f32 = jnp.float32

    def nrm(shape, scale=1.0):
        return scale * jax.random.normal(nk(), shape, f32)

    def gain(shape):
        return 1.0 + nrm(shape, 0.02)

    n_pages = PAST_LEN // PAGE_SIZE
    n_pool = (DEC_BATCH * n_pages * 5) // 4
    bw = LRU_W // LRU_BLOCKS

    x_prompt = nrm((BATCH, SEQ, D_MODEL))
    x_sample = nrm((DEC_BATCH, DEC_SEQ, D_MODEL))
    cache_diff_k = nrm((DEPTH, n_pool, PAGE_SIZE, DIFF_HEADS, 2, DIFF_QK))
    cache_diff_v = nrm((DEPTH, n_pool, PAGE_SIZE, DIFF_HEADS, DIFF_V))
    cache_dsa_k = nrm((DEPTH, n_pool, PAGE_SIZE, DSA_HEADS, DSA_HD))
    cache_dsa_v = nrm((DEPTH, n_pool, PAGE_SIZE, DSA_HEADS, DSA_HD))
    cache_idx_k = nrm((DEPTH, n_pool, PAGE_SIZE, IDX_DIM))
    state_lru_h = nrm((DEPTH, DEC_BATCH, LRU_W), 0.5)
    state_lru_conv = nrm((DEPTH, DEC_BATCH, LRU_CONV - 1, LRU_W))
    state_gla = nrm((DEPTH, DEC_BATCH, GLA_HEADS, GLA_DK, GLA_DV), 0.5)
    state_ffn_conv = nrm((DEPTH, DEC_BATCH, FFN_CONV - 1, D_FF))
    page_table = jax.random.permutation(nk(), n_pool)[: DEC_BATCH * n_pages].reshape(DEC_BATCH, n_pages).astype(jnp.int32)
    c_prompt = nrm((BATCH, D_MODEL))
    c_sample = nrm((DEC_BATCH, D_MODEL))

    w_ada = nrm((DEPTH, D_MODEL, 6 * D_MODEL), 0.5 * D_MODEL ** -0.5)
    b_ada = nrm((DEPTH, 6 * D_MODEL), 0.02)
    norm1_g = gain((DEPTH, D_MODEL))
    w_in = nrm((DEPTH, D_MODEL, D_IN), D_MODEL ** -0.5)
    lru_conv_w = nrm((DEPTH, LRU_CONV, LRU_W), LRU_CONV ** -0.5)
    lru_conv_b = nrm((DEPTH, LRU_W), 0.02)
    lru_wa = nrm((DEPTH, LRU_BLOCKS, bw, bw), bw ** -0.5)
    lru_ba = nrm((DEPTH, LRU_W), 0.02)
    lru_wx = nrm((DEPTH, LRU_BLOCKS, bw, bw), bw ** -0.5)
    lru_bx = nrm((DEPTH, LRU_W), 0.02)
    u = jax.random.uniform(nk(), (DEPTH, LRU_W), f32, 0.9, 0.999)
    a_base = u ** (1.0 / LRU_C)
    lru_lambda = jnp.log(a_base) - jnp.log1p(-a_base)
    diff_lq1 = nrm((DEPTH, DIFF_QK), 0.1)
    diff_lk1 = nrm((DEPTH, DIFF_QK), 0.1)
    diff_lq2 = nrm((DEPTH, DIFF_QK), 0.1)
    diff_lk2 = nrm((DEPTH, DIFF_QK), 0.1)
    diff_subln_g = gain((DEPTH, DIFF_V))
    gla_wa2 = nrm((DEPTH, GLA_RANK, GLA_HEADS * GLA_DK), GLA_RANK ** -0.5)
    gla_ba = nrm((DEPTH, GLA_HEADS * GLA_DK), 0.02)
    gla_norm_g = gain((DEPTH, GLA_DV))
    w_out = nrm((DEPTH, D_MODEL, D_MODEL), D_MODEL ** -0.5)
    norm2_g = gain((DEPTH, D_MODEL))
    ffn_w_gate = nrm((DEPTH, D_MODEL, D_FF), D_MODEL ** -0.5)
    ffn_w_up = nrm((DEPTH, D_MODEL, D_FF), D_MODEL ** -0.5)
    ffn_conv_w = nrm((DEPTH, FFN_CONV, D_FF), FFN_CONV ** -0.5)
    ffn_conv_b = nrm((DEPTH, D_FF), 0.02)
    ffn_w_down = nrm((DEPTH, D_FF, D_MODEL), D_FF ** -0.5)
    final_norm_g = gain((D_MODEL,))
    return {
        "x_prompt": x_prompt, "x_sample": x_sample,
        "cache_diff_k": cache_diff_k, "cache_diff_v": cache_diff_v,
        "cache_dsa_k": cache_dsa_k, "cache_dsa_v": cache_dsa_v, "cache_idx_k": cache_idx_k,
        "state_lru_h": state_lru_h, "state_lru_conv": state_lru_conv,
        "state_gla": state_gla, "state_ffn_conv": state_ffn_conv,
        "page_table": page_table, "c_prompt": c_prompt, "c_sample": c_sample,
        "w_ada": w_ada, "b_ada": b_ada, "norm1_g": norm1_g, "w_in": w_in,
        "lru_conv_w": lru_conv_w, "lru_conv_b": lru_conv_b, "lru_wa": lru_wa, "lru_ba": lru_ba,
        "lru_wx": lru_wx, "lru_bx": lru_bx, "lru_lambda": lru_lambda,
        "diff_lq1": diff_lq1, "diff_lk1": diff_lk1, "diff_lq2": diff_lq2, "diff_lk2": diff_lk2,
        "diff_subln_g": diff_subln_g, "gla_wa2": gla_wa2, "gla_ba": gla_ba, "gla_norm_g": gla_norm_g,
        "w_out": w_out, "norm2_g": norm2_g, "ffn_w_gate": ffn_w_gate, "ffn_w_up": ffn_w_up,
        "ffn_conv_w": ffn_conv_w, "ffn_conv_b": ffn_conv_b, "ffn_w_down": ffn_w_down,
        "final_norm_g": final_norm_g,
    }


def reference(x_prompt, x_sample, cache_diff_k, cache_diff_v, cache_dsa_k, cache_dsa_v, cache_idx_k,
              state_lru_h, state_lru_conv, state_gla, state_ffn_conv, page_table, c_prompt, c_sample,
              w_ada, b_ada, norm1_g, w_in, lru_conv_w, lru_conv_b, lru_wa, lru_ba, lru_wx, lru_bx, lru_lambda,
              diff_lq1, diff_lk1, diff_lq2, diff_lk2, diff_subln_g, gla_wa2, gla_ba, gla_norm_g,
              w_out, norm2_g, ffn_w_gate, ffn_w_up, ffn_conv_w, ffn_conv_b, ffn_w_down, final_norm_g):
    pos_p = jnp.arange(x_prompt.shape[1], dtype=jnp.int32)
    past_len = page_table.shape[1] * PAGE_SIZE
    pos_s = past_len + jnp.arange(x_sample.shape[1], dtype=jnp.int32)
    xp, xs = x_prompt, x_sample
    st_p, st_s = [], []
    for l in range(DEPTH):
        lp = dict(
            w_ada=w_ada[l], b_ada=b_ada[l], norm1_g=norm1_g[l], w_in=w_in[l],
            lru_conv_w=lru_conv_w[l], lru_conv_b=lru_conv_b[l], lru_wa=lru_wa[l], lru_ba=lru_ba[l],
            lru_wx=lru_wx[l], lru_bx=lru_bx[l], lru_lambda=lru_lambda[l],
            diff_lq1=diff_lq1[l], diff_lk1=diff_lk1[l], diff_lq2=diff_lq2[l], diff_lk2=diff_lk2[l],
            diff_subln_g=diff_subln_g[l], gla_wa2=gla_wa2[l], gla_ba=gla_ba[l], gla_norm_g=gla_norm_g[l],
            w_out=w_out[l], norm2_g=norm2_g[l], ffn_w_gate=ffn_w_gate[l], ffn_w_up=ffn_w_up[l],
            ffn_conv_w=ffn_conv_w[l], ffn_conv_b=ffn_conv_b[l], ffn_w_down=ffn_w_down[l],
        )
        xp, new_p = decoder_layer(xp, c_prompt, lp, l, pos_p, None, None)
        past = dict(
            diff_k=cache_diff_k[l], diff_v=cache_diff_v[l], dsa_k=cache_dsa_k[l], dsa_v=cache_dsa_v[l],
            idx_k=cache_idx_k[l], lru_h=state_lru_h[l], lru_conv=state_lru_conv[l], gla=state_gla[l],
            ffn_conv=state_ffn_conv[l],
        )
        xs, new_s = decoder_layer(xs, c_sample, lp, l, pos_s, past, page_table)
        st_p.append(new_p)
        st_s.append(new_s)
    y_prompt = rmsnorm(xp, final_norm_g)
    y_sample = rmsnorm(xs, final_norm_g)
    P = {k: jnp.stack([s[k] for s in st_p]) for k in STATE_KEYS}
    S = {k: jnp.stack([s[k] for s in st_s]) for k in STATE_KEYS}
    return (y_prompt, y_sample,
            P["diff_k"], P["diff_v"], P["dsa_k"], P["dsa_v"], P["idx_k"],
            P["lru_h"], P["lru_conv"], P["gla"], P["ffn_conv"],
            S["diff_k"], S["diff_v"], S["dsa_k"], S["dsa_v"], S["idx_k"],
            S["lru_h"], S["lru_conv"], S["gla"], S["ffn_conv"])
```

```python
import functools
import math

import jax
import jax.numpy as jnp
from jax import lax
from jax.experimental import pallas as pl
from jax.experimental.pallas import tpu as pltpu

F32 = jnp.float32
BF16 = jnp.bfloat16
I32 = jnp.int32

D_MODEL = 1024
GROUP_W = 256
LRU_CONV = 4
LRU_C = 8.0
DIFF_HEADS = 4
DIFF_QK = 32
DSA_HEADS = 4
DSA_HD = 64
IDX_HEADS = 8
IDX_DIM = 32
TOPK_MAX = 256
GLA_HEADS = 4
GLA_DK = 32
GLA_DV = 64
GLA_TAU = 16.0
D_FF = 2816
FFN_CONV = 3
ROPE_THETA = 10000.0
EPS = 1e-6
PAGE = 128

P_LRU, P_DIFF_QK, P_DIFF_V, P_DSA_QK, P_DSA_V, P_IDX, P_GLA_QK, P_GLA_V, P_GLA_G, P_MISC, P_END = (
    0, 512, 1024, 1280, 1792, 2048, 2560, 2816, 3072, 3328, 3456)

VMEM_LIMIT_BYTES = 56 * 1024 * 1024
NEG_BIG = -1e30
INT_MIN = -2 ** 31
KEY_NEG_INF = -2 ** 31 + 0x7FFFFF
BIG_COL = 2 ** 30


def _cparams(*sem):
    return pltpu.CompilerParams(dimension_semantics=sem, vmem_limit_bytes=VMEM_LIMIT_BYTES)


def _rms(x, g):
    return x * lax.rsqrt(jnp.mean(x * x, axis=-1, keepdims=True) + EPS) * g


def _silu(x):
    return x * jax.nn.sigmoid(x)


def _softplus(z):
    return jnp.maximum(z, 0.0) + jnp.log1p(jnp.exp(-jnp.abs(z)))


def _dot(a, b):
    return jnp.dot(a, b, preferred_element_type=F32)


def _dot_nt(a, b):
    return lax.dot_general(a, b, (((1,), (1,)), ((), ())), preferred_element_type=F32)


def _group_mean64(x2, ones_bd):
    hi = x2.astype(BF16)
    lo = (x2 - hi.astype(F32)).astype(BF16)
    return (_dot(hi, ones_bd) + _dot(lo, ones_bd)) * (1.0 / 64.0)


def _rope(x, cos, sin_signed, half):
    n = x.shape[1]
    reps = n // 128
    c = jnp.concatenate([cos] * reps, axis=1) if reps > 1 else cos
    s = jnp.concatenate([sin_signed] * reps, axis=1) if reps > 1 else sin_signed
    lane = lax.broadcasted_iota(I32, x.shape, 1)
    first = (lane % (2 * half)) < half
    swapped = jnp.where(first, pltpu.roll(x, n - half, 1), pltpu.roll(x, half, 1))
    return x * c + swapped * s


def _shift_rows(x, s, prev8):
    r = pltpu.roll(x, s, 0)
    p = pltpu.roll(prev8, s, 0)
    row = lax.broadcasted_iota(I32, (8, x.shape[1]), 0)
    top = jnp.where(row < s, p, r[:8])
    return jnp.concatenate([top, r[8:]], axis=0)


def _cumsum_rows(x):
    n = x.shape[0]
    row = lax.broadcasted_iota(I32, x.shape, 0)
    s = 1
    while s < n:
        x = x + jnp.where(row >= s, pltpu.roll(x, s, 0), 0.0)
        s *= 2
    return x


def _sort_key(x):
    bits = pltpu.bitcast(x, I32)
    return bits ^ ((bits >> 31) & 0x7FFFFFFF)


def _ada_kernel(c_ref, w_ref, b_ref, o_ref):
    c = c_ref[...]
    o_ref[...] = _dot(_silu(c).astype(BF16), w_ref[...].astype(BF16)) + b_ref[...]


def _ada(c_all, w_ada, b_ada):
    depth = w_ada.shape[0]
    n = c_all.shape[0]
    tn = 1024
    return pl.pallas_call(
        _ada_kernel,
        grid=(depth, 6 * D_MODEL // tn),
        in_specs=[pl.BlockSpec((n, D_MODEL), lambda l, j: (0, 0)),
                  pl.BlockSpec((None, D_MODEL, tn), lambda l, j: (l, 0, j)),
                  pl.BlockSpec((None, 1, tn), lambda l, j: (l, 0, j))],
        out_specs=pl.BlockSpec((None, n, tn), lambda l, j: (l, 0, j)),
        out_shape=jax.ShapeDtypeStruct((depth, n, 6 * D_MODEL), F32),
        compiler_params=_cparams("arbitrary", "arbitrary"),
        name="ada_mod",
    )(c_all, w_ada, b_ada.reshape(depth, 1, 6 * D_MODEL))


def _proj_kernel(x_ref, sh_ref, sc_ref, g_ref, w_ref, c32_ref, s32_ref, c64_ref, s64_ref,
                 lru_ref, qb_ref, kb_ref, vb_ref, qc_ref, kc_ref, vc_ref, iq_ref, ikt_ref,
                 gqk_ref, gv_ref, gg_ref, misc_ref):
    x = x_ref[0]
    h = _rms(x, g_ref[...]) * (1.0 + sc_ref[0]) + sh_ref[0]
    hb = h.astype(BF16)

    def mm(a, b):
        return _dot(hb, w_ref[:, a:b])

    lru_ref[0] = mm(P_LRU, P_DIFF_QK)
    c32, s32, c64, s64 = c32_ref[...], s32_ref[...], c64_ref[...], s64_ref[...]
    qk = _rope(mm(P_DIFF_QK, P_DIFF_V), c32, s32, 16)
    qb_ref[0] = qk[:, :256]
    kb_ref[0] = qk[:, 256:]
    vb_ref[0] = mm(P_DIFF_V, P_DSA_QK)
    qk = _rope(mm(P_DSA_QK, P_DSA_V), c64, s64, 32)
    qc_ref[0] = qk[:, :256]
    kc_ref[0] = qk[:, 256:]
    vc_ref[0] = mm(P_DSA_V, P_IDX)
    qk = _rope(mm(P_IDX, P_GLA_QK), c32, s32, 16)
    iq_ref[0] = qk[:, :256]
    ikt_ref[0] = qk[:, 256:]
    gqk_ref[0] = mm(P_GLA_QK, P_GLA_V)
    gv_ref[0] = mm(P_GLA_V, P_GLA_G)
    gg_ref[0] = mm(P_GLA_G, P_MISC)
    misc_ref[0] = mm(P_MISC, P_END)


_PROJ_WIDTHS = (512, 256, 256, 256, 256, 256, 256, 256, 256, 256, 256, 256, 128)


def _proj(x, sh, sc, norm_g, w_packed, l, tabs, tm):
    G, T, D = x.shape
    per_row = sh.shape[1] != 1
    mod_spec = (pl.BlockSpec((1, tm, D), lambda b, i: (b, i, 0)) if per_row
                else pl.BlockSpec((1, 1, D), lambda b, i: (b, 0, 0)))
    tab_spec = pl.BlockSpec((tm, 128), lambda b, i: (i, 0))
    return pl.pallas_call(
        _proj_kernel,
        grid=(G, T // tm),
        in_specs=[pl.BlockSpec((1, tm, D), lambda b, i: (b, i, 0)), mod_spec, mod_spec,
                  pl.BlockSpec((None, 1, D), lambda b, i: (l, 0, 0)),
                  pl.BlockSpec((None, D, P_END), lambda b, i: (l, 0, 0)),
                  tab_spec, tab_spec, tab_spec, tab_spec],
        out_specs=[pl.BlockSpec((1, tm, w), lambda b, i: (b, i, 0)) for w in _PROJ_WIDTHS],
        out_shape=[jax.ShapeDtypeStruct((G, T, w), F32) for w in _PROJ_WIDTHS],
        compiler_params=_cparams("arbitrary", "arbitrary"),
        name="in_proj",
    )(x, sh, sc, norm_g, w_packed, *tabs)


def _lru_gates(xa, gate_in, wa_ref, wx_ref, ba_ref, bx_ref, lam_ref):
    xb = xa.astype(BF16)
    ga = jax.nn.sigmoid(_dot(xb, wa_ref[...]) + ba_ref[...])
    gx = jax.nn.sigmoid(_dot(xb, wx_ref[...]) + bx_ref[...])
    log_a = (-LRU_C) * ga * _softplus(-lam_ref[...])
    a = jnp.exp(log_a)
    t = jnp.tanh(log_a)
    mult = jnp.sqrt(-2.0 * t / (1.0 - t))
    return a, mult, gx


def _lru_seq_kernel(lru_ref, cw_ref, cb_ref, wa_ref, wx_ref, ba_ref, bx_ref, lam_ref, h0_ref, buf_ref,
                    o_ref, hl_ref, cs_ref, hc_ref, cbuf_ref, *, tm):
    i = pl.program_id(1)

    @pl.when(i == 0)
    def _():
        hc_ref[...] = jnp.broadcast_to(h0_ref[0], hc_ref.shape)
        cbuf_ref[...] = buf_ref[0]

    xg = lru_ref[0]
    x = xg[:, :256]
    gate = xg[:, 256:]
    prev8 = cbuf_ref[...]
    w = cw_ref[...]
    xa = (cb_ref[...] + x * w[3:4] + _shift_rows(x, 1, prev8) * w[2:3]
          + _shift_rows(x, 2, prev8) * w[1:2] + _shift_rows(x, 3, prev8) * w[0:1])
    cbuf_ref[...] = x[tm - 8:]
    cs_ref[0] = x[tm - 8:]

    a, mult, gx = _lru_gates(xa, gate, wa_ref, wx_ref, ba_ref, bx_ref, lam_ref)
    row = lax.broadcasted_iota(I32, (tm, 256), 0)
    mult = jnp.where((row + i * tm) == 0, 1.0, mult)
    u = mult * gx * xa
    s = 1
    while s < tm:
        keep = row >= s
        a_sh = jnp.where(keep, pltpu.roll(a, s, 0), 1.0)
        u_sh = jnp.where(keep, pltpu.roll(u, s, 0), 0.0)
        u = a * u_sh + u
        a = a * a_sh
        s *= 2
    h = a * hc_ref[0:1, :] + u
    hlast = h[tm - 1:tm]
    hc_ref[...] = jnp.broadcast_to(hlast, hc_ref.shape)
    hl_ref[0] = hlast
    o_ref[0] = h * jax.nn.gelu(gate, approximate=True)


def _lru_seq(lru, lw, l, h0, buf8, tm):
    G, T, _ = lru.shape
    wspec = lambda shape: pl.BlockSpec((None,) + shape, lambda b, i: (l,) + (0,) * len(shape))
    return pl.pallas_call(
        functools.partial(_lru_seq_kernel, tm=tm),
        grid=(G, T // tm),
        in_specs=[pl.BlockSpec((1, tm, 512), lambda b, i: (b, i, 0)),
                  wspec((LRU_CONV, 256)), wspec((1, 256)), wspec((256, 256)), wspec((256, 256)),
                  wspec((1, 256)), wspec((1, 256)), wspec((1, 256)),
                  pl.BlockSpec((1, 1, 256), lambda b, i: (b, 0, 0)),
                  pl.BlockSpec((1, 8, 256), lambda b, i: (b, 0, 0))],
        out_specs=[pl.BlockSpec((1, tm, 256), lambda b, i: (b, i, 0)),
                   pl.BlockSpec((1, 1, 256), lambda b, i: (b, 0, 0)),
                   pl.BlockSpec((1, 8, 256), lambda b, i: (b, 0, 0))],
        out_shape=[jax.ShapeDtypeStruct((G, T, 256), F32), jax.ShapeDtypeStruct((G, 1, 256), F32),
                   jax.ShapeDtypeStruct((G, 8, 256), F32)],
        scratch_shapes=[pltpu.VMEM((8, 256), F32), pltpu.VMEM((8, 256), F32)],
        compiler_params=_cparams("arbitrary", "arbitrary"),
        name="lru_seq",
    )(lru, lw["conv_w"], lw["conv_b"], lw["wa"], lw["wx"], lw["ba"], lw["bx"], lw["lam"], h0, buf8)


def _lru_rows_kernel(xx_ref, gate_ref, cw_ref, cb_ref, wa_ref, wx_ref, ba_ref, bx_ref, lam_ref, h0_ref,
                     o_ref, hl_ref, *, nb, nt):
    xx = xx_ref[...]
    w = cw_ref[...]
    n = nt * nb
    xa = cb_ref[...]
    for j in range(LRU_CONV):
        xa = xa + xx[j * nb:j * nb + n] * w[j:j + 1]
    gate = gate_ref[...]
    a, mult, gx = _lru_gates(xa, gate, wa_ref, wx_ref, ba_ref, bx_ref, lam_ref)
    u = mult * gx * xa
    h = h0_ref[...]
    for t in range(nt):
        h = a[t * nb:(t + 1) * nb] * h + u[t * nb:(t + 1) * nb]
        o_ref[t * nb:(t + 1) * nb, :] = h * jax.nn.gelu(gate[t * nb:(t + 1) * nb], approximate=True)
    hl_ref[...] = h


def _lru_rows(xx, gate, lw, l, h0, nb, nt):
    n = nt * nb
    full = lambda a: pl.BlockSpec(a.shape, lambda i: (0,) * a.ndim)
    wspec = lambda shape: pl.BlockSpec((None,) + shape, lambda i: (l,) + (0,) * len(shape))
    return pl.pallas_call(
        functools.partial(_lru_rows_kernel, nb=nb, nt=nt),
        grid=(1,),
        in_specs=[full(xx), full(gate), wspec((LRU_CONV, 256)), wspec((1, 256)), wspec((256, 256)),
                  wspec((256, 256)), wspec((1, 256)), wspec((1, 256)), wspec((1, 256)), full(h0)],
        out_specs=[pl.BlockSpec((n, 256), lambda i: (0, 0)), pl.BlockSpec((nb, 256), lambda i: (0, 0))],
        out_shape=[jax.ShapeDtypeStruct((n, 256), F32), jax.ShapeDtypeStruct((nb, 256), F32)],
        compiler_params=_cparams("arbitrary"),
        name="lru_rows",
    )(xx, gate, lw["conv_w"], lw["conv_b"], lw["wa"], lw["wx"], lw["ba"], lw["bx"], lw["lam"], h0)


def _diff_finalize(acc_ref, l_ref, lam, osc_ref, g_ref, ones_ref, out_scale, rows, wide):
    for h in range(DIFF_HEADS):
        r1 = slice((2 * h) * rows, (2 * h + 1) * rows)
        r2 = slice((2 * h + 1) * rows, (2 * h + 2) * rows)
        cs = slice(64 * h, 64 * h + 64) if wide else slice(0, 64)
        o1 = acc_ref[r1, cs] / l_ref[r1, :]
        o2 = acc_ref[r2, cs] / l_ref[r2, :]
        osc_ref[:, 64 * h:64 * h + 64] = o1 - lam * o2
    o = osc_ref[...]
    ms = _group_mean64(o * o, ones_ref[...])
    return o * lax.rsqrt(ms + EPS) * g_ref[...] * out_scale


def _diff_attn_kernel(lam_ref, q_ref, k_ref, v_ref, g_ref, ones_ref, o_ref,
                      qm_ref, m_ref, l_ref, acc_ref, osc_ref, *, tq, out_scale):
    i = pl.program_id(1)
    q = q_ref[0]
    lane = lax.broadcasted_iota(I32, (tq, 256), 1)
    for hm in range(8):
        qm_ref[hm] = jnp.where((lane // 32) == hm, q, 0.0).astype(BF16)
    m_ref[...] = jnp.full(m_ref.shape, NEG_BIG, F32)
    l_ref[...] = jnp.zeros(l_ref.shape, F32)
    acc_ref[...] = jnp.zeros(acc_ref.shape, F32)
    row = lax.broadcasted_iota(I32, (tq, tq), 0)
    col = lax.broadcasted_iota(I32, (tq, tq), 1)

    def block(j, masked):
        r0 = pl.multiple_of(j * tq, tq)
        kj = k_ref[0, pl.ds(r0, tq), :].astype(BF16)
        vj = v_ref[0, pl.ds(r0, tq), :].astype(BF16)
        for hm in range(8):
            h = hm // 2
            rs = slice(hm * tq, (hm + 1) * tq)
            s = _dot_nt(qm_ref[hm], kj) * (DIFF_QK ** -0.5)
            if masked:
                s = jnp.where(col <= row, s, NEG_BIG)
            m_prev = m_ref[rs, :]
            m_new = jnp.maximum(m_prev, jnp.max(s, axis=1, keepdims=True))
            alpha = jnp.exp(m_prev - m_new)
            p = jnp.exp(s - m_new)
            l_ref[rs, :] = alpha * l_ref[rs, :] + jnp.sum(p, axis=1, keepdims=True)
            acc_ref[rs, :] = alpha * acc_ref[rs, :] + _dot(p.astype(BF16), vj[:, 64 * h:64 * h + 64])
            m_ref[rs, :] = m_new

    def loop_body(j, carry):
        block(j, False)
        return carry

    lax.fori_loop(0, i, loop_body, 0)
    block(i, True)
    o_ref[0] = _diff_finalize(acc_ref, l_ref, lam_ref[0, 0], osc_ref, g_ref, ones_ref, out_scale, tq, False)


def _diff_attn(lam, q, k, v, g_tiled, ones_bd, l, out_scale, tq):
    G, T, _ = q.shape
    return pl.pallas_call(
        functools.partial(_diff_attn_kernel, tq=tq, out_scale=out_scale),
        grid=(G, T // tq),
        in_specs=[pl.BlockSpec(memory_space=pltpu.SMEM),
                  pl.BlockSpec((1, tq, 256), lambda b, i: (b, i, 0)),
                  pl.BlockSpec((1, T, 256), lambda b, i: (b, 0, 0)),
                  pl.BlockSpec((1, T, 256), lambda b, i: (b, 0, 0)),
                  pl.BlockSpec((None, 1, 256), lambda b, i: (l, 0, 0)),
                  pl.BlockSpec((256, 256), lambda b, i: (0, 0))],
        out_specs=pl.BlockSpec((1, tq, 256), lambda b, i: (b, i, 0)),
        out_shape=jax.ShapeDtypeStruct((G, T, 256), F32),
        scratch_shapes=[pltpu.VMEM((8, tq, 256), BF16), pltpu.VMEM((8 * tq, 1), F32),
                        pltpu.VMEM((8 * tq, 1), F32), pltpu.VMEM((8 * tq, 64), F32),
                        pltpu.VMEM((tq, 256), F32)],
        compiler_params=_cparams("arbitrary", "arbitrary"),
        name="diff_attn",
    )(lam, q, k, v, g_tiled, ones_bd)


def _topk_threshold(count_ge, count_tie_below, topk, n_col_bits, shape):
    kf = float(topk)
    zero = jnp.zeros(shape, I32)
    tau = jnp.where(count_ge(zero) >= kf, zero, zero + INT_MIN)

    def bit_body(b, tau):
        cand = tau + lax.shift_left(jnp.int32(1), 30 - b)
        return jnp.where(count_ge(cand) >= kf, cand, tau)

    tau = lax.fori_loop(0, 31, bit_body, tau)
    cnt_ge = count_ge(tau)
    cnt_gt = count_ge(tau + 1)
    need = kf - cnt_gt
    has_tie = jnp.where(tau > KEY_NEG_INF, jnp.where(cnt_ge - cnt_gt > need, 1.0, 0.0), 0.0)

    def tie_search():
        def tie_body(b, g):
            cand = g + lax.shift_left(jnp.int32(1), n_col_bits - 1 - b)
            return jnp.where(count_tie_below(tau, cand) < need, cand, g)
        return lax.fori_loop(0, n_col_bits, tie_body, zero)

    g_tie = lax.cond(jnp.max(has_tie) > 0.0, tie_search, lambda: zero)
    g = jnp.where(tau == KEY_NEG_INF, -1, jnp.where(has_tie > 0.0, g_tie, BIG_COL))
    return tau, g


def _dsa_kernel(q_ref, k_ref, v_ref, iq_ref, ikt_ref, misc_ref, o_ref,
                qm_ref, iqm_ref, key_ref, m_ref, l_ref, acc_ref, *, tq, topk, n_col_bits):
    i = pl.program_id(1)
    nblk = i + 1
    lane = lax.broadcasted_iota(I32, (tq, 256), 1)
    iq = iq_ref[0]
    for h in range(IDX_HEADS):
        iqm_ref[h] = jnp.where((lane // 32) == h, iq, 0.0).astype(BF16)
    q = q_ref[0]
    for h in range(DSA_HEADS):
        qm_ref[h] = jnp.where((lane // 64) == h, q, 0.0).astype(BF16)
    w = misc_ref[0][:, 0:IDX_HEADS] * (IDX_HEADS ** -0.5)
    row = lax.broadcasted_iota(I32, (tq, tq), 0)
    col = lax.broadcasted_iota(I32, (tq, tq), 1)

    def score_block(j, masked):
        r0 = pl.multiple_of(j * tq, tq)
        ikj = ikt_ref[0, pl.ds(r0, tq), :].astype(BF16)
        tot = jnp.zeros((tq, tq), F32)
        for h in range(IDX_HEADS):
            sc = _dot_nt(iqm_ref[h], ikj) * (IDX_DIM ** -0.5)
            tot = tot + w[:, h:h + 1] * jnp.maximum(sc, 0.0)
        tot = jnp.where(tot == 0.0, 0.0, tot)
        if masked:
            tot = jnp.where(col <= row, tot, -jnp.inf)
        key_ref[j] = _sort_key(tot)

    def score_body(j, carry):
        score_block(j, False)
        return carry

    lax.fori_loop(0, i, score_body, 0)
    score_block(i, True)

    def count_ge(c):
        def body(j, acc):
            ge = jnp.where(key_ref[j] >= c, 1.0, 0.0)
            part = ge[:, 0:128]
            for t in range(1, tq // 128):
                part = part + ge[:, 128 * t:128 * (t + 1)]
            return acc + part
        acc = lax.fori_loop(0, nblk, body, jnp.zeros((tq, 128), F32))
        return jnp.sum(acc, axis=1, keepdims=True)

    def count_tie_below(tau, c):
        def body(j, acc):
            colg = col + j * tq
            hit = jnp.where(key_ref[j] == tau, jnp.where(colg < c, 1.0, 0.0), 0.0)
            part = hit[:, 0:128]
            for t in range(1, tq // 128):
                part = part + hit[:, 128 * t:128 * (t + 1)]
            return acc + part
        acc = lax.fori_loop(0, nblk, body, jnp.zeros((tq, 128), F32))
        return jnp.sum(acc, axis=1, keepdims=True)

    tau, g = _topk_threshold(count_ge, count_tie_below, topk, n_col_bits, (tq, 1))

    m_ref[...] = jnp.full(m_ref.shape, NEG_BIG, F32)
    l_ref[...] = jnp.zeros(l_ref.shape, F32)
    acc_ref[...] = jnp.zeros(acc_ref.shape, F32)

    def att_body(j, carry):
        r0 = pl.multiple_of(j * tq, tq)
        colg = col + j * tq
        sel = key_ref[j] >= tau + jnp.where(colg > g, 1, 0)
        kj = k_ref[0, pl.ds(r0, tq), :].astype(BF16)
        vj = v_ref[0, pl.ds(r0, tq), :].astype(BF16)
        for h in range(DSA_HEADS):
            rs = slice(h * tq, (h + 1) * tq)
            s = _dot_nt(qm_ref[h], kj) * (DSA_HD ** -0.5)
            m_prev = m_ref[rs, :]
            m_new = jnp.maximum(m_prev, jnp.max(jnp.where(sel, s, NEG_BIG), axis=1, keepdims=True))
            alpha = jnp.exp(m_prev - m_new)
            p = jnp.where(sel, jnp.exp(s - m_new), 0.0)
            l_ref[rs, :] = alpha * l_ref[rs, :] + jnp.sum(p, axis=1, keepdims=True)
            acc_ref[rs, :] = alpha * acc_ref[rs, :] + _dot(p.astype(BF16), vj[:, 64 * h:64 * h + 64])
            m_ref[rs, :] = m_new
        return carry

    lax.fori_loop(0, nblk, att_body, 0)
    for h in range(DSA_HEADS):
        rs = slice(h * tq, (h + 1) * tq)
        o_ref[0, :, 64 * h:64 * h + 64] = acc_ref[rs, :] / l_ref[rs, :]


def _dsa_attn(q, k, v, iq, ikt, misc, topk, tq):
    G, T, _ = q.shape
    nb = T // tq
    return pl.pallas_call(
        functools.partial(_dsa_kernel, tq=tq, topk=topk, n_col_bits=max(1, (T - 1).bit_length())),
        grid=(G, nb),
        in_specs=[pl.BlockSpec((1, tq, 256), lambda b, i: (b, i, 0)),
                  pl.BlockSpec((1, T, 256), lambda b, i: (b, 0, 0)),
                  pl.BlockSpec((1, T, 256), lambda b, i: (b, 0, 0)),
                  pl.BlockSpec((1, tq, 256), lambda b, i: (b, i, 0)),
                  pl.BlockSpec((1, T, 256), lambda b, i: (b, 0, 0)),
                  pl.BlockSpec((1, tq, 128), lambda b, i: (b, i, 0))],
        out_specs=pl.BlockSpec((1, tq, 256), lambda b, i: (b, i, 0)),
        out_shape=jax.ShapeDtypeStruct((G, T, 256), F32),
        scratch_shapes=[pltpu.VMEM((DSA_HEADS, tq, 256), BF16), pltpu.VMEM((IDX_HEADS, tq, 256), BF16),
                        pltpu.VMEM((nb, tq, tq), I32), pltpu.VMEM((DSA_HEADS * tq, 1), F32),
                        pltpu.VMEM((DSA_HEADS * tq, 1), F32), pltpu.VMEM((DSA_HEADS * tq, 64), F32)],
        compiler_params=_cparams("arbitrary", "arbitrary"),
        name="dsa_attn",
    )(q, k, v, iq, ikt, misc)


def _gla_kernel(qk_ref, v_ref, g_ref, misc_ref, wa_ref, ba_ref, gn_ref, ones_ref, bd_ref, s0_ref,
                o_ref, sfin_ref, s_ref, *, tm, chunk, n_valid):
    i = pl.program_id(1)

    @pl.when(i == 0)
    def _():
        s_ref[...] = jnp.zeros(s_ref.shape, F32)
        for h in range(GLA_HEADS):
            s_ref[32 * h:32 * h + 32, 64 * h:64 * h + 64] = s0_ref[0, h]

    tri = (lax.broadcasted_iota(I32, (chunk, chunk), 0) >= lax.broadcasted_iota(I32, (chunk, chunk), 1))
    crow = lax.broadcasted_iota(I32, (chunk, 128), 0)

    def chunk_body(c, carry):
        r0 = pl.multiple_of(c * chunk, chunk)
        rows = pl.ds(r0, chunk)
        a_lin = _dot(misc_ref[0, rows, :].astype(BF16), wa_ref[...]) + ba_ref[...]
        la = (jnp.minimum(a_lin, 0.0) - jnp.log1p(jnp.exp(-jnp.abs(a_lin)))) * (1.0 / GLA_TAU)
        qk = qk_ref[0, rows, :]
        q = qk[:, :128] * (GLA_DK ** -0.5)
        k = qk[:, 128:]
        if n_valid is not None:
            live = (crow + r0) < n_valid
            la = jnp.where(live, la, 0.0)
            k = jnp.where(live, k, 0.0)
        v = v_ref[0, rows, :]
        vb = v.astype(BF16)
        b = _cumsum_rows(la)
        bT = b.T
        kT = k.T
        S = s_ref[...]
        o = _dot((q * jnp.exp(b)).astype(BF16), S.astype(BF16))
        outs = []
        for h in range(GLA_HEADS):
            att = jnp.zeros((chunk, chunk), F32)
            for kk in range(GLA_DK):
                L = GLA_DK * h + kk
                att = att + (q[:, L:L + 1] * kT[L:L + 1, :]) * jnp.exp(b[:, L:L + 1] - bT[L:L + 1, :])
            att = jnp.where(tri, att, 0.0)
            outs.append(_dot(att.astype(BF16), vb[:, 64 * h:64 * h + 64]))
        o = o + jnp.concatenate(outs, axis=1)
        bendT = bT[:, chunk - 1:chunk]
        kdT = kT * jnp.exp(bendT - bT)
        s_ref[...] = jnp.exp(bendT) * S + _dot(kdT.astype(BF16), vb) * bd_ref[...]
        ms = _group_mean64(o * o, ones_ref[...])
        o_ref[0, rows, :] = o * lax.rsqrt(ms + EPS) * gn_ref[...] * _silu(g_ref[0, rows, :])
        return carry

    lax.fori_loop(0, tm // chunk, chunk_body, 0)

    @pl.when(i == pl.num_programs(1) - 1)
    def _():
        for h in range(GLA_HEADS):
            sfin_ref[0, h] = s_ref[32 * h:32 * h + 32, 64 * h:64 * h + 64]


def _gla(gqk, gv, gg, misc, gw, l, s0, ones_bd, bd_mask, tm, chunk, n_valid):
    G, T, _ = gqk.shape
    wspec = lambda shape: pl.BlockSpec((None,) + shape, lambda b, i: (l,) + (0,) * len(shape))
    return pl.pallas_call(
        functools.partial(_gla_kernel, tm=tm, chunk=chunk, n_valid=n_valid),
        grid=(G, T // tm),
        in_specs=[pl.BlockSpec((1, tm, 256), lambda b, i: (b, i, 0)),
                  pl.BlockSpec((1, tm, 256), lambda b, i: (b, i, 0)),
                  pl.BlockSpec((1, tm, 256), lambda b, i: (b, i, 0)),
                  pl.BlockSpec((1, tm, 128), lambda b, i: (b, i, 0)),
                  wspec((128, 128)), wspec((1, 128)), wspec((1, 256)),
                  pl.BlockSpec((256, 256), lambda b, i: (0, 0)),
                  pl.BlockSpec((128, 256), lambda b, i: (0, 0)),
                  pl.BlockSpec((1, GLA_HEADS, GLA_DK, GLA_DV), lambda b, i: (b, 0, 0, 0))],
        out_specs=[pl.BlockSpec((1, tm, 256), lambda b, i: (b, i, 0)),
                   pl.BlockSpec((1, GLA_HEADS, GLA_DK, GLA_DV), lambda b, i: (b, 0, 0, 0))],
        out_shape=[jax.ShapeDtypeStruct((G, T, 256), F32),
                   jax.ShapeDtypeStruct((G, GLA_HEADS, GLA_DK, GLA_DV), F32)],
        scratch_shapes=[pltpu.VMEM((128, 256), F32)],
        compiler_params=_cparams("arbitrary", "arbitrary"),
        name="gla",
    )(gqk, gv, gg, misc, gw["wa"], gw["ba"], gw["gn"], ones_bd, bd_mask, s0)


def _ffn_kernel(*refs, tm, seq, final, nb):
    (x_ref, oa_ref, ob_ref, oc_ref, od_ref, g1_ref, sh2_ref, sc2_ref, g2_ref, n2_ref,
     wout_ref, wg_ref, wu_ref, wd_ref, cw_ref, cb_ref, buf_ref, fng_ref, y_ref, cs_ref) = refs[:20]
    x = x_ref[0]
    mix = _dot(oa_ref[0].astype(BF16), wout_ref[0:256, :])
    mix = mix + _dot(ob_ref[0].astype(BF16), wout_ref[256:512, :])
    mix = mix + _dot(oc_ref[0].astype(BF16), wout_ref[512:768, :])
    mix = mix + _dot(od_ref[0].astype(BF16), wout_ref[768:1024, :])
    x1 = x + g1_ref[0] * mix
    h2 = _rms(x1, n2_ref[...]) * (1.0 + sc2_ref[0]) + sh2_ref[0]
    hb = h2.astype(BF16)
    gate = _dot(hb, wg_ref[...])
    w = cw_ref[...]
    if seq:
        cbuf_ref = refs[20]

        @pl.when(pl.program_id(1) == 0)
        def _():
            cbuf_ref[...] = buf_ref[0]

        prev8 = cbuf_ref[...]
        z = cb_ref[...] + gate * w[2:3] + _shift_rows(gate, 1, prev8) * w[1:2] + _shift_rows(gate, 2, prev8) * w[0:1]
        cbuf_ref[...] = gate[tm - 8:]
        cs_ref[0] = gate[tm - 8:]
    else:
        xx = jnp.concatenate([buf_ref[0], gate], axis=0)
        z = cb_ref[...] + xx[0:tm] * w[0:1] + xx[nb:nb + tm] * w[1:2] + xx[2 * nb:2 * nb + tm] * w[2:3]
        cs_ref[0] = xx[tm:tm + 2 * nb]
    up = _dot(hb, wu_ref[...])
    y = _dot((_silu(z) * up).astype(BF16), wd_ref[...])
    x2 = x1 + g2_ref[0] * y
    if final:
        x2 = _rms(x2, fng_ref[...])
    y_ref[0] = x2


def _ffn(x, o4, mods, fw, l, buf, final_g, tm, seq, final, nb=0):
    G, T, D = x.shape
    per_row = mods[0].shape[1] != 1
    mod_spec = (pl.BlockSpec((1, tm, D), lambda b, i: (b, i, 0)) if per_row
                else pl.BlockSpec((1, 1, D), lambda b, i: (b, 0, 0)))
    o_spec = pl.BlockSpec((1, tm, 256), lambda b, i: (b, i, 0))
    once = pl.Buffered(1)
    wspec = lambda shape: pl.BlockSpec((None,) + shape, lambda b, i: (l,) + (0,) * len(shape), pipeline_mode=once)
    nbuf = buf.shape[1]
    return pl.pallas_call(
        functools.partial(_ffn_kernel, tm=tm, seq=seq, final=final, nb=nb),
        grid=(G, T // tm),
        in_specs=[pl.BlockSpec((1, tm, D), lambda b, i: (b, i, 0)), o_spec, o_spec, o_spec, o_spec,
                  mod_spec, mod_spec, mod_spec, mod_spec,
                  wspec((1, D)), wspec((D, D)), wspec((D, D_FF)), wspec((D, D_FF)), wspec((D_FF, D)),
                  wspec((FFN_CONV, D_FF)), wspec((1, D_FF)),
                  pl.BlockSpec((1, nbuf, D_FF), lambda b, i: (b, 0, 0)),
                  pl.BlockSpec((1, D), lambda b, i: (0, 0))],
        out_specs=[pl.BlockSpec((1, tm, D), lambda b, i: (b, i, 0)),
                   pl.BlockSpec((1, nbuf, D_FF), lambda b, i: (b, 0, 0))],
        out_shape=[jax.ShapeDtypeStruct((G, T, D), F32), jax.ShapeDtypeStruct((G, nbuf, D_FF), F32)],
        scratch_shapes=[pltpu.VMEM((8, D_FF), F32)] if seq else [],
        compiler_params=_cparams("arbitrary", "arbitrary"),
        name="out_ffn",
    )(x, *o4, *mods, fw["n2"], fw["wout"], fw["wg"], fw["wu"], fw["wd"], fw["cw"], fw["cb"], buf, final_g)


def _diff_dec_kernel(pt_ref, lam_ref, q_ref, kT_ref, vT_ref, kn_ref, vn_ref, g_ref, ones_ref, o_ref,
                     qs_ref, m_ref, l_ref, acc_ref, osc_ref, *, n_new, out_scale):
    p = pl.program_id(1)
    lane = lax.broadcasted_iota(I32, (8, 256), 1)

    @pl.when(p == 0)
    def _():
        q = q_ref[0]
        for hm in range(8):
            qs_ref[8 * hm:8 * hm + 8, :] = jnp.where((lane // 32) == hm, q, 0.0).astype(BF16)
        m_ref[...] = jnp.full(m_ref.shape, NEG_BIG, F32)
        l_ref[...] = jnp.zeros(l_ref.shape, F32)
        acc_ref[...] = jnp.zeros(acc_ref.shape, F32)

    def update(s, v_contract):
        m_prev = m_ref[...]
        m_new = jnp.maximum(m_prev, jnp.max(s, axis=1, keepdims=True))
        alpha = jnp.exp(m_prev - m_new)
        pr = jnp.exp(s - m_new)
        l_ref[...] = alpha * l_ref[...] + jnp.sum(pr, axis=1, keepdims=True)
        acc_ref[...] = alpha * acc_ref[...] + v_contract(pr.astype(BF16))
        m_ref[...] = m_new

    qs = qs_ref[...]
    s = _dot(qs, kT_ref[0, 0].astype(BF16)) * (DIFF_QK ** -0.5)
    vT = vT_ref[0, 0].astype(BF16)
    update(s, lambda pr: _dot_nt(pr, vT))

    @pl.when(p == pl.num_programs(1) - 1)
    def _():
        s = _dot_nt(qs, kn_ref[0].astype(BF16)) * (DIFF_QK ** -0.5)
        t = lax.broadcasted_iota(I32, s.shape, 0) % 8
        c = lax.broadcasted_iota(I32, s.shape, 1)
        s = jnp.where((c <= t) & (c < n_new), s, NEG_BIG)
        vn = vn_ref[0].astype(BF16)
        update(s, lambda pr: _dot(pr, vn))
        o_ref[0] = _diff_finalize(acc_ref, l_ref, lam_ref[0, 0], osc_ref, g_ref, ones_ref, out_scale, 8, True)


def _diff_dec(page_table, lam, q8, ckT, cvT, kn, vn, g_tiled, ones_bd, l, out_scale, n_new):
    B, NP = page_table.shape
    gs = pltpu.PrefetchScalarGridSpec(
        num_scalar_prefetch=1, grid=(B, NP),
        in_specs=[pl.BlockSpec(memory_space=pltpu.SMEM),
                  pl.BlockSpec((1, 8, 256), lambda b, p, pt: (b, 0, 0)),
                  pl.BlockSpec((1, 1, 256, PAGE), lambda b, p, pt: (l, pt[b, p], 0, 0)),
                  pl.BlockSpec((1, 1, 256, PAGE), lambda b, p, pt: (l, pt[b, p], 0, 0)),
                  pl.BlockSpec((1, 128, 256), lambda b, p, pt: (b, 0, 0)),
                  pl.BlockSpec((1, 128, 256), lambda b, p, pt: (b, 0, 0)),
                  pl.BlockSpec((None, 1, 256), lambda b, p, pt: (l, 0, 0)),
                  pl.BlockSpec((256, 256), lambda b, p, pt: (0, 0))],
        out_specs=pl.BlockSpec((1, 8, 256), lambda b, p, pt: (b, 0, 0)),
        scratch_shapes=[pltpu.VMEM((64, 256), BF16), pltpu.VMEM((64, 1), F32), pltpu.VMEM((64, 1), F32),
                        pltpu.VMEM((64, 256), F32), pltpu.VMEM((8, 256), F32)])
    return pl.pallas_call(
        functools.partial(_diff_dec_kernel, n_new=n_new, out_scale=out_scale),
        grid_spec=gs, out_shape=jax.ShapeDtypeStruct((B, 8, 256), F32),
        compiler_params=_cparams("arbitrary", "arbitrary"),
        name="diff_decode",
    )(page_table, lam, q8, ckT, cvT, kn, vn, g_tiled, ones_bd)


def _idx_dec_kernel(pt_ref, iq_ref, w_ref, ikT_ref, iknT_ref, keys_ref, keysn_ref, *, n_new):
    p = pl.program_id(1)
    iq = iq_ref[0].astype(BF16)
    w = w_ref[0]

    def scores(ikT):
        sc = _dot(iq, ikT.astype(BF16)) * (IDX_DIM ** -0.5)
        tot = jnp.sum((w * jnp.maximum(sc, 0.0)).reshape(8, 8, PAGE), axis=1)
        return jnp.where(tot == 0.0, 0.0, tot)

    keys_ref[0, 0] = _sort_key(scores(ikT_ref[0, 0]))

    @pl.when(p == pl.num_programs(1) - 1)
    def _():
        tot = scores(iknT_ref[0])
        t = lax.broadcasted_iota(I32, tot.shape, 0)
        c = lax.broadcasted_iota(I32, tot.shape, 1)
        keysn_ref[0] = _sort_key(jnp.where((c <= t) & (c < n_new), tot, -jnp.inf))


def _idx_dec(page_table, iq64, w64, cikT, iknT, l, n_new):
    B, NP = page_table.shape
    gs = pltpu.PrefetchScalarGridSpec(
        num_scalar_prefetch=1, grid=(B, NP),
        in_specs=[pl.BlockSpec((1, 64, IDX_DIM), lambda b, p, pt: (b, 0, 0)),
                  pl.BlockSpec((1, 64, 1), lambda b, p, pt: (b, 0, 0)),
                  pl.BlockSpec((1, 1, IDX_DIM, PAGE), lambda b, p, pt: (l, pt[b, p], 0, 0)),
                  pl.BlockSpec((1, IDX_DIM, 128), lambda b, p, pt: (b, 0, 0))],
        out_specs=[pl.BlockSpec((1, 1, 8, PAGE), lambda b, p, pt: (b, p, 0, 0)),
                   pl.BlockSpec((1, 8, 128), lambda b, p, pt: (b, 0, 0))])
    return pl.pallas_call(
        functools.partial(_idx_dec_kernel, n_new=n_new),
        grid_spec=gs,
        out_shape=[jax.ShapeDtypeStruct((B, NP, 8, PAGE), I32), jax.ShapeDtypeStruct((B, 8, 128), I32)],
        compiler_params=_cparams("arbitrary", "arbitrary"),
        name="idx_decode",
    )(page_table, iq64, w64, cikT, iknT)


def _dsa_dec_kernel(pt_ref, q_ref, keys_ref, keysn_ref, kT_ref, vT_ref, kn_ref, vn_ref, o_ref,
                    qs_ref, tg_ref, m_ref, l_ref, acc_ref, *, topk, n_pages, n_col_bits):
    p = pl.program_id(1)
    lane = lax.broadcasted_iota(I32, (8, 256), 1)
    colp = lax.broadcasted_iota(I32, (8, PAGE), 1)

    @pl.when(p == 0)
    def _():
        q = q_ref[0]
        for h in range(DSA_HEADS):
            qs_ref[8 * h:8 * h + 8, :] = jnp.where((lane // 64) == h, q, 0.0).astype(BF16)
        m_ref[...] = jnp.full(m_ref.shape, NEG_BIG, F32)
        l_ref[...] = jnp.zeros(l_ref.shape, F32)
        acc_ref[...] = jnp.zeros(acc_ref.shape, F32)

        def count_ge(c):
            def body(j, acc):
                return acc + jnp.where(keys_ref[0, j] >= c, 1.0, 0.0)
            acc = lax.fori_loop(0, n_pages, body, jnp.where(keysn_ref[0] >= c, 1.0, 0.0))
            return jnp.sum(acc, axis=1, keepdims=True)

        def count_tie_below(tau, c):
            def body(j, acc):
                return acc + jnp.where(keys_ref[0, j] == tau, jnp.where(colp + j * PAGE < c, 1.0, 0.0), 0.0)
            first = jnp.where(keysn_ref[0] == tau, jnp.where(colp + n_pages * PAGE < c, 1.0, 0.0), 0.0)
            acc = lax.fori_loop(0, n_pages, body, first)
            return jnp.sum(acc, axis=1, keepdims=True)

        tau, g = _topk_threshold(count_ge, count_tie_below, topk, n_col_bits, (8, 1))
        tg_ref[:, 0:1] = tau
        tg_ref[:, 1:2] = g

    tau = tg_ref[:, 0:1]
    g = tg_ref[:, 1:2]

    def update(s, key8, thr8, v_contract):
        sel = jnp.concatenate([key8] * DSA_HEADS, axis=0) >= jnp.concatenate([thr8] * DSA_HEADS, axis=0)
        m_prev = m_ref[...]
        m_new = jnp.maximum(m_prev, jnp.max(jnp.where(sel, s, NEG_BIG), axis=1, keepdims=True))
        alpha = jnp.exp(m_prev - m_new)
        pr = jnp.where(sel, jnp.exp(s - m_new), 0.0)
        l_ref[...] = alpha * l_ref[...] + jnp.sum(pr, axis=1, keepdims=True)
        acc_ref[...] = alpha * acc_ref[...] + v_contract(pr.astype(BF16))
        m_ref[...] = m_new

    qs = qs_ref[...]
    thr8 = tau + jnp.where(colp + p * PAGE > g, 1, 0)
    s = _dot(qs, kT_ref[0, 0].astype(BF16)) * (DSA_HD ** -0.5)
    vT = vT_ref[0, 0].astype(BF16)
    update(s, keys_ref[0, p], thr8, lambda pr: _dot_nt(pr, vT))

    @pl.when(p == pl.num_programs(1) - 1)
    def _():
        thrn = tau + jnp.where(colp + n_pages * PAGE > g, 1, 0)
        s = _dot_nt(qs, kn_ref[0].astype(BF16)) * (DSA_HD ** -0.5)
        vn = vn_ref[0].astype(BF16)
        update(s, keysn_ref[0], thrn, lambda pr: _dot(pr, vn))
        for h in range(DSA_HEADS):
            o_ref[0, :, 64 * h:64 * h + 64] = (acc_ref[8 * h:8 * h + 8, 64 * h:64 * h + 64]
                                               / l_ref[8 * h:8 * h + 8, :])


def _dsa_dec(page_table, q8, keys, keysn, ckT, cvT, kn, vn, l, topk):
    B, NP = page_table.shape
    n_keys = NP * PAGE + 128
    gs = pltpu.PrefetchScalarGridSpec(
        num_scalar_prefetch=1, grid=(B, NP),
        in_specs=[pl.BlockSpec((1, 8, 256), lambda b, p, pt: (b, 0, 0)),
                  pl.BlockSpec((1, NP, 8, PAGE), lambda b, p, pt: (b, 0, 0, 0)),
                  pl.BlockSpec((1, 8, 128), lambda b, p, pt: (b, 0, 0)),
                  pl.BlockSpec((1, 1, 256, PAGE), lambda b, p, pt: (l, pt[b, p], 0, 0)),
                  pl.BlockSpec((1, 1, 256, PAGE), lambda b, p, pt: (l, pt[b, p], 0, 0)),
                  pl.BlockSpec((1, 128, 256), lambda b, p, pt: (b, 0, 0)),
                  pl.BlockSpec((1, 128, 256), lambda b, p, pt: (b, 0, 0))],
        out_specs=pl.BlockSpec((1, 8, 256), lambda b, p, pt: (b, 0, 0)),
        scratch_shapes=[pltpu.VMEM((32, 256), BF16), pltpu.VMEM((8, 128), I32), pltpu.VMEM((32, 1), F32),
                        pltpu.VMEM((32, 1), F32), pltpu.VMEM((32, 256), F32)])
    return pl.pallas_call(
        functools.partial(_dsa_dec_kernel, topk=topk, n_pages=NP, n_col_bits=(n_keys - 1).bit_length()),
        grid_spec=gs, out_shape=jax.ShapeDtypeStruct((B, 8, 256), F32),
        compiler_params=_cparams("arbitrary", "arbitrary"),
        name="dsa_decode",
    )(page_table, q8, keys, keysn, ckT, cvT, kn, vn)


def _rope_tables(pos, half):
    inv = ROPE_THETA ** (-jnp.arange(half, dtype=F32) / half)
    ang = pos.astype(F32)[:, None] * inv[None, :]
    cos, sin = jnp.cos(ang), jnp.sin(ang)
    reps = 128 // (2 * half)
    return (jnp.tile(jnp.concatenate([cos, cos], axis=1), (1, reps)),
            jnp.tile(jnp.concatenate([-sin, sin], axis=1), (1, reps)))


def _pack_w_in(w_in):
    names = (("lru_x", 256), ("lru_gate", 256), ("diff_q", 256), ("diff_k", 256), ("diff_v", 256),
             ("dsa_q", 256), ("dsa_k", 256), ("dsa_v", 256), ("idx_q", 256), ("idx_k", 32), ("idx_w", 8),
             ("gla_q", 128), ("gla_k", 128), ("gla_v", 256), ("gla_g", 256), ("gla_a", 16))
    seg, off = {}, 0
    for n, w in names:
        seg[n] = w_in[..., off:off + w]
        off += w
    pad = jnp.zeros(w_in.shape[:-1] + (128 - 24,), w_in.dtype)
    cols = [seg["lru_x"], seg["lru_gate"], seg["diff_q"], seg["diff_k"], seg["diff_v"],
            seg["dsa_q"], seg["dsa_k"], seg["dsa_v"], seg["idx_q"]] + [seg["idx_k"]] * IDX_HEADS + [
            seg["gla_q"], seg["gla_k"], seg["gla_v"], seg["gla_g"], seg["idx_w"], seg["gla_a"], pad]
    return jnp.concatenate(cols, axis=-1).astype(BF16)


def _block_diag(w):
    depth, nb, bw, _ = w.shape
    eye = jnp.eye(nb, dtype=w.dtype)
    return jnp.einsum("lnij,nm->lnimj", w, eye).reshape(depth, nb * bw, nb * bw)


def _group_ones(n, group):
    r = jnp.arange(n) // group
    return (r[:, None] == r[None, :]).astype(BF16)


def _prompt_layer(x, mod, l, W, tabs, final, tm=256, tq=256):
    G, T, _ = x.shape
    sh1, sc1, g1, sh2, sc2, g2 = [m[:, None, :] for m in jnp.split(mod, 6, axis=-1)]
    (lru, qb, kb, vb, qc, kc, vc, iq, ikt, gqk, gv, gg, misc) = _proj(
        x, sh1, sc1, W["norm1"], W["w_in"], l, tabs, tm)
    zeros = lambda *s: jnp.zeros(s, F32)
    o_a, h_last, cs8 = _lru_seq(lru, W["lru"], l, zeros(G, 1, 256), zeros(G, 8, 256), tm)
    o_b = _diff_attn(W["lam"][l], qb, kb, vb, W["subln"], W["ones64"], l, W["out_scale"][l], tq)
    o_c = _dsa_attn(qc, kc, vc, iq, ikt, misc, min(TOPK_MAX, T // 4), tq)
    o_d, s_last = _gla(gqk, gv, gg, misc, W["gla"], l, zeros(G, GLA_HEADS, GLA_DK, GLA_DV),
                       W["ones64"], W["bd_mask"], tm, 64, None)
    y, fcs8 = _ffn(x, (o_a, o_b, o_c, o_d), (g1, sh2, sc2, g2), W["ffn"], l, zeros(G, 8, D_FF),
                   W["final_g"], tm, True, final)
    new = dict(
        diff_k=kb.reshape(G, T, DIFF_HEADS, 2, DIFF_QK), diff_v=vb.reshape(G, T, DIFF_HEADS, 64),
        dsa_k=kc.reshape(G, T, DSA_HEADS, DSA_HD), dsa_v=vc.reshape(G, T, DSA_HEADS, DSA_HD),
        idx_k=ikt[:, :, :IDX_DIM], lru_h=h_last[:, 0], lru_conv=cs8[:, 8 - (LRU_CONV - 1):],
        gla=s_last, ffn_conv=fcs8[:, 8 - (FFN_CONV - 1):])
    return y, new


def _pad_rows(a, n):
    return jnp.pad(a, ((0, 0), (0, n - a.shape[1])) + ((0, 0),) * (a.ndim - 2))


def _sample_layer(xr, mod, l, W, tabs, caches, states, page_table, final, B, Tn):
    n = Tn * B
    tm_major = lambda a: jnp.tile(a, (Tn, 1))[None]
    sh1, sc1, g1, sh2, sc2, g2 = [tm_major(m) for m in jnp.split(mod, 6, axis=-1)]
    outs = _proj(xr, sh1, sc1, W["norm1"], W["w_in"], l, tabs, n)
    (lru, qb, kb, vb, qc, kc, vc, iq, ikt, gqk, gv, gg, misc) = [o[0] for o in outs]
    to_b = lambda a: a.reshape(Tn, B, -1).transpose(1, 0, 2)
    to_t = lambda a: a.transpose(1, 0, 2).reshape(n, -1)

    conv_prev = states["lru_conv"][l].transpose(1, 0, 2).reshape((LRU_CONV - 1) * B, 256)
    xx = jnp.concatenate([conv_prev, lru[:, :256]], axis=0)
    o_a, h_last = _lru_rows(xx, lru[:, 256:], W["lru"], l, states["lru_h"][l], B, Tn)
    new_conv = xx[Tn * B:].reshape(LRU_CONV - 1, B, 256).transpose(1, 0, 2)

    kb_b, vb_b = to_b(kb), to_b(vb)
    o_b = _diff_dec(page_table, W["lam"][l], _pad_rows(to_b(qb), 8), caches["diff_kT"], caches["diff_vT"],
                    _pad_rows(kb_b, 128), _pad_rows(vb_b, 128), W["subln"], W["ones64"], l,
                    W["out_scale"][l], Tn)[:, :Tn]

    past_len = page_table.shape[1] * PAGE
    topk = min(TOPK_MAX, (past_len + Tn) // 4)
    kc_b, vc_b = to_b(kc), to_b(vc)
    ik_b = to_b(ikt)[:, :, :IDX_DIM]
    iq64 = _pad_rows(to_b(iq), 8).reshape(B, 8 * IDX_HEADS, IDX_DIM)
    w64 = (_pad_rows(to_b(misc)[:, :, :IDX_HEADS], 8) * (IDX_HEADS ** -0.5)).reshape(B, 8 * IDX_HEADS, 1)
    iknT = _pad_rows(ik_b, 128).transpose(0, 2, 1)
    keys, keysn = _idx_dec(page_table, iq64, w64, caches["idx_kT"], iknT, l, Tn)
    o_c = _dsa_dec(page_table, _pad_rows(to_b(qc), 8), keys, keysn, caches["dsa_kT"], caches["dsa_vT"],
                   _pad_rows(kc_b, 128), _pad_rows(vc_b, 128), l, topk)[:, :Tn]

    pad16 = lambda a: _pad_rows(to_b(a), 16)
    o_d, s_last = _gla(pad16(gqk), pad16(gv), pad16(gg), pad16(misc), W["gla"], l, states["gla"][l],
                       W["ones64"], W["bd_mask"], 16, 16, Tn)
    o_d = o_d[:, :Tn]

    ffn_prev = states["ffn_conv"][l].transpose(1, 0, 2).reshape(1, (FFN_CONV - 1) * B, D_FF)
    o4 = (o_a[None], to_t(o_b)[None], to_t(o_c)[None], to_t(o_d)[None])
    y, fcs = _ffn(xr, o4, (g1, sh2, sc2, g2), W["ffn"], l, ffn_prev, W["final_g"], n, False, final, nb=B)
    new = dict(
        diff_k=kb_b.reshape(B, Tn, DIFF_HEADS, 2, DIFF_QK), diff_v=vb_b.reshape(B, Tn, DIFF_HEADS, 64),
        dsa_k=kc_b.reshape(B, Tn, DSA_HEADS, DSA_HD), dsa_v=vc_b.reshape(B, Tn, DSA_HEADS, DSA_HD),
        idx_k=ik_b, lru_h=h_last, lru_conv=new_conv, gla=s_last,
        ffn_conv=fcs[0].reshape(FFN_CONV - 1, B, D_FF).transpose(1, 0, 2))
    return y, new


def kernel(x_prompt, x_sample, cache_diff_k, cache_diff_v, cache_dsa_k, cache_dsa_v, cache_idx_k, state_lru_h, state_lru_conv, state_gla, state_ffn_conv, page_table, c_prompt, c_sample, w_ada, b_ada, norm1_g, w_in, lru_conv_w, lru_conv_b, lru_wa, lru_ba, lru_wx, lru_bx, lru_lambda, diff_lq1, diff_lk1, diff_lq2, diff_lk2, diff_subln_g, gla_wa2, gla_ba, gla_norm_g, w_out, norm2_g, ffn_w_gate, ffn_w_up, ffn_conv_w, ffn_conv_b, ffn_w_down, final_norm_g):
    depth = w_in.shape[0]
    Bp, T, D = x_prompt.shape
    Bs, Tn, _ = x_sample.shape
    n_pool = cache_diff_k.shape[1]
    past_len = page_table.shape[1] * PAGE

    lam_init = jnp.asarray([0.8 - 0.6 * math.exp(-0.3 * l) for l in range(depth)], F32)
    lam = (jnp.exp(jnp.sum(diff_lq1 * diff_lk1, axis=-1)) - jnp.exp(jnp.sum(diff_lq2 * diff_lk2, axis=-1))
           + lam_init).reshape(depth, 1, 1)
    row3 = lambda a: a.reshape(depth, 1, a.shape[-1])
    wa2_pad = jnp.zeros((depth, 128, 128), F32).at[:, IDX_HEADS:IDX_HEADS + gla_wa2.shape[1], :].set(gla_wa2)
    W = dict(
        norm1=row3(norm1_g), w_in=_pack_w_in(w_in),
        lru=dict(conv_w=lru_conv_w, conv_b=row3(lru_conv_b), wa=_block_diag(lru_wa).astype(BF16),
                 wx=_block_diag(lru_wx).astype(BF16), ba=row3(lru_ba), bx=row3(lru_bx), lam=row3(lru_lambda)),
        lam=lam, out_scale=[1.0 - (0.8 - 0.6 * math.exp(-0.3 * l)) for l in range(depth)],
        subln=row3(jnp.tile(diff_subln_g, (1, DIFF_HEADS))), ones64=_group_ones(256, 64),
        bd_mask=(jnp.arange(128)[:, None] // GLA_DK == jnp.arange(256)[None, :] // GLA_DV).astype(F32),
        gla=dict(wa=wa2_pad.astype(BF16), ba=row3(gla_ba), gn=row3(jnp.tile(gla_norm_g, (1, GLA_HEADS)))),
        ffn=dict(n2=row3(norm2_g), wout=w_out.astype(BF16), wg=ffn_w_gate.astype(BF16),
                 wu=ffn_w_up.astype(BF16), wd=ffn_w_down.astype(BF16), cw=ffn_conv_w, cb=row3(ffn_conv_b)),
        final_g=final_norm_g.reshape(1, D))

    mods = _ada(jnp.concatenate([c_prompt, c_sample], axis=0), w_ada, b_ada)

    pos_p = jnp.arange(T, dtype=I32)
    pos_s = jnp.repeat(past_len + jnp.arange(Tn, dtype=I32), Bs)
    tabs_p = _rope_tables(pos_p, 16) + _rope_tables(pos_p, 32)
    tabs_s = _rope_tables(pos_s, 16) + _rope_tables(pos_s, 32)

    caches = dict(
        diff_kT=cache_diff_k.transpose(0, 1, 3, 4, 5, 2).reshape(depth, n_pool, 256, PAGE),
        diff_vT=cache_diff_v.transpose(0, 1, 3, 4, 2).reshape(depth, n_pool, 256, PAGE),
        dsa_kT=cache_dsa_k.transpose(0, 1, 3, 4, 2).reshape(depth, n_pool, 256, PAGE),
        dsa_vT=cache_dsa_v.transpose(0, 1, 3, 4, 2).reshape(depth, n_pool, 256, PAGE),
        idx_kT=cache_idx_k.transpose(0, 1, 3, 2))
    states = dict(lru_h=state_lru_h, lru_conv=state_lru_conv, gla=state_gla, ffn_conv=state_ffn_conv)

    xp = x_prompt
    xs = x_sample.transpose(1, 0, 2).reshape(1, Tn * Bs, D)
    st_p, st_s = [], []
    for l in range(depth):
        final = l == depth - 1
        xp, new_p = _prompt_layer(xp, mods[l, :Bp], l, W, tabs_p, final)
        xs, new_s = _sample_layer(xs, mods[l, Bp:], l, W, tabs_s, caches, states, page_table, final, Bs, Tn)
        st_p.append(new_p)
        st_s.append(new_s)
    y_prompt = xp
    y_sample = xs.reshape(Tn, Bs, D).transpose(1, 0, 2)
    keys = ("diff_k", "diff_v", "dsa_k", "dsa_v", "idx_k", "lru_h", "lru_conv", "gla", "ffn_conv")
    P = [jnp.stack([s[k] for s in st_p]) for k in keys]
    S = [jnp.stack([s[k] for s in st_s]) for k in keys]
    return (y_prompt, y_sample, *P, *S)
```

```python
import functools
import math

import jax
import jax.numpy as jnp
from jax import lax
from jax.experimental import pallas as pl
from jax.experimental.pallas import tpu as pltpu

F32 = jnp.float32
BF16 = jnp.bfloat16
I32 = jnp.int32

D_MODEL = 1024
GROUP_W = 256
LRU_CONV = 4
LRU_C = 8.0
DIFF_HEADS = 4
DIFF_QK = 32
DSA_HEADS = 4
DSA_HD = 64
IDX_HEADS = 8
IDX_DIM = 32
TOPK_MAX = 256
GLA_HEADS = 4
GLA_DK = 32
GLA_DV = 64
GLA_TAU = 16.0
GLA_CHUNK = 64
D_FF = 2816
FFN_CONV = 3
ROPE_THETA = 10000.0
EPS = 1e-6
PAGE = 128

P_LRU, P_DIFF_QK, P_DSA_QK, P_IDX, P_GLA_QK, P_GLA_V, P_GLA_G, P_MISC, P_END = (
    0, 512, 1024, 1536, 2048, 2304, 2560, 2816, 2944)
PAGES_PER_STEP = 16
HEADS_PER_STAGE = 4

VMEM_LIMIT_BYTES = 56 * 1024 * 1024
NEG_BIG = -1e30
LOG2E = 1.4426950408889634
INT_MIN = -2 ** 31
KEY_NEG_INF = -2 ** 31 + 0x7FFFFF
BIG_COL = 2 ** 30


def _cparams(*sem):
    return pltpu.CompilerParams(dimension_semantics=sem, vmem_limit_bytes=VMEM_LIMIT_BYTES)


def _rms(x, g):
    return x * lax.rsqrt(jnp.mean(x * x, axis=-1, keepdims=True) + EPS) * g


def _silu(x):
    return x * jax.nn.sigmoid(x)


def _softplus(z):
    return jnp.maximum(z, 0.0) + jnp.log1p(jnp.exp(-jnp.abs(z)))


def _dot(a, b):
    return jnp.dot(a, b, preferred_element_type=F32)


def _dot_nt(a, b):
    return lax.dot_general(a, b, (((1,), (1,)), ((), ())), preferred_element_type=F32)


def _group_mean64(x2, ones_bd):
    hi = x2.astype(BF16)
    lo = (x2 - hi.astype(F32)).astype(BF16)
    return (_dot(hi, ones_bd) + _dot(lo, ones_bd)) * (1.0 / 64.0)


def _rope(x, cos, sin_signed, half):
    n = x.shape[1]
    reps = n // 128
    c = jnp.concatenate([cos] * reps, axis=1) if reps > 1 else cos
    s = jnp.concatenate([sin_signed] * reps, axis=1) if reps > 1 else sin_signed
    lane = lax.broadcasted_iota(I32, x.shape, 1)
    first = (lane % (2 * half)) < half
    swapped = jnp.where(first, pltpu.roll(x, n - half, 1), pltpu.roll(x, half, 1))
    return x * c + swapped * s


def _shift_rows(x, s, prev8):
    r = pltpu.roll(x, s, 0)
    p = pltpu.roll(prev8, s, 0)
    row = lax.broadcasted_iota(I32, (8, x.shape[1]), 0)
    top = jnp.where(row < s, p, r[:8])
    return jnp.concatenate([top, r[8:]], axis=0)


def _cumsum_rows(x):
    n = x.shape[0]
    row = lax.broadcasted_iota(I32, x.shape, 0)
    s = 1
    while s < n:
        x = x + jnp.where(row >= s, pltpu.roll(x, s, 0), 0.0)
        s *= 2
    return x


def _sort_key(x):
    bits = pltpu.bitcast(x, I32)
    return bits ^ ((bits >> 31) & 0x7FFFFFFF)


def _ada_kernel(c_ref, w_ref, b_ref, o_ref):
    c = c_ref[...]
    o_ref[...] = _dot(_silu(c).astype(BF16), w_ref[...].astype(BF16)) + b_ref[...]


def _ada(c_all, w_ada, b_ada):
    depth = w_ada.shape[0]
    n = c_all.shape[0]
    tn = 1024
    return pl.pallas_call(
        _ada_kernel,
        grid=(depth, 6 * D_MODEL // tn),
        in_specs=[pl.BlockSpec((n, D_MODEL), lambda l, j: (0, 0)),
                  pl.BlockSpec((None, D_MODEL, tn), lambda l, j: (l, 0, j)),
                  pl.BlockSpec((None, 1, tn), lambda l, j: (l, 0, j))],
        out_specs=pl.BlockSpec((None, n, tn), lambda l, j: (l, 0, j)),
        out_shape=jax.ShapeDtypeStruct((depth, n, 6 * D_MODEL), F32),
        compiler_params=_cparams("arbitrary", "arbitrary"),
        name="ada_mod",
    )(c_all, w_ada, b_ada.reshape(depth, 1, 6 * D_MODEL))


def _proj_kernel(x_ref, sh_ref, sc_ref, g_ref, w_ref, wvT_ref, c32_ref, s32_ref, c64_ref, s64_ref,
                 lru_ref, qb_ref, kb_ref, vbT_ref, qc_ref, kc_ref, vcT_ref, iq_ref, ikt_ref,
                 gqk_ref, gv_ref, gg_ref, misc_ref):
    x = x_ref[0]
    h = _rms(x, g_ref[...]) * (1.0 + sc_ref[0]) + sh_ref[0]
    hb = h.astype(BF16)

    def mm(a, b):
        return _dot(hb, w_ref[:, a:b])

    lru_ref[0] = mm(P_LRU, P_DIFF_QK)
    c32, s32, c64, s64 = c32_ref[...], s32_ref[...], c64_ref[...], s64_ref[...]
    qk = _rope(mm(P_DIFF_QK, P_DSA_QK), c32, s32, 16)
    qb_ref[0] = qk[:, :256]
    kb_ref[0] = qk[:, 256:]
    qk = _rope(mm(P_DSA_QK, P_IDX), c64, s64, 32)
    qc_ref[0] = qk[:, :256]
    kc_ref[0] = qk[:, 256:]
    qk = _rope(mm(P_IDX, P_GLA_QK), c32, s32, 16)
    iq_ref[0] = qk[:, :256]
    ikt_ref[0] = qk[:, 256:]
    gqk_ref[0] = mm(P_GLA_QK, P_GLA_V)
    gv_ref[0] = mm(P_GLA_V, P_GLA_G)
    gg_ref[0] = mm(P_GLA_G, P_MISC)
    misc_ref[0] = mm(P_MISC, P_END)
    vbT_ref[0] = _dot_nt(wvT_ref[0:256, :], hb)
    vcT_ref[0] = _dot_nt(wvT_ref[256:512, :], hb)


_PROJ_WIDTHS = (512, 256, 256, None, 256, 256, None, 256, 256, 256, 256, 256, 128)


def _proj(x, sh, sc, norm_g, w_packed, wvT, l, tabs, tm):
    G, T, D = x.shape
    per_row = sh.shape[1] != 1
    mod_spec = (pl.BlockSpec((1, tm, D), lambda b, i: (b, i, 0)) if per_row
                else pl.BlockSpec((1, 1, D), lambda b, i: (b, 0, 0)))
    tab_spec = pl.BlockSpec((tm, 128), lambda b, i: (i, 0))
    out_specs = [pl.BlockSpec((1, 256, tm), lambda b, i: (b, 0, i)) if w is None
                 else pl.BlockSpec((1, tm, w), lambda b, i: (b, i, 0)) for w in _PROJ_WIDTHS]
    out_shape = [jax.ShapeDtypeStruct((G, 256, T) if w is None else (G, T, w), F32) for w in _PROJ_WIDTHS]
    return pl.pallas_call(
        _proj_kernel,
        grid=(G, T // tm),
        in_specs=[pl.BlockSpec((1, tm, D), lambda b, i: (b, i, 0)), mod_spec, mod_spec,
                  pl.BlockSpec((None, 1, D), lambda b, i: (l, 0, 0)),
                  pl.BlockSpec((None, D, P_END), lambda b, i: (l, 0, 0)),
                  pl.BlockSpec((None, 512, D), lambda b, i: (l, 0, 0)),
                  tab_spec, tab_spec, tab_spec, tab_spec],
        out_specs=out_specs, out_shape=out_shape,
        compiler_params=_cparams("arbitrary", "arbitrary"),
        name="in_proj",
    )(x, sh, sc, norm_g, w_packed, wvT, *tabs)


def _lru_gates(xa, gate_in, wa_ref, wx_ref, ba_ref, bx_ref, lam_ref):
    xb = xa.astype(BF16)
    ga = jax.nn.sigmoid(_dot(xb, wa_ref[...]) + ba_ref[...])
    gx = jax.nn.sigmoid(_dot(xb, wx_ref[...]) + bx_ref[...])
    log_a = (-LRU_C) * ga * _softplus(-lam_ref[...])
    a = jnp.exp(log_a)
    t = jnp.tanh(log_a)
    mult = jnp.sqrt(-2.0 * t / (1.0 - t))
    return a, mult, gx


def _lru_seq_kernel(lru_ref, cw_ref, cb_ref, wa_ref, wx_ref, ba_ref, bx_ref, lam_ref, h0_ref, buf_ref,
                    o_ref, hl_ref, cs_ref, hc_ref, cbuf_ref, *, tm):
    i = pl.program_id(1)

    @pl.when(i == 0)
    def _():
        hc_ref[...] = jnp.broadcast_to(h0_ref[0], hc_ref.shape)
        cbuf_ref[...] = buf_ref[0]

    xg = lru_ref[0]
    x = xg[:, :256]
    gate = xg[:, 256:]
    prev8 = cbuf_ref[...]
    w = cw_ref[...]
    xa = (cb_ref[...] + x * w[3:4] + _shift_rows(x, 1, prev8) * w[2:3]
          + _shift_rows(x, 2, prev8) * w[1:2] + _shift_rows(x, 3, prev8) * w[0:1])
    cbuf_ref[...] = x[tm - 8:]
    cs_ref[0] = x[tm - 8:]

    a, mult, gx = _lru_gates(xa, gate, wa_ref, wx_ref, ba_ref, bx_ref, lam_ref)
    row = lax.broadcasted_iota(I32, (tm, 256), 0)
    mult = jnp.where((row + i * tm) == 0, 1.0, mult)
    u = mult * gx * xa
    s = 1
    while s < tm:
        keep = row >= s
        a_sh = jnp.where(keep, pltpu.roll(a, s, 0), 1.0)
        u_sh = jnp.where(keep, pltpu.roll(u, s, 0), 0.0)
        u = a * u_sh + u
        a = a * a_sh
        s *= 2
    h = a * hc_ref[0:1, :] + u
    hlast = h[tm - 1:tm]
    hc_ref[...] = jnp.broadcast_to(hlast, hc_ref.shape)
    hl_ref[0] = hlast
    o_ref[0] = h * jax.nn.gelu(gate, approximate=True)


def _lru_seq(lru, lw, l, h0, buf8, tm):
    G, T, _ = lru.shape
    wspec = lambda shape: pl.BlockSpec((None,) + shape, lambda b, i: (l,) + (0,) * len(shape))
    return pl.pallas_call(
        functools.partial(_lru_seq_kernel, tm=tm),
        grid=(G, T // tm),
        in_specs=[pl.BlockSpec((1, tm, 512), lambda b, i: (b, i, 0)),
                  wspec((LRU_CONV, 256)), wspec((1, 256)), wspec((256, 256)), wspec((256, 256)),
                  wspec((1, 256)), wspec((1, 256)), wspec((1, 256)),
                  pl.BlockSpec((1, 1, 256), lambda b, i: (b, 0, 0)),
                  pl.BlockSpec((1, 8, 256), lambda b, i: (b, 0, 0))],
        out_specs=[pl.BlockSpec((1, tm, 256), lambda b, i: (b, i, 0)),
                   pl.BlockSpec((1, 1, 256), lambda b, i: (b, 0, 0)),
                   pl.BlockSpec((1, 8, 256), lambda b, i: (b, 0, 0))],
        out_shape=[jax.ShapeDtypeStruct((G, T, 256), F32), jax.ShapeDtypeStruct((G, 1, 256), F32),
                   jax.ShapeDtypeStruct((G, 8, 256), F32)],
        scratch_shapes=[pltpu.VMEM((8, 256), F32), pltpu.VMEM((8, 256), F32)],
        compiler_params=_cparams("arbitrary", "arbitrary"),
        name="lru_seq",
    )(lru, lw["conv_w"], lw["conv_b"], lw["wa"], lw["wx"], lw["ba"], lw["bx"], lw["lam"], h0, buf8)


def _lru_rows_kernel(xx_ref, gate_ref, cw_ref, cb_ref, wa_ref, wx_ref, ba_ref, bx_ref, lam_ref, h0_ref,
                     o_ref, hl_ref, *, nb, nt):
    xx = xx_ref[...]
    w = cw_ref[...]
    n = nt * nb
    xa = cb_ref[...]
    for j in range(LRU_CONV):
        xa = xa + xx[j * nb:j * nb + n] * w[j:j + 1]
    gate = gate_ref[...]
    a, mult, gx = _lru_gates(xa, gate, wa_ref, wx_ref, ba_ref, bx_ref, lam_ref)
    u = mult * gx * xa
    h = h0_ref[...]
    for t in range(nt):
        h = a[t * nb:(t + 1) * nb] * h + u[t * nb:(t + 1) * nb]
        o_ref[t * nb:(t + 1) * nb, :] = h * jax.nn.gelu(gate[t * nb:(t + 1) * nb], approximate=True)
    hl_ref[...] = h


def _lru_rows(xx, gate, lw, l, h0, nb, nt):
    n = nt * nb
    full = lambda a: pl.BlockSpec(a.shape, lambda i: (0,) * a.ndim)
    wspec = lambda shape: pl.BlockSpec((None,) + shape, lambda i: (l,) + (0,) * len(shape))
    return pl.pallas_call(
        functools.partial(_lru_rows_kernel, nb=nb, nt=nt),
        grid=(1,),
        in_specs=[full(xx), full(gate), wspec((LRU_CONV, 256)), wspec((1, 256)), wspec((256, 256)),
                  wspec((256, 256)), wspec((1, 256)), wspec((1, 256)), wspec((1, 256)), full(h0)],
        out_specs=[pl.BlockSpec((n, 256), lambda i: (0, 0)), pl.BlockSpec((nb, 256), lambda i: (0, 0))],
        out_shape=[jax.ShapeDtypeStruct((n, 256), F32), jax.ShapeDtypeStruct((nb, 256), F32)],
        compiler_params=_cparams("arbitrary"),
        name="lru_rows",
    )(xx, gate, lw["conv_w"], lw["conv_b"], lw["wa"], lw["wx"], lw["ba"], lw["bx"], lw["lam"], h0)


def _diff_finalize(acc_ref, l_ref, lam, osc_ref, g_ref, ones_ref, out_scale, rows, wide):
    for h in range(DIFF_HEADS):
        r1 = slice((2 * h) * rows, (2 * h + 1) * rows)
        r2 = slice((2 * h + 1) * rows, (2 * h + 2) * rows)
        cs = slice(64 * h, 64 * h + 64) if wide else slice(0, 64)
        o1 = acc_ref[r1, cs] / l_ref[r1, :]
        o2 = acc_ref[r2, cs] / l_ref[r2, :]
        osc_ref[:, 64 * h:64 * h + 64] = o1 - lam * o2
    o = osc_ref[...]
    ms = _group_mean64(o * o, ones_ref[...])
    return o * lax.rsqrt(ms + EPS) * g_ref[...] * out_scale


def _softmax_steps(ss, sel, m_ref, l_ref, acc_ref, idxs, vTs, c):
    n = range(len(ss))
    m_prev = [m_ref[i:i + 1, :] for i in idxs]
    l_prev = [l_ref[i:i + 1, :] for i in idxs]
    m_new = [jnp.maximum(m_prev[a], jnp.max(ss[a] if sel is None else jnp.where(sel, ss[a], NEG_BIG),
                                            axis=0, keepdims=True)) for a in n]
    alpha = [jnp.exp2((m_prev[a] - m_new[a]) * c) for a in n]
    p = [jnp.exp2((ss[a] - m_new[a]) * c) for a in n]
    if sel is not None:
        p = [jnp.where(sel, pa, 0.0) for pa in p]
    l_new = [alpha[a] * l_prev[a] + jnp.sum(p[a], axis=0, keepdims=True) for a in n]
    pv = [_dot(vTs[a], p[a].astype(BF16)) for a in n]
    acc_new = [alpha[a] * acc_ref[idxs[a]] + pv[a] for a in n]
    for a in n:
        i = idxs[a]
        m_ref[i:i + 1, :] = m_new[a]
        l_ref[i:i + 1, :] = l_new[a]
        acc_ref[i] = acc_new[a]


def _diff_attn_kernel(lam_ref, q_ref, k_ref, vT_ref, g_ref, o_ref,
                      qmT_ref, m_ref, l_ref, acc_ref, osc_ref, *, tq, out_scale):
    i = pl.program_id(1)
    qT = q_ref[0].T
    feat = lax.broadcasted_iota(I32, (256, tq), 0)
    for hm in range(8):
        qmT_ref[hm] = jnp.where((feat // DIFF_QK) == hm, qT, 0.0).astype(BF16)
    m_ref[...] = jnp.full(m_ref.shape, NEG_BIG, F32)
    l_ref[...] = jnp.zeros(l_ref.shape, F32)
    acc_ref[...] = jnp.zeros(acc_ref.shape, F32)
    krow = lax.broadcasted_iota(I32, (tq, tq), 0)
    qcol = lax.broadcasted_iota(I32, (tq, tq), 1)
    c = (DIFF_QK ** -0.5) * LOG2E

    def block(j, masked):
        r0 = pl.multiple_of(j * tq, tq)
        kj = k_ref[0, pl.ds(r0, tq), :].astype(BF16)
        vjT = vT_ref[0, :, pl.ds(r0, tq)].astype(BF16)
        for h0 in range(0, 8, HEADS_PER_STAGE):
            hms = list(range(h0, h0 + HEADS_PER_STAGE))
            ss = [_dot(kj, qmT_ref[hm]) for hm in hms]
            if masked:
                ss = [jnp.where(krow <= qcol, s, NEG_BIG) for s in ss]
            vTs = [vjT[64 * (hm // 2):64 * (hm // 2) + 64, :] for hm in hms]
            _softmax_steps(ss, None, m_ref, l_ref, acc_ref, hms, vTs, c)

    def loop_body(j, carry):
        block(j, False)
        return carry

    lax.fori_loop(0, i, loop_body, 0)
    block(i, True)
    lam = lam_ref[0, 0]
    for h in range(DIFF_HEADS):
        o = (acc_ref[2 * h] / l_ref[2 * h:2 * h + 1, :]
             - lam * (acc_ref[2 * h + 1] / l_ref[2 * h + 1:2 * h + 2, :]))
        ms = jnp.mean(o * o, axis=0, keepdims=True)
        osc_ref[64 * h:64 * h + 64, :] = o * lax.rsqrt(ms + EPS)
    o_ref[0] = osc_ref[...].T * g_ref[...] * out_scale


def _diff_attn(lam, q, k, vT, g_tiled, l, out_scale, tq):
    G, T, _ = q.shape
    return pl.pallas_call(
        functools.partial(_diff_attn_kernel, tq=tq, out_scale=out_scale),
        grid=(G, T // tq),
        in_specs=[pl.BlockSpec(memory_space=pltpu.SMEM),
                  pl.BlockSpec((1, tq, 256), lambda b, i: (b, i, 0)),
                  pl.BlockSpec((1, T, 256), lambda b, i: (b, 0, 0)),
                  pl.BlockSpec((1, 256, T), lambda b, i: (b, 0, 0)),
                  pl.BlockSpec((None, 1, 256), lambda b, i: (l, 0, 0))],
        out_specs=pl.BlockSpec((1, tq, 256), lambda b, i: (b, i, 0)),
        out_shape=jax.ShapeDtypeStruct((G, T, 256), F32),
        scratch_shapes=[pltpu.VMEM((8, 256, tq), BF16), pltpu.VMEM((8, tq), F32),
                        pltpu.VMEM((8, tq), F32), pltpu.VMEM((8, 64, tq), F32),
                        pltpu.VMEM((256, tq), F32)],
        compiler_params=_cparams("arbitrary", "arbitrary"),
        name="diff_attn",
    )(lam, q, k, vT, g_tiled)


def _topk_threshold(count_ge, count_tie_below, topk, n_col_bits, shape):
    kf = float(topk)
    zero = jnp.zeros(shape, I32)
    tau = jnp.where(count_ge(zero) >= kf, zero, zero + INT_MIN)

    def bit_body(b, tau):
        cand = tau + lax.shift_left(jnp.int32(1), 30 - b)
        return jnp.where(count_ge(cand) >= kf, cand, tau)

    tau = lax.fori_loop(0, 31, bit_body, tau)
    cnt_ge = count_ge(tau)
    cnt_gt = count_ge(tau + 1)
    need = kf - cnt_gt
    has_tie = jnp.where(tau > KEY_NEG_INF, jnp.where(cnt_ge - cnt_gt > need, 1.0, 0.0), 0.0)

    def tie_search():
        def tie_body(b, g):
            cand = g + lax.shift_left(jnp.int32(1), n_col_bits - 1 - b)
            return jnp.where(count_tie_below(tau, cand) < need, cand, g)
        return lax.fori_loop(0, n_col_bits, tie_body, zero)

    g_tie = lax.cond(jnp.max(has_tie) > 0.0, tie_search, lambda: zero)
    g = jnp.where(tau == KEY_NEG_INF, -1, jnp.where(has_tie > 0.0, g_tie, BIG_COL))
    return tau, g


def _dsa_kernel(q_ref, k_ref, vT_ref, iq_ref, ikt_ref, misc_ref, o_ref,
                qmT_ref, iqmT_ref, key_ref, m_ref, l_ref, acc_ref, osc_ref, *, tq, topk, n_col_bits):
    i = pl.program_id(1)
    nblk = i + 1
    feat = lax.broadcasted_iota(I32, (256, tq), 0)
    iqT = iq_ref[0].T
    for h in range(IDX_HEADS):
        iqmT_ref[h] = jnp.where((feat // IDX_DIM) == h, iqT, 0.0).astype(BF16)
    qT = q_ref[0].T
    for h in range(DSA_HEADS):
        qmT_ref[h] = jnp.where((feat // DSA_HD) == h, qT, 0.0).astype(BF16)
    wT = misc_ref[0].T[0:IDX_HEADS, :] * ((IDX_HEADS ** -0.5) * (IDX_DIM ** -0.5))
    krow = lax.broadcasted_iota(I32, (tq, tq), 0)
    qcol = lax.broadcasted_iota(I32, (tq, tq), 1)

    def score_block(j, masked):
        r0 = pl.multiple_of(j * tq, tq)
        ikj = ikt_ref[0, pl.ds(r0, tq), :].astype(BF16)
        tot = jnp.zeros((tq, tq), F32)
        for h in range(IDX_HEADS):
            tot = tot + wT[h:h + 1, :] * jnp.maximum(_dot(ikj, iqmT_ref[h]), 0.0)
        tot = jnp.where(tot == 0.0, 0.0, tot)
        if masked:
            tot = jnp.where(krow <= qcol, tot, -jnp.inf)
        key_ref[j] = _sort_key(tot)

    def score_body(j, carry):
        score_block(j, False)
        return carry

    lax.fori_loop(0, i, score_body, 0)
    score_block(i, True)

    def count_rows(hit_fn):
        def body(j, acc):
            return acc + jnp.sum(hit_fn(j).reshape(tq // 8, 8, tq), axis=0)
        acc = lax.fori_loop(0, nblk, body, jnp.zeros((8, tq), F32))
        return jnp.sum(acc, axis=0, keepdims=True)

    def count_ge(c):
        return count_rows(lambda j: jnp.where(key_ref[j] >= c, 1.0, 0.0))

    def count_tie_below(tau, c):
        return count_rows(lambda j: jnp.where(key_ref[j] == tau, jnp.where(krow + j * tq < c, 1.0, 0.0), 0.0))

    tau, g = _topk_threshold(count_ge, count_tie_below, topk, n_col_bits, (1, tq))

    m_ref[...] = jnp.full(m_ref.shape, NEG_BIG, F32)
    l_ref[...] = jnp.zeros(l_ref.shape, F32)
    acc_ref[...] = jnp.zeros(acc_ref.shape, F32)
    c = (DSA_HD ** -0.5) * LOG2E

    def att_body(j, carry):
        r0 = pl.multiple_of(j * tq, tq)
        sel = key_ref[j] >= tau + jnp.where(krow + j * tq > g, 1, 0)
        kj = k_ref[0, pl.ds(r0, tq), :].astype(BF16)
        vjT = vT_ref[0, :, pl.ds(r0, tq)].astype(BF16)
        hs = list(range(DSA_HEADS))
        ss = [_dot(kj, qmT_ref[h]) for h in hs]
        _softmax_steps(ss, sel, m_ref, l_ref, acc_ref, hs, [vjT[64 * h:64 * h + 64, :] for h in hs], c)
        return carry

    lax.fori_loop(0, nblk, att_body, 0)
    for h in range(DSA_HEADS):
        osc_ref[64 * h:64 * h + 64, :] = acc_ref[h] / l_ref[h:h + 1, :]
    o_ref[0] = osc_ref[...].T


def _dsa_attn(q, k, vT, iq, ikt, misc, topk, tq):
    G, T, _ = q.shape
    nb = T // tq
    return pl.pallas_call(
        functools.partial(_dsa_kernel, tq=tq, topk=topk, n_col_bits=max(1, (T - 1).bit_length())),
        grid=(G, nb),
        in_specs=[pl.BlockSpec((1, tq, 256), lambda b, i: (b, i, 0)),
                  pl.BlockSpec((1, T, 256), lambda b, i: (b, 0, 0)),
                  pl.BlockSpec((1, 256, T), lambda b, i: (b, 0, 0)),
                  pl.BlockSpec((1, tq, 256), lambda b, i: (b, i, 0)),
                  pl.BlockSpec((1, T, 256), lambda b, i: (b, 0, 0)),
                  pl.BlockSpec((1, tq, 128), lambda b, i: (b, i, 0))],
        out_specs=pl.BlockSpec((1, tq, 256), lambda b, i: (b, i, 0)),
        out_shape=jax.ShapeDtypeStruct((G, T, 256), F32),
        scratch_shapes=[pltpu.VMEM((DSA_HEADS, 256, tq), BF16), pltpu.VMEM((IDX_HEADS, 256, tq), BF16),
                        pltpu.VMEM((nb, tq, tq), I32), pltpu.VMEM((8, tq), F32),
                        pltpu.VMEM((8, tq), F32), pltpu.VMEM((DSA_HEADS, 64, tq), F32),
                        pltpu.VMEM((256, tq), F32)],
        compiler_params=_cparams("arbitrary", "arbitrary"),
        name="dsa_attn",
    )(q, k, vT, iq, ikt, misc)


def _gla_kernel(qk_ref, v_ref, g_ref, misc_ref, wa_ref, ba_ref, gn_ref, ones_ref, bd_ref, s0_ref,
                gsel_ref, negq_ref, negk_ref, pair_ref, o_ref, sfin_ref, s_ref, *, tm, chunk, n_valid):
    i = pl.program_id(1)

    @pl.when(i == 0)
    def _():
        s_ref[...] = jnp.zeros(s_ref.shape, F32)
        for h in range(GLA_HEADS):
            s_ref[32 * h:32 * h + 32, 64 * h:64 * h + 64] = s0_ref[0, h]

    crow = lax.broadcasted_iota(I32, (chunk, 128), 0)
    lane_h = lax.broadcasted_iota(I32, (chunk, 128), 1) // GLA_DK
    vlane_h = lax.broadcasted_iota(I32, (chunk, 256), 1) // GLA_DV
    n_levels = chunk.bit_length() - 1

    def chunk_body(c, carry):
        r0 = pl.multiple_of(c * chunk, chunk)
        rows = pl.ds(r0, chunk)
        a_lin = _dot(misc_ref[0, rows, :].astype(BF16), wa_ref[...]) + ba_ref[...]
        la = (jnp.minimum(a_lin, 0.0) - jnp.log1p(jnp.exp(-jnp.abs(a_lin)))) * (1.0 / GLA_TAU)
        qk = qk_ref[0, rows, :]
        q = qk[:, :128] * (GLA_DK ** -0.5)
        k = qk[:, 128:]
        if n_valid is not None:
            live = (crow + r0) < n_valid
            la = jnp.where(live, la, 0.0)
            k = jnp.where(live, k, 0.0)
        v = v_ref[0, rows, :]
        vb = v.astype(BF16)
        b = _cumsum_rows(la)
        bT = b.T
        kT = k.T
        S = s_ref[...]
        o = _dot((q * jnp.exp(b)).astype(BF16), S.astype(BF16))
        b_hi = b.astype(BF16)
        r1 = b - b_hi.astype(F32)
        b_mid = r1.astype(BF16)
        b_lo = (r1 - b_mid.astype(F32)).astype(BF16)
        gsel = gsel_ref[...]
        ref_rows = _dot(gsel, b_hi) + _dot(gsel, b_mid) + _dot(gsel, b_lo)
        att = jnp.zeros((GLA_HEADS * chunk, chunk), F32)
        for j in range(n_levels):
            rj = ref_rows[j * chunk:(j + 1) * chunk]
            qt = q * jnp.exp((b - rj) + negq_ref[j])
            kh = k * jnp.exp((rj - b) + negk_ref[j])
            qs = jnp.concatenate([jnp.where(lane_h == h, qt, 0.0) for h in range(GLA_HEADS)], axis=0)
            att = att + _dot_nt(qs.astype(BF16), kh.astype(BF16)) * pair_ref[j]
        x = _dot(att.astype(BF16), vb)
        for h in range(GLA_HEADS):
            o = o + jnp.where(vlane_h == h, x[h * chunk:(h + 1) * chunk], 0.0)
        o = o + _dot((q * k).astype(BF16), bd_ref[...].astype(BF16)) * v
        bendT = bT[:, chunk - 1:chunk]
        kdT = kT * jnp.exp(bendT - bT)
        s_ref[...] = jnp.exp(bendT) * S + _dot(kdT.astype(BF16), vb) * bd_ref[...]
        ms = _group_mean64(o * o, ones_ref[...])
        o_ref[0, rows, :] = o * lax.rsqrt(ms + EPS) * gn_ref[...] * _silu(g_ref[0, rows, :])
        return carry

    lax.fori_loop(0, tm // chunk, chunk_body, 0)

    @pl.when(i == pl.num_programs(1) - 1)
    def _():
        for h in range(GLA_HEADS):
            sfin_ref[0, h] = s_ref[32 * h:32 * h + 32, 64 * h:64 * h + 64]


def _gla_tables(chunk):
    t = jnp.arange(chunk)
    gsel, negq, negk, pair = [], [], [], []
    m = chunk // 2
    while m >= 1:
        blk, second = t // (2 * m), (t % (2 * m)) >= m
        gsel.append((t[None, :] == (blk * 2 * m + m - 1)[:, None]).astype(BF16))
        negq.append(jnp.broadcast_to(jnp.where(second, 0.0, -jnp.inf)[:, None], (chunk, 128)))
        negk.append(jnp.broadcast_to(jnp.where(second, -jnp.inf, 0.0)[:, None], (chunk, 128)))
        pair.append(jnp.tile((blk[:, None] == blk[None, :]).astype(F32), (GLA_HEADS, 1)))
        m //= 2
    return (jnp.concatenate(gsel, axis=0), jnp.stack(negq).astype(F32), jnp.stack(negk).astype(F32),
            jnp.stack(pair))


def _gla(gqk, gv, gg, misc, gw, l, s0, ones_bd, bd_mask, tm, chunk, n_valid):
    G, T, _ = gqk.shape
    wspec = lambda shape: pl.BlockSpec((None,) + shape, lambda b, i: (l,) + (0,) * len(shape))
    tables = _gla_tables(chunk)
    const = lambda a: pl.BlockSpec(a.shape, lambda b, i: (0,) * a.ndim)
    return pl.pallas_call(
        functools.partial(_gla_kernel, tm=tm, chunk=chunk, n_valid=n_valid),
        grid=(G, T // tm),
        in_specs=[pl.BlockSpec((1, tm, 256), lambda b, i: (b, i, 0)),
                  pl.BlockSpec((1, tm, 256), lambda b, i: (b, i, 0)),
                  pl.BlockSpec((1, tm, 256), lambda b, i: (b, i, 0)),
                  pl.BlockSpec((1, tm, 128), lambda b, i: (b, i, 0)),
                  wspec((128, 128)), wspec((1, 128)), wspec((1, 256)),
                  pl.BlockSpec((256, 256), lambda b, i: (0, 0)),
                  pl.BlockSpec((128, 256), lambda b, i: (0, 0)),
                  pl.BlockSpec((1, GLA_HEADS, GLA_DK, GLA_DV), lambda b, i: (b, 0, 0, 0))]
                 + [const(a) for a in tables],
        out_specs=[pl.BlockSpec((1, tm, 256), lambda b, i: (b, i, 0)),
                   pl.BlockSpec((1, GLA_HEADS, GLA_DK, GLA_DV), lambda b, i: (b, 0, 0, 0))],
        out_shape=[jax.ShapeDtypeStruct((G, T, 256), F32),
                   jax.ShapeDtypeStruct((G, GLA_HEADS, GLA_DK, GLA_DV), F32)],
        scratch_shapes=[pltpu.VMEM((128, 256), F32)],
        compiler_params=_cparams("arbitrary", "arbitrary"),
        name="gla",
    )(gqk, gv, gg, misc, gw["wa"], gw["ba"], gw["gn"], ones_bd, bd_mask, s0, *tables)


def _ffn_kernel(*refs, tm, seq, final, nb):
    (x_ref, oa_ref, ob_ref, oc_ref, od_ref, g1_ref, sh2_ref, sc2_ref, g2_ref, n2_ref,
     wout_ref, wg_ref, wu_ref, wd_ref, cw_ref, cb_ref, buf_ref, fng_ref, y_ref, cs_ref) = refs[:20]
    x = x_ref[0]
    mix = _dot(oa_ref[0].astype(BF16), wout_ref[0:256, :])
    mix = mix + _dot(ob_ref[0].astype(BF16), wout_ref[256:512, :])
    mix = mix + _dot(oc_ref[0].astype(BF16), wout_ref[512:768, :])
    mix = mix + _dot(od_ref[0].astype(BF16), wout_ref[768:1024, :])
    x1 = x + g1_ref[0] * mix
    h2 = _rms(x1, n2_ref[...]) * (1.0 + sc2_ref[0]) + sh2_ref[0]
    hb = h2.astype(BF16)
    gate = _dot(hb, wg_ref[...])
    w = cw_ref[...]
    if seq:
        cbuf_ref = refs[20]

        @pl.when(pl.program_id(1) == 0)
        def _():
            cbuf_ref[...] = buf_ref[0]

        prev8 = cbuf_ref[...]
        z = cb_ref[...] + gate * w[2:3] + _shift_rows(gate, 1, prev8) * w[1:2] + _shift_rows(gate, 2, prev8) * w[0:1]
        cbuf_ref[...] = gate[tm - 8:]
        cs_ref[0] = gate[tm - 8:]
    else:
        xx = jnp.concatenate([buf_ref[0], gate], axis=0)
        z = cb_ref[...] + xx[0:tm] * w[0:1] + xx[nb:nb + tm] * w[1:2] + xx[2 * nb:2 * nb + tm] * w[2:3]
        cs_ref[0] = xx[tm:tm + 2 * nb]
    up = _dot(hb, wu_ref[...])
    y = _dot((_silu(z) * up).astype(BF16), wd_ref[...])
    x2 = x1 + g2_ref[0] * y
    if final:
        x2 = _rms(x2, fng_ref[...])
    y_ref[0] = x2


def _ffn(x, o4, mods, fw, l, buf, final_g, tm, seq, final, nb=0):
    G, T, D = x.shape
    per_row = mods[0].shape[1] != 1
    mod_spec = (pl.BlockSpec((1, tm, D), lambda b, i: (b, i, 0)) if per_row
                else pl.BlockSpec((1, 1, D), lambda b, i: (b, 0, 0)))
    o_spec = pl.BlockSpec((1, tm, 256), lambda b, i: (b, i, 0))
    once = pl.Buffered(1)
    wspec = lambda shape: pl.BlockSpec((None,) + shape, lambda b, i: (l,) + (0,) * len(shape), pipeline_mode=once)
    nbuf = buf.shape[1]
    return pl.pallas_call(
        functools.partial(_ffn_kernel, tm=tm, seq=seq, final=final, nb=nb),
        grid=(G, T // tm),
        in_specs=[pl.BlockSpec((1, tm, D), lambda b, i: (b, i, 0)), o_spec, o_spec, o_spec, o_spec,
                  mod_spec, mod_spec, mod_spec, mod_spec,
                  wspec((1, D)), wspec((D, D)), wspec((D, D_FF)), wspec((D, D_FF)), wspec((D_FF, D)),
                  wspec((FFN_CONV, D_FF)), wspec((1, D_FF)),
                  pl.BlockSpec((1, nbuf, D_FF), lambda b, i: (b, 0, 0)),
                  pl.BlockSpec((1, D), lambda b, i: (0, 0))],
        out_specs=[pl.BlockSpec((1, tm, D), lambda b, i: (b, i, 0)),
                   pl.BlockSpec((1, nbuf, D_FF), lambda b, i: (b, 0, 0))],
        out_shape=[jax.ShapeDtypeStruct((G, T, D), F32), jax.ShapeDtypeStruct((G, nbuf, D_FF), F32)],
        scratch_shapes=[pltpu.VMEM((8, D_FF), F32)] if seq else [],
        compiler_params=_cparams("arbitrary", "arbitrary"),
        name="out_ffn",
    )(x, *o4, *mods, fw["n2"], fw["wout"], fw["wg"], fw["wu"], fw["wd"], fw["cw"], fw["cb"], buf, final_g)


def _diff_dec_kernel(pt_ref, lam_ref, q_ref, *refs, n_new, out_scale, nps):
    kT_refs, vT_refs = refs[:nps], refs[nps:2 * nps]
    kn_ref, vnT_ref, g_ref, ones_ref, o_ref, qs_ref, m_ref, l_ref, acc_ref, osc_ref = refs[2 * nps:]
    p = pl.program_id(1)
    lane = lax.broadcasted_iota(I32, (8, 256), 1)

    @pl.when(p == 0)
    def _():
        q = q_ref[0]
        for hm in range(8):
            qs_ref[8 * hm:8 * hm + 8, :] = jnp.where((lane // 32) == hm, q, 0.0).astype(BF16)
        m_ref[...] = jnp.full(m_ref.shape, NEG_BIG, F32)
        l_ref[...] = jnp.zeros(l_ref.shape, F32)
        acc_ref[...] = jnp.zeros(acc_ref.shape, F32)

    def update(s, v_contract):
        m_prev = m_ref[...]
        m_new = jnp.maximum(m_prev, jnp.max(s, axis=1, keepdims=True))
        alpha = jnp.exp(m_prev - m_new)
        pr = jnp.exp(s - m_new)
        l_ref[...] = alpha * l_ref[...] + jnp.sum(pr, axis=1, keepdims=True)
        acc_ref[...] = alpha * acc_ref[...] + v_contract(pr.astype(BF16))
        m_ref[...] = m_new

    qs = qs_ref[...]
    s = jnp.concatenate([_dot(qs, r[0, 0].astype(BF16)) for r in kT_refs], axis=1) * (DIFF_QK ** -0.5)

    def pv(pr):
        out = _dot_nt(pr[:, 0:PAGE], vT_refs[0][0, 0].astype(BF16))
        for i in range(1, nps):
            out = out + _dot_nt(pr[:, PAGE * i:PAGE * (i + 1)], vT_refs[i][0, 0].astype(BF16))
        return out

    update(s, pv)

    @pl.when(p == pl.num_programs(1) - 1)
    def _():
        s = _dot_nt(qs, kn_ref[0].astype(BF16)) * (DIFF_QK ** -0.5)
        t = lax.broadcasted_iota(I32, s.shape, 0) % 8
        c = lax.broadcasted_iota(I32, s.shape, 1)
        s = jnp.where((c <= t) & (c < n_new), s, NEG_BIG)
        vnT = vnT_ref[0].astype(BF16)
        update(s, lambda pr: _dot_nt(pr, vnT))
        o_ref[0] = _diff_finalize(acc_ref, l_ref, lam_ref[0, 0], osc_ref, g_ref, ones_ref, out_scale, 8, True)


def _page_specs(l, nps, rows):
    return [pl.BlockSpec((1, 1, rows, PAGE), lambda b, p, pt, i=i: (l, pt[b, p * nps + i], 0, 0))
            for i in range(nps)]


def _diff_dec(page_table, lam, q8, ckT, cvT, kn, vnT, g_tiled, ones_bd, l, out_scale, n_new):
    B, NP = page_table.shape
    nps = math.gcd(NP, PAGES_PER_STEP)
    gs = pltpu.PrefetchScalarGridSpec(
        num_scalar_prefetch=1, grid=(B, NP // nps),
        in_specs=[pl.BlockSpec(memory_space=pltpu.SMEM),
                  pl.BlockSpec((1, 8, 256), lambda b, p, pt: (b, 0, 0))]
                 + _page_specs(l, nps, 256) + _page_specs(l, nps, 256)
                 + [pl.BlockSpec((1, 128, 256), lambda b, p, pt: (b, 0, 0)),
                    pl.BlockSpec((1, 256, 128), lambda b, p, pt: (b, 0, 0)),
                    pl.BlockSpec((None, 1, 256), lambda b, p, pt: (l, 0, 0)),
                    pl.BlockSpec((256, 256), lambda b, p, pt: (0, 0))],
        out_specs=pl.BlockSpec((1, 8, 256), lambda b, p, pt: (b, 0, 0)),
        scratch_shapes=[pltpu.VMEM((64, 256), BF16), pltpu.VMEM((64, 1), F32), pltpu.VMEM((64, 1), F32),
                        pltpu.VMEM((64, 256), F32), pltpu.VMEM((8, 256), F32)])
    return pl.pallas_call(
        functools.partial(_diff_dec_kernel, n_new=n_new, out_scale=out_scale, nps=nps),
        grid_spec=gs, out_shape=jax.ShapeDtypeStruct((B, 8, 256), F32),
        compiler_params=_cparams("arbitrary", "arbitrary"),
        name="diff_decode",
    )(page_table, lam, q8, *([ckT] * nps), *([cvT] * nps), kn, vnT, g_tiled, ones_bd)


def _idx_dec_kernel(pt_ref, iq_ref, w_ref, *refs, n_new, nps):
    ikT_refs = refs[:nps]
    iknT_ref, keys_ref, keysn_ref = refs[nps:]
    p = pl.program_id(1)
    iq = iq_ref[0].astype(BF16)
    w = w_ref[0]

    def scores(ikT):
        sc = _dot(iq, ikT.astype(BF16)) * (IDX_DIM ** -0.5)
        tot = jnp.sum((w * jnp.maximum(sc, 0.0)).reshape(8, 8, PAGE), axis=1)
        return jnp.where(tot == 0.0, 0.0, tot)

    for i in range(nps):
        keys_ref[0, i] = _sort_key(scores(ikT_refs[i][0, 0]))

    @pl.when(p == pl.num_programs(1) - 1)
    def _():
        tot = scores(iknT_ref[0])
        t = lax.broadcasted_iota(I32, tot.shape, 0)
        c = lax.broadcasted_iota(I32, tot.shape, 1)
        keysn_ref[0] = _sort_key(jnp.where((c <= t) & (c < n_new), tot, -jnp.inf))


def _idx_dec(page_table, iq64, w64, cikT, iknT, l, n_new):
    B, NP = page_table.shape
    nps = math.gcd(NP, PAGES_PER_STEP)
    gs = pltpu.PrefetchScalarGridSpec(
        num_scalar_prefetch=1, grid=(B, NP // nps),
        in_specs=[pl.BlockSpec((1, 64, IDX_DIM), lambda b, p, pt: (b, 0, 0)),
                  pl.BlockSpec((1, 64, 1), lambda b, p, pt: (b, 0, 0))]
                 + _page_specs(l, nps, IDX_DIM)
                 + [pl.BlockSpec((1, IDX_DIM, 128), lambda b, p, pt: (b, 0, 0))],
        out_specs=[pl.BlockSpec((1, nps, 8, PAGE), lambda b, p, pt: (b, p, 0, 0)),
                   pl.BlockSpec((1, 8, 128), lambda b, p, pt: (b, 0, 0))])
    return pl.pallas_call(
        functools.partial(_idx_dec_kernel, n_new=n_new, nps=nps),
        grid_spec=gs,
        out_shape=[jax.ShapeDtypeStruct((B, NP, 8, PAGE), I32), jax.ShapeDtypeStruct((B, 8, 128), I32)],
        compiler_params=_cparams("arbitrary", "arbitrary"),
        name="idx_decode",
    )(page_table, iq64, w64, *([cikT] * nps), iknT)


def _dsa_dec_kernel(pt_ref, q_ref, keys_ref, keysn_ref, *refs, topk, n_pages, n_col_bits, nps):
    kT_refs, vT_refs = refs[:nps], refs[nps:2 * nps]
    kn_ref, vnT_ref, o_ref, qs_ref, tg_ref, m_ref, l_ref, acc_ref = refs[2 * nps:]
    p = pl.program_id(1)
    lane = lax.broadcasted_iota(I32, (8, 256), 1)
    colp = lax.broadcasted_iota(I32, (8, PAGE), 1)

    @pl.when(p == 0)
    def _():
        q = q_ref[0]
        for h in range(DSA_HEADS):
            qs_ref[8 * h:8 * h + 8, :] = jnp.where((lane // 64) == h, q, 0.0).astype(BF16)
        m_ref[...] = jnp.full(m_ref.shape, NEG_BIG, F32)
        l_ref[...] = jnp.zeros(l_ref.shape, F32)
        acc_ref[...] = jnp.zeros(acc_ref.shape, F32)

        def count_ge(c):
            def body(j, acc):
                return acc + jnp.where(keys_ref[0, j] >= c, 1.0, 0.0)
            acc = lax.fori_loop(0, n_pages, body, jnp.where(keysn_ref[0] >= c, 1.0, 0.0))
            return jnp.sum(acc, axis=1, keepdims=True)

        def count_tie_below(tau, c):
            def body(j, acc):
                return acc + jnp.where(keys_ref[0, j] == tau, jnp.where(colp + j * PAGE < c, 1.0, 0.0), 0.0)
            first = jnp.where(keysn_ref[0] == tau, jnp.where(colp + n_pages * PAGE < c, 1.0, 0.0), 0.0)
            acc = lax.fori_loop(0, n_pages, body, first)
            return jnp.sum(acc, axis=1, keepdims=True)

        tau, g = _topk_threshold(count_ge, count_tie_below, topk, n_col_bits, (8, 1))
        tg_ref[:, 0:1] = tau
        tg_ref[:, 1:2] = g

    tau = tg_ref[:, 0:1]
    g = tg_ref[:, 1:2]

    def update(s, key8, thr8, v_contract):
        sel = jnp.concatenate([key8] * DSA_HEADS, axis=0) >= jnp.concatenate([thr8] * DSA_HEADS, axis=0)
        m_prev = m_ref[...]
        m_new = jnp.maximum(m_prev, jnp.max(jnp.where(sel, s, NEG_BIG), axis=1, keepdims=True))
        alpha = jnp.exp(m_prev - m_new)
        pr = jnp.where(sel, jnp.exp(s - m_new), 0.0)
        l_ref[...] = alpha * l_ref[...] + jnp.sum(pr, axis=1, keepdims=True)
        acc_ref[...] = alpha * acc_ref[...] + v_contract(pr.astype(BF16))
        m_ref[...] = m_new

    qs = qs_ref[...]
    s = jnp.concatenate([_dot(qs, r[0, 0].astype(BF16)) for r in kT_refs], axis=1) * (DSA_HD ** -0.5)
    key8 = jnp.concatenate([keys_ref[0, p * nps + i] for i in range(nps)], axis=1)
    col = lax.broadcasted_iota(I32, key8.shape, 1) + p * (nps * PAGE)
    thr8 = tau + jnp.where(col > g, 1, 0)

    def pv(pr):
        out = _dot_nt(pr[:, 0:PAGE], vT_refs[0][0, 0].astype(BF16))
        for i in range(1, nps):
            out = out + _dot_nt(pr[:, PAGE * i:PAGE * (i + 1)], vT_refs[i][0, 0].astype(BF16))
        return out

    update(s, key8, thr8, pv)

    @pl.when(p == pl.num_programs(1) - 1)
    def _():
        thrn = tau + jnp.where(colp + n_pages * PAGE > g, 1, 0)
        s = _dot_nt(qs, kn_ref[0].astype(BF16)) * (DSA_HD ** -0.5)
        vnT = vnT_ref[0].astype(BF16)
        update(s, keysn_ref[0], thrn, lambda pr: _dot_nt(pr, vnT))
        for h in range(DSA_HEADS):
            o_ref[0, :, 64 * h:64 * h + 64] = (acc_ref[8 * h:8 * h + 8, 64 * h:64 * h + 64]
                                               / l_ref[8 * h:8 * h + 8, :])


def _dsa_dec(page_table, q8, keys, keysn, ckT, cvT, kn, vnT, l, topk):
    B, NP = page_table.shape
    nps = math.gcd(NP, PAGES_PER_STEP)
    n_keys = NP * PAGE + 128
    gs = pltpu.PrefetchScalarGridSpec(
        num_scalar_prefetch=1, grid=(B, NP // nps),
        in_specs=[pl.BlockSpec((1, 8, 256), lambda b, p, pt: (b, 0, 0)),
                  pl.BlockSpec((1, NP, 8, PAGE), lambda b, p, pt: (b, 0, 0, 0)),
                  pl.BlockSpec((1, 8, 128), lambda b, p, pt: (b, 0, 0))]
                 + _page_specs(l, nps, 256) + _page_specs(l, nps, 256)
                 + [pl.BlockSpec((1, 128, 256), lambda b, p, pt: (b, 0, 0)),
                    pl.BlockSpec((1, 256, 128), lambda b, p, pt: (b, 0, 0))],
        out_specs=pl.BlockSpec((1, 8, 256), lambda b, p, pt: (b, 0, 0)),
        scratch_shapes=[pltpu.VMEM((32, 256), BF16), pltpu.VMEM((8, 128), I32), pltpu.VMEM((32, 1), F32),
                        pltpu.VMEM((32, 1), F32), pltpu.VMEM((32, 256), F32)])
    return pl.pallas_call(
        functools.partial(_dsa_dec_kernel, topk=topk, n_pages=NP, n_col_bits=(n_keys - 1).bit_length(), nps=nps),
        grid_spec=gs, out_shape=jax.ShapeDtypeStruct((B, 8, 256), F32),
        compiler_params=_cparams("arbitrary", "arbitrary"),
        name="dsa_decode",
    )(page_table, q8, keys, keysn, *([ckT] * nps), *([cvT] * nps), kn, vnT)


def _rope_tables(pos, half):
    inv = ROPE_THETA ** (-jnp.arange(half, dtype=F32) / half)
    ang = pos.astype(F32)[:, None] * inv[None, :]
    cos, sin = jnp.cos(ang), jnp.sin(ang)
    reps = 128 // (2 * half)
    return (jnp.tile(jnp.concatenate([cos, cos], axis=1), (1, reps)),
            jnp.tile(jnp.concatenate([-sin, sin], axis=1), (1, reps)))


def _pack_w_in(w_in):
    names = (("lru_x", 256), ("lru_gate", 256), ("diff_q", 256), ("diff_k", 256), ("diff_v", 256),
             ("dsa_q", 256), ("dsa_k", 256), ("dsa_v", 256), ("idx_q", 256), ("idx_k", 32), ("idx_w", 8),
             ("gla_q", 128), ("gla_k", 128), ("gla_v", 256), ("gla_g", 256), ("gla_a", 16))
    seg, off = {}, 0
    for n, w in names:
        seg[n] = w_in[..., off:off + w]
        off += w
    pad = jnp.zeros(w_in.shape[:-1] + (128 - 24,), w_in.dtype)
    cols = [seg["lru_x"], seg["lru_gate"], seg["diff_q"], seg["diff_k"],
            seg["dsa_q"], seg["dsa_k"], seg["idx_q"]] + [seg["idx_k"]] * IDX_HEADS + [
            seg["gla_q"], seg["gla_k"], seg["gla_v"], seg["gla_g"], seg["idx_w"], seg["gla_a"], pad]
    wvT = jnp.concatenate([seg["diff_v"], seg["dsa_v"]], axis=-1).transpose(0, 2, 1)
    return jnp.concatenate(cols, axis=-1).astype(BF16), wvT.astype(BF16)


def _block_diag(w):
    depth, nb, bw, _ = w.shape
    eye = jnp.eye(nb, dtype=w.dtype)
    return jnp.einsum("lnij,nm->lnimj", w, eye).reshape(depth, nb * bw, nb * bw)


def _group_ones(n, group):
    r = jnp.arange(n) // group
    return (r[:, None] == r[None, :]).astype(BF16)


def _prompt_layer(x, mod, l, W, tabs, final, tm=256, tq=256):
    G, T, _ = x.shape
    sh1, sc1, g1, sh2, sc2, g2 = [m[:, None, :] for m in jnp.split(mod, 6, axis=-1)]
    (lru, qb, kb, vbT, qc, kc, vcT, iq, ikt, gqk, gv, gg, misc) = _proj(
        x, sh1, sc1, W["norm1"], W["w_in"], W["wvT"], l, tabs, tm)
    zeros = lambda *s: jnp.zeros(s, F32)
    o_a, h_last, cs8 = _lru_seq(lru, W["lru"], l, zeros(G, 1, 256), zeros(G, 8, 256), tm)
    o_b = _diff_attn(W["lam"][l], qb, kb, vbT, W["subln"], l, W["out_scale"][l], tq)
    o_c = _dsa_attn(qc, kc, vcT, iq, ikt, misc, min(TOPK_MAX, T // 4), tq)
    from_T = lambda a: a.reshape(G, 4, 64, T).transpose(0, 3, 1, 2)
    o_d, s_last = _gla(gqk, gv, gg, misc, W["gla"], l, zeros(G, GLA_HEADS, GLA_DK, GLA_DV),
                       W["ones64"], W["bd_mask"], tm, GLA_CHUNK, None)
    y, fcs8 = _ffn(x, (o_a, o_b, o_c, o_d), (g1, sh2, sc2, g2), W["ffn"], l, zeros(G, 8, D_FF),
                   W["final_g"], tm, True, final)
    new = dict(
        diff_k=kb.reshape(G, T, DIFF_HEADS, 2, DIFF_QK), diff_v=from_T(vbT),
        dsa_k=kc.reshape(G, T, DSA_HEADS, DSA_HD), dsa_v=from_T(vcT),
        idx_k=ikt[:, :, :IDX_DIM], lru_h=h_last[:, 0], lru_conv=cs8[:, 8 - (LRU_CONV - 1):],
        gla=s_last, ffn_conv=fcs8[:, 8 - (FFN_CONV - 1):])
    return y, new


def _pad_rows(a, n):
    return jnp.pad(a, ((0, 0), (0, n - a.shape[1])) + ((0, 0),) * (a.ndim - 2))


def _sample_layer(xr, mod, l, W, tabs, caches, states, page_table, final, B, Tn):
    n = Tn * B
    tm_major = lambda a: jnp.tile(a, (Tn, 1))[None]
    sh1, sc1, g1, sh2, sc2, g2 = [tm_major(m) for m in jnp.split(mod, 6, axis=-1)]
    outs = _proj(xr, sh1, sc1, W["norm1"], W["w_in"], W["wvT"], l, tabs, n)
    (lru, qb, kb, vbT, qc, kc, vcT, iq, ikt, gqk, gv, gg, misc) = [o[0] for o in outs]
    to_b = lambda a: a.reshape(Tn, B, -1).transpose(1, 0, 2)
    to_t = lambda a: a.transpose(1, 0, 2).reshape(n, -1)
    to_bT = lambda a: a.reshape(256, Tn, B).transpose(2, 0, 1)
    pad_lanes = lambda a: jnp.pad(a, ((0, 0), (0, 0), (0, 128 - a.shape[2])))

    conv_prev = states["lru_conv"][l].transpose(1, 0, 2).reshape((LRU_CONV - 1) * B, 256)
    xx = jnp.concatenate([conv_prev, lru[:, :256]], axis=0)
    o_a, h_last = _lru_rows(xx, lru[:, 256:], W["lru"], l, states["lru_h"][l], B, Tn)
    new_conv = xx[Tn * B:].reshape(LRU_CONV - 1, B, 256).transpose(1, 0, 2)

    kb_b, vb_bT = to_b(kb), to_bT(vbT)
    o_b = _diff_dec(page_table, W["lam"][l], _pad_rows(to_b(qb), 8), caches["diff_kT"], caches["diff_vT"],
                    _pad_rows(kb_b, 128), pad_lanes(vb_bT), W["subln"], W["ones64"], l,
                    W["out_scale"][l], Tn)[:, :Tn]

    past_len = page_table.shape[1] * PAGE
    topk = min(TOPK_MAX, (past_len + Tn) // 4)
    kc_b, vc_bT = to_b(kc), to_bT(vcT)
    ik_b = to_b(ikt)[:, :, :IDX_DIM]
    iq64 = _pad_rows(to_b(iq), 8).reshape(B, 8 * IDX_HEADS, IDX_DIM)
    w64 = (_pad_rows(to_b(misc)[:, :, :IDX_HEADS], 8) * (IDX_HEADS ** -0.5)).reshape(B, 8 * IDX_HEADS, 1)
    iknT = _pad_rows(ik_b, 128).transpose(0, 2, 1)
    keys, keysn = _idx_dec(page_table, iq64, w64, caches["idx_kT"], iknT, l, Tn)
    o_c = _dsa_dec(page_table, _pad_rows(to_b(qc), 8), keys, keysn, caches["dsa_kT"], caches["dsa_vT"],
                   _pad_rows(kc_b, 128), pad_lanes(vc_bT), l, topk)[:, :Tn]

    padc = lambda a: _pad_rows(to_b(a), GLA_CHUNK)
    o_d, s_last = _gla(padc(gqk), padc(gv), padc(gg), padc(misc), W["gla"], l, states["gla"][l],
                       W["ones64"], W["bd_mask"], GLA_CHUNK, GLA_CHUNK, Tn)
    o_d = o_d[:, :Tn]

    ffn_prev = states["ffn_conv"][l].transpose(1, 0, 2).reshape(1, (FFN_CONV - 1) * B, D_FF)
    o4 = (o_a[None], to_t(o_b)[None], to_t(o_c)[None], to_t(o_d)[None])
    y, fcs = _ffn(xr, o4, (g1, sh2, sc2, g2), W["ffn"], l, ffn_prev, W["final_g"], n, False, final, nb=B)
    new = dict(
        diff_k=kb_b.reshape(B, Tn, DIFF_HEADS, 2, DIFF_QK),
        diff_v=vb_bT.transpose(0, 2, 1).reshape(B, Tn, DIFF_HEADS, 64),
        dsa_k=kc_b.reshape(B, Tn, DSA_HEADS, DSA_HD),
        dsa_v=vc_bT.transpose(0, 2, 1).reshape(B, Tn, DSA_HEADS, DSA_HD),
        idx_k=ik_b, lru_h=h_last, lru_conv=new_conv, gla=s_last,
        ffn_conv=fcs[0].reshape(FFN_CONV - 1, B, D_FF).transpose(1, 0, 2))
    return y, new


def kernel(x_prompt, x_sample, cache_diff_k, cache_diff_v, cache_dsa_k, cache_dsa_v, cache_idx_k, state_lru_h, state_lru_conv, state_gla, state_ffn_conv, page_table, c_prompt, c_sample, w_ada, b_ada, norm1_g, w_in, lru_conv_w, lru_conv_b, lru_wa, lru_ba, lru_wx, lru_bx, lru_lambda, diff_lq1, diff_lk1, diff_lq2, diff_lk2, diff_subln_g, gla_wa2, gla_ba, gla_norm_g, w_out, norm2_g, ffn_w_gate, ffn_w_up, ffn_conv_w, ffn_conv_b, ffn_w_down, final_norm_g):
    depth = w_in.shape[0]
    Bp, T, D = x_prompt.shape
    Bs, Tn, _ = x_sample.shape
    n_pool = cache_diff_k.shape[1]
    past_len = page_table.shape[1] * PAGE

    lam_init = jnp.asarray([0.8 - 0.6 * math.exp(-0.3 * l) for l in range(depth)], F32)
    lam = (jnp.exp(jnp.sum(diff_lq1 * diff_lk1, axis=-1)) - jnp.exp(jnp.sum(diff_lq2 * diff_lk2, axis=-1))
           + lam_init).reshape(depth, 1, 1)
    row3 = lambda a: a.reshape(depth, 1, a.shape[-1])
    wa2_pad = jnp.zeros((depth, 128, 128), F32).at[:, IDX_HEADS:IDX_HEADS + gla_wa2.shape[1], :].set(gla_wa2)
    w_packed, wvT = _pack_w_in(w_in)
    W = dict(
        norm1=row3(norm1_g), w_in=w_packed, wvT=wvT,
        lru=dict(conv_w=lru_conv_w, conv_b=row3(lru_conv_b), wa=_block_diag(lru_wa).astype(BF16),
                 wx=_block_diag(lru_wx).astype(BF16), ba=row3(lru_ba), bx=row3(lru_bx), lam=row3(lru_lambda)),
        lam=lam, out_scale=[1.0 - (0.8 - 0.6 * math.exp(-0.3 * l)) for l in range(depth)],
        subln=row3(jnp.tile(diff_subln_g, (1, DIFF_HEADS))), ones64=_group_ones(256, 64),
        bd_mask=(jnp.arange(128)[:, None] // GLA_DK == jnp.arange(256)[None, :] // GLA_DV).astype(F32),
        gla=dict(wa=wa2_pad.astype(BF16), ba=row3(gla_ba), gn=row3(jnp.tile(gla_norm_g, (1, GLA_HEADS)))),
        ffn=dict(n2=row3(norm2_g), wout=w_out.astype(BF16), wg=ffn_w_gate.astype(BF16),
                 wu=ffn_w_up.astype(BF16), wd=ffn_w_down.astype(BF16), cw=ffn_conv_w, cb=row3(ffn_conv_b)),
        final_g=final_norm_g.reshape(1, D))

    mods = _ada(jnp.concatenate([c_prompt, c_sample], axis=0), w_ada, b_ada)

    pos_p = jnp.arange(T, dtype=I32)
    pos_s = jnp.repeat(past_len + jnp.arange(Tn, dtype=I32), Bs)
    tabs_p = _rope_tables(pos_p, 16) + _rope_tables(pos_p, 32)
    tabs_s = _rope_tables(pos_s, 16) + _rope_tables(pos_s, 32)

    caches = dict(
        diff_kT=cache_diff_k.transpose(0, 1, 3, 4, 5, 2).reshape(depth, n_pool, 256, PAGE),
        diff_vT=cache_diff_v.transpose(0, 1, 3, 4, 2).reshape(depth, n_pool, 256, PAGE),
        dsa_kT=cache_dsa_k.transpose(0, 1, 3, 4, 2).reshape(depth, n_pool, 256, PAGE),
        dsa_vT=cache_dsa_v.transpose(0, 1, 3, 4, 2).reshape(depth, n_pool, 256, PAGE),
        idx_kT=cache_idx_k.transpose(0, 1, 3, 2))
    states = dict(lru_h=state_lru_h, lru_conv=state_lru_conv, gla=state_gla, ffn_conv=state_ffn_conv)

    xp = x_prompt
    xs = x_sample.transpose(1, 0, 2).reshape(1, Tn * Bs, D)
    st_p, st_s = [], []
    for l in range(depth):
        final = l == depth - 1
        xp, new_p = _prompt_layer(xp, mods[l, :Bp], l, W, tabs_p, final)
        xs, new_s = _sample_layer(xs, mods[l, Bp:], l, W, tabs_s, caches, states, page_table, final, Bs, Tn)
        st_p.append(new_p)
        st_s.append(new_s)
    y_prompt = xp
    y_sample = xs.reshape(Tn, Bs, D).transpose(1, 0, 2)
    keys = ("diff_k", "diff_v", "dsa_k", "dsa_v", "idx_k", "lru_h", "lru_conv", "gla", "ffn_conv")
    P = [jnp.stack([s[k] for s in st_p]) for k in keys]
    S = [jnp.stack([s[k] for s in st_s]) for k in keys]
    return (y_prompt, y_sample, *P, *S)
```

```python
import functools
import math

import jax
import jax.numpy as jnp
from jax import lax
from jax.experimental import pallas as pl
from jax.experimental.pallas import tpu as pltpu

F32 = jnp.float32
BF16 = jnp.bfloat16
I32 = jnp.int32

D_MODEL = 1024
GROUP_W = 256
LRU_CONV = 4
LRU_C = 8.0
DIFF_HEADS = 4
DIFF_QK = 32
DSA_HEADS = 4
DSA_HD = 64
IDX_HEADS = 8
IDX_DIM = 32
TOPK_MAX = 256
GLA_HEADS = 4
GLA_DK = 32
GLA_DV = 64
GLA_TAU = 16.0
GLA_CHUNK = 64
D_FF = 2816
FFN_CONV = 3
ROPE_THETA = 10000.0
EPS = 1e-6
PAGE = 128

P_LRU, P_DIFF_QK, P_DSA_QK, P_IDX, P_GLA_QK, P_GLA_V, P_GLA_G, P_MISC, P_END = (
    0, 512, 1024, 1536, 2048, 2304, 2560, 2816, 2944)
PAGES_PER_STEP = 16
HEADS_PER_STAGE = 8

VMEM_LIMIT_BYTES = 56 * 1024 * 1024
NEG_BIG = -1e30
LOG2E = 1.4426950408889634
INT_MIN = -2 ** 31
KEY_NEG_INF = -2 ** 31 + 0x7FFFFF
BIG_COL = 2 ** 30


def _cparams(*sem):
    return pltpu.CompilerParams(dimension_semantics=sem, vmem_limit_bytes=VMEM_LIMIT_BYTES)


def _rms(x, g):
    return x * lax.rsqrt(jnp.mean(x * x, axis=-1, keepdims=True) + EPS) * g


def _silu(x):
    return x * jax.nn.sigmoid(x)


def _softplus(z):
    return jnp.maximum(z, 0.0) + jnp.log1p(jnp.exp(-jnp.abs(z)))


def _dot(a, b):
    return jnp.dot(a, b, preferred_element_type=F32)


def _dot_nt(a, b):
    return lax.dot_general(a, b, (((1,), (1,)), ((), ())), preferred_element_type=F32)


def _group_mean64(x2, ones_bd):
    hi = x2.astype(BF16)
    lo = (x2 - hi.astype(F32)).astype(BF16)
    return (_dot(hi, ones_bd) + _dot(lo, ones_bd)) * (1.0 / 64.0)


def _rope(x, cos, sin_signed, half):
    n = x.shape[1]
    reps = n // 128
    c = jnp.concatenate([cos] * reps, axis=1) if reps > 1 else cos
    s = jnp.concatenate([sin_signed] * reps, axis=1) if reps > 1 else sin_signed
    lane = lax.broadcasted_iota(I32, x.shape, 1)
    first = (lane % (2 * half)) < half
    swapped = jnp.where(first, pltpu.roll(x, n - half, 1), pltpu.roll(x, half, 1))
    return x * c + swapped * s


def _shift_rows(x, s, prev8):
    r = pltpu.roll(x, s, 0)
    p = pltpu.roll(prev8, s, 0)
    row = lax.broadcasted_iota(I32, (8, x.shape[1]), 0)
    top = jnp.where(row < s, p, r[:8])
    return jnp.concatenate([top, r[8:]], axis=0)


def _cumsum_rows(x):
    n = x.shape[0]
    row = lax.broadcasted_iota(I32, x.shape, 0)
    s = 1
    while s < n:
        x = x + jnp.where(row >= s, pltpu.roll(x, s, 0), 0.0)
        s *= 2
    return x


def _sort_key(x):
    bits = pltpu.bitcast(x, I32)
    return bits ^ ((bits >> 31) & 0x7FFFFFFF)


def _ada_kernel(c_ref, w_ref, b_ref, o_ref):
    c = c_ref[...]
    o_ref[...] = _dot(_silu(c).astype(BF16), w_ref[...].astype(BF16)) + b_ref[...]


def _ada(c_all, w_ada, b_ada):
    depth = w_ada.shape[0]
    n = c_all.shape[0]
    tn = 1024
    return pl.pallas_call(
        _ada_kernel,
        grid=(depth, 6 * D_MODEL // tn),
        in_specs=[pl.BlockSpec((n, D_MODEL), lambda l, j: (0, 0)),
                  pl.BlockSpec((None, D_MODEL, tn), lambda l, j: (l, 0, j)),
                  pl.BlockSpec((None, 1, tn), lambda l, j: (l, 0, j))],
        out_specs=pl.BlockSpec((None, n, tn), lambda l, j: (l, 0, j)),
        out_shape=jax.ShapeDtypeStruct((depth, n, 6 * D_MODEL), F32),
        compiler_params=_cparams("arbitrary", "arbitrary"),
        name="ada_mod",
    )(c_all, w_ada, b_ada.reshape(depth, 1, 6 * D_MODEL))


def _proj_kernel(x_ref, sh_ref, sc_ref, g_ref, w_ref, wvT_ref, c32_ref, s32_ref, c64_ref, s64_ref,
                 lru_ref, qb_ref, kb_ref, vbT_ref, qc_ref, kc_ref, vcT_ref, iq_ref, ikt_ref,
                 gqk_ref, gv_ref, gg_ref, misc_ref):
    x = x_ref[0]
    h = _rms(x, g_ref[...]) * (1.0 + sc_ref[0]) + sh_ref[0]
    hb = h.astype(BF16)

    def mm(a, b):
        return _dot(hb, w_ref[:, a:b])

    lru_ref[0] = mm(P_LRU, P_DIFF_QK)
    c32, s32, c64, s64 = c32_ref[...], s32_ref[...], c64_ref[...], s64_ref[...]
    qk = _rope(mm(P_DIFF_QK, P_DSA_QK), c32, s32, 16)
    qb_ref[0] = qk[:, :256]
    kb_ref[0] = qk[:, 256:]
    qk = _rope(mm(P_DSA_QK, P_IDX), c64, s64, 32)
    qc_ref[0] = qk[:, :256]
    kc_ref[0] = qk[:, 256:]
    qk = _rope(mm(P_IDX, P_GLA_QK), c32, s32, 16)
    iq_ref[0] = qk[:, :256]
    ikt_ref[0] = qk[:, 256:]
    gqk_ref[0] = mm(P_GLA_QK, P_GLA_V)
    gv_ref[0] = mm(P_GLA_V, P_GLA_G)
    gg_ref[0] = mm(P_GLA_G, P_MISC)
    misc_ref[0] = mm(P_MISC, P_END)
    vbT_ref[0] = _dot_nt(wvT_ref[0:256, :], hb)
    vcT_ref[0] = _dot_nt(wvT_ref[256:512, :], hb)


_PROJ_WIDTHS = (512, 256, 256, None, 256, 256, None, 256, 256, 256, 256, 256, 128)


def _proj(x, sh, sc, norm_g, w_packed, wvT, l, tabs, tm):
    G, T, D = x.shape
    per_row = sh.shape[1] != 1
    mod_spec = (pl.BlockSpec((1, tm, D), lambda b, i: (b, i, 0)) if per_row
                else pl.BlockSpec((1, 1, D), lambda b, i: (b, 0, 0)))
    tab_spec = pl.BlockSpec((tm, 128), lambda b, i: (i, 0))
    out_specs = [pl.BlockSpec((1, 256, tm), lambda b, i: (b, 0, i)) if w is None
                 else pl.BlockSpec((1, tm, w), lambda b, i: (b, i, 0)) for w in _PROJ_WIDTHS]
    out_shape = [jax.ShapeDtypeStruct((G, 256, T) if w is None else (G, T, w), F32) for w in _PROJ_WIDTHS]
    return pl.pallas_call(
        _proj_kernel,
        grid=(G, T // tm),
        in_specs=[pl.BlockSpec((1, tm, D), lambda b, i: (b, i, 0)), mod_spec, mod_spec,
                  pl.BlockSpec((None, 1, D), lambda b, i: (l, 0, 0)),
                  pl.BlockSpec((None, D, P_END), lambda b, i: (l, 0, 0)),
                  pl.BlockSpec((None, 512, D), lambda b, i: (l, 0, 0)),
                  tab_spec, tab_spec, tab_spec, tab_spec],
        out_specs=out_specs, out_shape=out_shape,
        compiler_params=_cparams("arbitrary", "arbitrary"),
        name="in_proj",
    )(x, sh, sc, norm_g, w_packed, wvT, *tabs)


def _lru_gates(xa, gate_in, wa_ref, wx_ref, ba_ref, bx_ref, lam_ref):
    xb = xa.astype(BF16)
    ga = jax.nn.sigmoid(_dot(xb, wa_ref[...]) + ba_ref[...])
    gx = jax.nn.sigmoid(_dot(xb, wx_ref[...]) + bx_ref[...])
    log_a = (-LRU_C) * ga * _softplus(-lam_ref[...])
    a = jnp.exp(log_a)
    t = jnp.tanh(log_a)
    mult = jnp.sqrt(-2.0 * t / (1.0 - t))
    return a, mult, gx


def _lru_seq_kernel(lru_ref, cw_ref, cb_ref, wa_ref, wx_ref, ba_ref, bx_ref, lam_ref, h0_ref, buf_ref,
                    o_ref, hl_ref, cs_ref, hc_ref, cbuf_ref, *, tm):
    i = pl.program_id(1)

    @pl.when(i == 0)
    def _():
        hc_ref[...] = jnp.broadcast_to(h0_ref[0], hc_ref.shape)
        cbuf_ref[...] = buf_ref[0]

    xg = lru_ref[0]
    x = xg[:, :256]
    gate = xg[:, 256:]
    prev8 = cbuf_ref[...]
    w = cw_ref[...]
    xa = (cb_ref[...] + x * w[3:4] + _shift_rows(x, 1, prev8) * w[2:3]
          + _shift_rows(x, 2, prev8) * w[1:2] + _shift_rows(x, 3, prev8) * w[0:1])
    cbuf_ref[...] = x[tm - 8:]
    cs_ref[0] = x[tm - 8:]

    a, mult, gx = _lru_gates(xa, gate, wa_ref, wx_ref, ba_ref, bx_ref, lam_ref)
    row = lax.broadcasted_iota(I32, (tm, 256), 0)
    mult = jnp.where((row + i * tm) == 0, 1.0, mult)
    u = mult * gx * xa
    s = 1
    while s < tm:
        keep = row >= s
        a_sh = jnp.where(keep, pltpu.roll(a, s, 0), 1.0)
        u_sh = jnp.where(keep, pltpu.roll(u, s, 0), 0.0)
        u = a * u_sh + u
        a = a * a_sh
        s *= 2
    h = a * hc_ref[0:1, :] + u
    hlast = h[tm - 1:tm]
    hc_ref[...] = jnp.broadcast_to(hlast, hc_ref.shape)
    hl_ref[0] = hlast
    o_ref[0] = h * jax.nn.gelu(gate, approximate=True)


def _lru_seq(lru, lw, l, h0, buf8, tm):
    G, T, _ = lru.shape
    wspec = lambda shape: pl.BlockSpec((None,) + shape, lambda b, i: (l,) + (0,) * len(shape))
    return pl.pallas_call(
        functools.partial(_lru_seq_kernel, tm=tm),
        grid=(G, T // tm),
        in_specs=[pl.BlockSpec((1, tm, 512), lambda b, i: (b, i, 0)),
                  wspec((LRU_CONV, 256)), wspec((1, 256)), wspec((256, 256)), wspec((256, 256)),
                  wspec((1, 256)), wspec((1, 256)), wspec((1, 256)),
                  pl.BlockSpec((1, 1, 256), lambda b, i: (b, 0, 0)),
                  pl.BlockSpec((1, 8, 256), lambda b, i: (b, 0, 0))],
        out_specs=[pl.BlockSpec((1, tm, 256), lambda b, i: (b, i, 0)),
                   pl.BlockSpec((1, 1, 256), lambda b, i: (b, 0, 0)),
                   pl.BlockSpec((1, 8, 256), lambda b, i: (b, 0, 0))],
        out_shape=[jax.ShapeDtypeStruct((G, T, 256), F32), jax.ShapeDtypeStruct((G, 1, 256), F32),
                   jax.ShapeDtypeStruct((G, 8, 256), F32)],
        scratch_shapes=[pltpu.VMEM((8, 256), F32), pltpu.VMEM((8, 256), F32)],
        compiler_params=_cparams("arbitrary", "arbitrary"),
        name="lru_seq",
    )(lru, lw["conv_w"], lw["conv_b"], lw["wa"], lw["wx"], lw["ba"], lw["bx"], lw["lam"], h0, buf8)


def _lru_rows_kernel(xx_ref, gate_ref, cw_ref, cb_ref, wa_ref, wx_ref, ba_ref, bx_ref, lam_ref, h0_ref,
                     o_ref, hl_ref, *, nb, nt):
    xx = xx_ref[...]
    w = cw_ref[...]
    n = nt * nb
    xa = cb_ref[...]
    for j in range(LRU_CONV):
        xa = xa + xx[j * nb:j * nb + n] * w[j:j + 1]
    gate = gate_ref[...]
    a, mult, gx = _lru_gates(xa, gate, wa_ref, wx_ref, ba_ref, bx_ref, lam_ref)
    u = mult * gx * xa
    h = h0_ref[...]
    for t in range(nt):
        h = a[t * nb:(t + 1) * nb] * h + u[t * nb:(t + 1) * nb]
        o_ref[t * nb:(t + 1) * nb, :] = h * jax.nn.gelu(gate[t * nb:(t + 1) * nb], approximate=True)
    hl_ref[...] = h


def _lru_rows(xx, gate, lw, l, h0, nb, nt):
    n = nt * nb
    full = lambda a: pl.BlockSpec(a.shape, lambda i: (0,) * a.ndim)
    wspec = lambda shape: pl.BlockSpec((None,) + shape, lambda i: (l,) + (0,) * len(shape))
    return pl.pallas_call(
        functools.partial(_lru_rows_kernel, nb=nb, nt=nt),
        grid=(1,),
        in_specs=[full(xx), full(gate), wspec((LRU_CONV, 256)), wspec((1, 256)), wspec((256, 256)),
                  wspec((256, 256)), wspec((1, 256)), wspec((1, 256)), wspec((1, 256)), full(h0)],
        out_specs=[pl.BlockSpec((n, 256), lambda i: (0, 0)), pl.BlockSpec((nb, 256), lambda i: (0, 0))],
        out_shape=[jax.ShapeDtypeStruct((n, 256), F32), jax.ShapeDtypeStruct((nb, 256), F32)],
        compiler_params=_cparams("arbitrary"),
        name="lru_rows",
    )(xx, gate, lw["conv_w"], lw["conv_b"], lw["wa"], lw["wx"], lw["ba"], lw["bx"], lw["lam"], h0)


def _diff_finalize(acc_ref, l_ref, lam, osc_ref, g_ref, ones_ref, out_scale, rows, wide):
    for h in range(DIFF_HEADS):
        r1 = slice((2 * h) * rows, (2 * h + 1) * rows)
        r2 = slice((2 * h + 1) * rows, (2 * h + 2) * rows)
        cs = slice(64 * h, 64 * h + 64) if wide else slice(0, 64)
        o1 = acc_ref[r1, cs] / l_ref[r1, :]
        o2 = acc_ref[r2, cs] / l_ref[r2, :]
        osc_ref[:, 64 * h:64 * h + 64] = o1 - lam * o2
    o = osc_ref[...]
    ms = _group_mean64(o * o, ones_ref[...])
    return o * lax.rsqrt(ms + EPS) * g_ref[...] * out_scale


def _softmax_steps(ss, sel, m_ref, l_ref, acc_ref, idxs, vTs, c):
    n = range(len(ss))
    m_prev = [m_ref[i:i + 1, :] for i in idxs]
    l_prev = [l_ref[i:i + 1, :] for i in idxs]
    if sel is not None:
        ss = [jnp.where(sel, s, NEG_BIG) for s in ss]
    m_new = [jnp.maximum(m_prev[a], jnp.max(ss[a], axis=0, keepdims=True)) for a in n]
    alpha = [jnp.exp2((m_prev[a] - m_new[a]) * c) for a in n]
    p = [jnp.exp2((ss[a] - m_new[a]) * c) for a in n]
    l_new = [alpha[a] * l_prev[a] + jnp.sum(p[a], axis=0, keepdims=True) for a in n]
    pv = [_dot(vTs[a], p[a].astype(BF16)) for a in n]
    acc_new = [alpha[a] * acc_ref[idxs[a]] + pv[a] for a in n]
    for a in n:
        i = idxs[a]
        m_ref[i:i + 1, :] = m_new[a]
        l_ref[i:i + 1, :] = l_new[a]
        acc_ref[i] = acc_new[a]


def _diff_attn_kernel(lam_ref, q_ref, k_ref, vT_ref, g_ref, o_ref,
                      qmT_ref, m_ref, l_ref, acc_ref, osc_ref, *, tq, out_scale):
    i = pl.program_id(1)
    qT = q_ref[0].T
    feat = lax.broadcasted_iota(I32, (256, tq), 0)
    for hm in range(8):
        qmT_ref[hm] = jnp.where((feat // DIFF_QK) == hm, qT, 0.0).astype(BF16)
    m_ref[...] = jnp.full(m_ref.shape, NEG_BIG, F32)
    l_ref[...] = jnp.zeros(l_ref.shape, F32)
    acc_ref[...] = jnp.zeros(acc_ref.shape, F32)
    krow = lax.broadcasted_iota(I32, (tq, tq), 0)
    qcol = lax.broadcasted_iota(I32, (tq, tq), 1)
    c = (DIFF_QK ** -0.5) * LOG2E

    def block(j, masked):
        r0 = pl.multiple_of(j * tq, tq)
        kj = k_ref[0, pl.ds(r0, tq), :].astype(BF16)
        vjT = vT_ref[0, :, pl.ds(r0, tq)].astype(BF16)
        for h0 in range(0, 8, HEADS_PER_STAGE):
            hms = list(range(h0, h0 + HEADS_PER_STAGE))
            ss = [_dot(kj, qmT_ref[hm]) for hm in hms]
            if masked:
                ss = [jnp.where(krow <= qcol, s, NEG_BIG) for s in ss]
            vTs = [vjT[64 * (hm // 2):64 * (hm // 2) + 64, :] for hm in hms]
            _softmax_steps(ss, None, m_ref, l_ref, acc_ref, hms, vTs, c)

    def loop_body(j, carry):
        block(j, False)
        return carry

    lax.fori_loop(0, i, loop_body, 0)
    block(i, True)
    lam = lam_ref[0, 0]
    for h in range(DIFF_HEADS):
        o = (acc_ref[2 * h] / l_ref[2 * h:2 * h + 1, :]
             - lam * (acc_ref[2 * h + 1] / l_ref[2 * h + 1:2 * h + 2, :]))
        ms = jnp.mean(o * o, axis=0, keepdims=True)
        osc_ref[64 * h:64 * h + 64, :] = o * lax.rsqrt(ms + EPS)
    o_ref[0] = osc_ref[...].T * g_ref[...] * out_scale


def _diff_attn(lam, q, k, vT, g_tiled, l, out_scale, tq):
    G, T, _ = q.shape
    return pl.pallas_call(
        functools.partial(_diff_attn_kernel, tq=tq, out_scale=out_scale),
        grid=(G, T // tq),
        in_specs=[pl.BlockSpec(memory_space=pltpu.SMEM),
                  pl.BlockSpec((1, tq, 256), lambda b, i: (b, i, 0)),
                  pl.BlockSpec((1, T, 256), lambda b, i: (b, 0, 0)),
                  pl.BlockSpec((1, 256, T), lambda b, i: (b, 0, 0)),
                  pl.BlockSpec((None, 1, 256), lambda b, i: (l, 0, 0))],
        out_specs=pl.BlockSpec((1, tq, 256), lambda b, i: (b, i, 0)),
        out_shape=jax.ShapeDtypeStruct((G, T, 256), F32),
        scratch_shapes=[pltpu.VMEM((8, 256, tq), BF16), pltpu.VMEM((8, tq), F32),
                        pltpu.VMEM((8, tq), F32), pltpu.VMEM((8, 64, tq), F32),
                        pltpu.VMEM((256, tq), F32)],
        compiler_params=_cparams("arbitrary", "arbitrary"),
        name="diff_attn",
    )(lam, q, k, vT, g_tiled)


def _topk_threshold(count_ge, count_tie_below, topk, n_col_bits, shape):
    kf = float(topk)
    zero = jnp.zeros(shape, I32)
    tau = jnp.where(count_ge(zero) >= kf, zero, zero + INT_MIN)

    def bit_body(b, tau):
        cand = tau + lax.shift_left(jnp.int32(1), 30 - b)
        return jnp.where(count_ge(cand) >= kf, cand, tau)

    tau = lax.fori_loop(0, 31, bit_body, tau)
    cnt_ge = count_ge(tau)
    cnt_gt = count_ge(tau + 1)
    need = kf - cnt_gt
    has_tie = jnp.where(tau > KEY_NEG_INF, jnp.where(cnt_ge - cnt_gt > need, 1.0, 0.0), 0.0)

    def tie_search():
        def tie_body(b, g):
            cand = g + lax.shift_left(jnp.int32(1), n_col_bits - 1 - b)
            return jnp.where(count_tie_below(tau, cand) < need, cand, g)
        return lax.fori_loop(0, n_col_bits, tie_body, zero)

    g_tie = lax.cond(jnp.max(has_tie) > 0.0, tie_search, lambda: zero)
    g = jnp.where(tau == KEY_NEG_INF, -1, jnp.where(has_tie > 0.0, g_tie, BIG_COL))
    return tau, g


def _dsa_kernel(q_ref, k_ref, vT_ref, iq_ref, ikt_ref, misc_ref, o_ref,
                qmT_ref, iqmT_ref, key_ref, m_ref, l_ref, acc_ref, osc_ref, *, tq, topk, n_col_bits):
    i = pl.program_id(1)
    nblk = i + 1
    feat = lax.broadcasted_iota(I32, (256, tq), 0)
    iqT = iq_ref[0].T
    for h in range(IDX_HEADS):
        iqmT_ref[h] = jnp.where((feat // IDX_DIM) == h, iqT, 0.0).astype(BF16)
    qT = q_ref[0].T
    for h in range(DSA_HEADS):
        qmT_ref[h] = jnp.where((feat // DSA_HD) == h, qT, 0.0).astype(BF16)
    wT = misc_ref[0].T[0:IDX_HEADS, :] * ((IDX_HEADS ** -0.5) * (IDX_DIM ** -0.5))
    krow = lax.broadcasted_iota(I32, (tq, tq), 0)
    qcol = lax.broadcasted_iota(I32, (tq, tq), 1)

    def score_block(j, masked):
        r0 = pl.multiple_of(j * tq, tq)
        ikj = ikt_ref[0, pl.ds(r0, tq), :].astype(BF16)
        tot = jnp.zeros((tq, tq), F32)
        for h in range(IDX_HEADS):
            tot = tot + wT[h:h + 1, :] * jnp.maximum(_dot(ikj, iqmT_ref[h]), 0.0)
        tot = jnp.where(tot == 0.0, 0.0, tot)
        if masked:
            tot = jnp.where(krow <= qcol, tot, -jnp.inf)
        key_ref[j] = _sort_key(tot)

    def score_body(j, carry):
        score_block(j, False)
        return carry

    lax.fori_loop(0, i, score_body, 0)
    score_block(i, True)

    @pl.when(nblk % 2 == 1)
    def _():
        key_ref[nblk] = jnp.full((tq, tq), INT_MIN, I32)

    def count_rows(hit_fn):
        def body(jj, acc):
            a = jnp.sum(hit_fn(2 * jj).reshape(tq // 8, 8, tq), axis=0)
            b = jnp.sum(hit_fn(2 * jj + 1).reshape(tq // 8, 8, tq), axis=0)
            return acc + (a + b)
        acc = lax.fori_loop(0, (nblk + 1) // 2, body, jnp.zeros((8, tq), F32))
        return jnp.sum(acc, axis=0, keepdims=True)

    def count_ge(c):
        return count_rows(lambda j: jnp.where(key_ref[j] >= c, 1.0, 0.0))

    def count_tie_below(tau, c):
        return count_rows(lambda j: jnp.where(key_ref[j] == tau, jnp.where(krow + j * tq < c, 1.0, 0.0), 0.0))

    tau, g = _topk_threshold(count_ge, count_tie_below, topk, n_col_bits, (1, tq))

    m_ref[...] = jnp.full(m_ref.shape, NEG_BIG, F32)
    l_ref[...] = jnp.zeros(l_ref.shape, F32)
    acc_ref[...] = jnp.zeros(acc_ref.shape, F32)
    c = (DSA_HD ** -0.5) * LOG2E

    def att_body(j, carry):
        r0 = pl.multiple_of(j * tq, tq)
        sel = key_ref[j] >= tau + jnp.where(krow + j * tq > g, 1, 0)
        kj = k_ref[0, pl.ds(r0, tq), :].astype(BF16)
        vjT = vT_ref[0, :, pl.ds(r0, tq)].astype(BF16)
        hs = list(range(DSA_HEADS))
        ss = [_dot(kj, qmT_ref[h]) for h in hs]
        _softmax_steps(ss, sel, m_ref, l_ref, acc_ref, hs, [vjT[64 * h:64 * h + 64, :] for h in hs], c)
        return carry

    lax.fori_loop(0, nblk, att_body, 0)
    for h in range(DSA_HEADS):
        osc_ref[64 * h:64 * h + 64, :] = acc_ref[h] / l_ref[h:h + 1, :]
    o_ref[0] = osc_ref[...].T


def _dsa_attn(q, k, vT, iq, ikt, misc, topk, tq):
    G, T, _ = q.shape
    nb = T // tq
    return pl.pallas_call(
        functools.partial(_dsa_kernel, tq=tq, topk=topk, n_col_bits=max(1, (T - 1).bit_length())),
        grid=(G, nb),
        in_specs=[pl.BlockSpec((1, tq, 256), lambda b, i: (b, i, 0)),
                  pl.BlockSpec((1, T, 256), lambda b, i: (b, 0, 0)),
                  pl.BlockSpec((1, 256, T), lambda b, i: (b, 0, 0)),
                  pl.BlockSpec((1, tq, 256), lambda b, i: (b, i, 0)),
                  pl.BlockSpec((1, T, 256), lambda b, i: (b, 0, 0)),
                  pl.BlockSpec((1, tq, 128), lambda b, i: (b, i, 0))],
        out_specs=pl.BlockSpec((1, tq, 256), lambda b, i: (b, i, 0)),
        out_shape=jax.ShapeDtypeStruct((G, T, 256), F32),
        scratch_shapes=[pltpu.VMEM((DSA_HEADS, 256, tq), BF16), pltpu.VMEM((IDX_HEADS, 256, tq), BF16),
                        pltpu.VMEM((nb + 1, tq, tq), I32), pltpu.VMEM((8, tq), F32),
                        pltpu.VMEM((8, tq), F32), pltpu.VMEM((DSA_HEADS, 64, tq), F32),
                        pltpu.VMEM((256, tq), F32)],
        compiler_params=_cparams("arbitrary", "arbitrary"),
        name="dsa_attn",
    )(q, k, vT, iq, ikt, misc)


def _gla_kernel(qk_ref, v_ref, g_ref, misc_ref, wa_ref, ba_ref, gn_ref, ones_ref, bd_ref, s0_ref,
                gsel_ref, negq_ref, negk_ref, pair_ref, o_ref, sfin_ref, s_ref, *, tm, chunk, n_valid):
    i = pl.program_id(1)

    @pl.when(i == 0)
    def _():
        s_ref[...] = jnp.zeros(s_ref.shape, F32)
        for h in range(GLA_HEADS):
            s_ref[32 * h:32 * h + 32, 64 * h:64 * h + 64] = s0_ref[0, h]

    crow = lax.broadcasted_iota(I32, (chunk, 128), 0)
    lane_h = lax.broadcasted_iota(I32, (chunk, 128), 1) // GLA_DK
    vlane_h = lax.broadcasted_iota(I32, (chunk, 256), 1) // GLA_DV
    n_levels = chunk.bit_length() - 1

    def chunk_body(c, carry):
        r0 = pl.multiple_of(c * chunk, chunk)
        rows = pl.ds(r0, chunk)
        a_lin = _dot(misc_ref[0, rows, :].astype(BF16), wa_ref[...]) + ba_ref[...]
        la = (jnp.minimum(a_lin, 0.0) - jnp.log1p(jnp.exp(-jnp.abs(a_lin)))) * (1.0 / GLA_TAU)
        qk = qk_ref[0, rows, :]
        q = qk[:, :128] * (GLA_DK ** -0.5)
        k = qk[:, 128:]
        if n_valid is not None:
            live = (crow + r0) < n_valid
            la = jnp.where(live, la, 0.0)
            k = jnp.where(live, k, 0.0)
        v = v_ref[0, rows, :]
        vb = v.astype(BF16)
        b = _cumsum_rows(la)
        bT = b.T
        kT = k.T
        S = s_ref[...]
        o = _dot((q * jnp.exp(b)).astype(BF16), S.astype(BF16))
        b_hi = b.astype(BF16)
        r1 = b - b_hi.astype(F32)
        b_mid = r1.astype(BF16)
        b_lo = (r1 - b_mid.astype(F32)).astype(BF16)
        gsel = gsel_ref[...]
        ref_rows = _dot(gsel, b_hi) + _dot(gsel, b_mid) + _dot(gsel, b_lo)
        att = jnp.zeros((GLA_HEADS * chunk, chunk), F32)
        for j in range(n_levels):
            rj = ref_rows[j * chunk:(j + 1) * chunk]
            qt = q * jnp.exp((b - rj) + negq_ref[j])
            kh = k * jnp.exp((rj - b) + negk_ref[j])
            qs = jnp.concatenate([jnp.where(lane_h == h, qt, 0.0) for h in range(GLA_HEADS)], axis=0)
            att = att + _dot_nt(qs.astype(BF16), kh.astype(BF16)) * pair_ref[j]
        x = _dot(att.astype(BF16), vb)
        for h in range(GLA_HEADS):
            o = o + jnp.where(vlane_h == h, x[h * chunk:(h + 1) * chunk], 0.0)
        o = o + _dot((q * k).astype(BF16), bd_ref[...].astype(BF16)) * v
        bendT = bT[:, chunk - 1:chunk]
        kdT = kT * jnp.exp(bendT - bT)
        s_ref[...] = jnp.exp(bendT) * S + _dot(kdT.astype(BF16), vb) * bd_ref[...]
        ms = _group_mean64(o * o, ones_ref[...])
        o_ref[0, rows, :] = o * lax.rsqrt(ms + EPS) * gn_ref[...] * _silu(g_ref[0, rows, :])
        return carry

    lax.fori_loop(0, tm // chunk, chunk_body, 0)

    @pl.when(i == pl.num_programs(1) - 1)
    def _():
        for h in range(GLA_HEADS):
            sfin_ref[0, h] = s_ref[32 * h:32 * h + 32, 64 * h:64 * h + 64]


def _gla_tables(chunk):
    t = jnp.arange(chunk)
    gsel, negq, negk, pair = [], [], [], []
    m = chunk // 2
    while m >= 1:
        blk, second = t // (2 * m), (t % (2 * m)) >= m
        gsel.append((t[None, :] == (blk * 2 * m + m - 1)[:, None]).astype(BF16))
        negq.append(jnp.broadcast_to(jnp.where(second, 0.0, -jnp.inf)[:, None], (chunk, 128)))
        negk.append(jnp.broadcast_to(jnp.where(second, -jnp.inf, 0.0)[:, None], (chunk, 128)))
        pair.append(jnp.tile((blk[:, None] == blk[None, :]).astype(F32), (GLA_HEADS, 1)))
        m //= 2
    return (jnp.concatenate(gsel, axis=0), jnp.stack(negq).astype(F32), jnp.stack(negk).astype(F32),
            jnp.stack(pair))


def _gla(gqk, gv, gg, misc, gw, l, s0, ones_bd, bd_mask, tm, chunk, n_valid):
    G, T, _ = gqk.shape
    wspec = lambda shape: pl.BlockSpec((None,) + shape, lambda b, i: (l,) + (0,) * len(shape))
    tables = _gla_tables(chunk)
    const = lambda a: pl.BlockSpec(a.shape, lambda b, i: (0,) * a.ndim)
    return pl.pallas_call(
        functools.partial(_gla_kernel, tm=tm, chunk=chunk, n_valid=n_valid),
        grid=(G, T // tm),
        in_specs=[pl.BlockSpec((1, tm, 256), lambda b, i: (b, i, 0)),
                  pl.BlockSpec((1, tm, 256), lambda b, i: (b, i, 0)),
                  pl.BlockSpec((1, tm, 256), lambda b, i: (b, i, 0)),
                  pl.BlockSpec((1, tm, 128), lambda b, i: (b, i, 0)),
                  wspec((128, 128)), wspec((1, 128)), wspec((1, 256)),
                  pl.BlockSpec((256, 256), lambda b, i: (0, 0)),
                  pl.BlockSpec((128, 256), lambda b, i: (0, 0)),
                  pl.BlockSpec((1, GLA_HEADS, GLA_DK, GLA_DV), lambda b, i: (b, 0, 0, 0))]
                 + [const(a) for a in tables],
        out_specs=[pl.BlockSpec((1, tm, 256), lambda b, i: (b, i, 0)),
                   pl.BlockSpec((1, GLA_HEADS, GLA_DK, GLA_DV), lambda b, i: (b, 0, 0, 0))],
        out_shape=[jax.ShapeDtypeStruct((G, T, 256), F32),
                   jax.ShapeDtypeStruct((G, GLA_HEADS, GLA_DK, GLA_DV), F32)],
        scratch_shapes=[pltpu.VMEM((128, 256), F32)],
        compiler_params=_cparams("arbitrary", "arbitrary"),
        name="gla",
    )(gqk, gv, gg, misc, gw["wa"], gw["ba"], gw["gn"], ones_bd, bd_mask, s0, *tables)


def _ffn_kernel(*refs, tm, seq, final, nb):
    (x_ref, oa_ref, ob_ref, oc_ref, od_ref, g1_ref, sh2_ref, sc2_ref, g2_ref, n2_ref,
     wout_ref, wg_ref, wu_ref, wd_ref, cw_ref, cb_ref, buf_ref, fng_ref, y_ref, cs_ref) = refs[:20]
    x = x_ref[0]
    mix = _dot(oa_ref[0].astype(BF16), wout_ref[0:256, :])
    mix = mix + _dot(ob_ref[0].astype(BF16), wout_ref[256:512, :])
    mix = mix + _dot(oc_ref[0].astype(BF16), wout_ref[512:768, :])
    mix = mix + _dot(od_ref[0].astype(BF16), wout_ref[768:1024, :])
    x1 = x + g1_ref[0] * mix
    h2 = _rms(x1, n2_ref[...]) * (1.0 + sc2_ref[0]) + sh2_ref[0]
    hb = h2.astype(BF16)
    gate = _dot(hb, wg_ref[...])
    w = cw_ref[...]
    if seq:
        cbuf_ref = refs[20]

        @pl.when(pl.program_id(1) == 0)
        def _():
            cbuf_ref[...] = buf_ref[0]

        prev8 = cbuf_ref[...]
        z = cb_ref[...] + gate * w[2:3] + _shift_rows(gate, 1, prev8) * w[1:2] + _shift_rows(gate, 2, prev8) * w[0:1]
        cbuf_ref[...] = gate[tm - 8:]
        cs_ref[0] = gate[tm - 8:]
    else:
        xx = jnp.concatenate([buf_ref[0], gate], axis=0)
        z = cb_ref[...] + xx[0:tm] * w[0:1] + xx[nb:nb + tm] * w[1:2] + xx[2 * nb:2 * nb + tm] * w[2:3]
        cs_ref[0] = xx[tm:tm + 2 * nb]
    up = _dot(hb, wu_ref[...])
    y = _dot((_silu(z) * up).astype(BF16), wd_ref[...])
    x2 = x1 + g2_ref[0] * y
    if final:
        x2 = _rms(x2, fng_ref[...])
    y_ref[0] = x2


def _ffn(x, o4, mods, fw, l, buf, final_g, tm, seq, final, nb=0):
    G, T, D = x.shape
    per_row = mods[0].shape[1] != 1
    mod_spec = (pl.BlockSpec((1, tm, D), lambda b, i: (b, i, 0)) if per_row
                else pl.BlockSpec((1, 1, D), lambda b, i: (b, 0, 0)))
    o_spec = pl.BlockSpec((1, tm, 256), lambda b, i: (b, i, 0))
    once = pl.Buffered(1)
    wspec = lambda shape: pl.BlockSpec((None,) + shape, lambda b, i: (l,) + (0,) * len(shape), pipeline_mode=once)
    nbuf = buf.shape[1]
    return pl.pallas_call(
        functools.partial(_ffn_kernel, tm=tm, seq=seq, final=final, nb=nb),
        grid=(G, T // tm),
        in_specs=[pl.BlockSpec((1, tm, D), lambda b, i: (b, i, 0)), o_spec, o_spec, o_spec, o_spec,
                  mod_spec, mod_spec, mod_spec, mod_spec,
                  wspec((1, D)), wspec((D, D)), wspec((D, D_FF)), wspec((D, D_FF)), wspec((D_FF, D)),
                  wspec((FFN_CONV, D_FF)), wspec((1, D_FF)),
                  pl.BlockSpec((1, nbuf, D_FF), lambda b, i: (b, 0, 0)),
                  pl.BlockSpec((1, D), lambda b, i: (0, 0))],
        out_specs=[pl.BlockSpec((1, tm, D), lambda b, i: (b, i, 0)),
                   pl.BlockSpec((1, nbuf, D_FF), lambda b, i: (b, 0, 0))],
        out_shape=[jax.ShapeDtypeStruct((G, T, D), F32), jax.ShapeDtypeStruct((G, nbuf, D_FF), F32)],
        scratch_shapes=[pltpu.VMEM((8, D_FF), F32)] if seq else [],
        compiler_params=_cparams("arbitrary", "arbitrary"),
        name="out_ffn",
    )(x, *o4, *mods, fw["n2"], fw["wout"], fw["wg"], fw["wu"], fw["wd"], fw["cw"], fw["cb"], buf, final_g)


def _diff_dec_kernel(pt_ref, lam_ref, q_ref, *refs, n_new, out_scale, nps):
    kT_refs, vT_refs = refs[:nps], refs[nps:2 * nps]
    kn_ref, vnT_ref, g_ref, ones_ref, o_ref, qs_ref, m_ref, l_ref, acc_ref, osc_ref = refs[2 * nps:]
    p = pl.program_id(1)
    lane = lax.broadcasted_iota(I32, (8, 256), 1)

    @pl.when(p == 0)
    def _():
        q = q_ref[0]
        for hm in range(8):
            qs_ref[8 * hm:8 * hm + 8, :] = jnp.where((lane // 32) == hm, q, 0.0).astype(BF16)
        m_ref[...] = jnp.full(m_ref.shape, NEG_BIG, F32)
        l_ref[...] = jnp.zeros(l_ref.shape, F32)
        acc_ref[...] = jnp.zeros(acc_ref.shape, F32)

    def update(s, v_contract):
        m_prev = m_ref[...]
        m_new = jnp.maximum(m_prev, jnp.max(s, axis=1, keepdims=True))
        alpha = jnp.exp(m_prev - m_new)
        pr = jnp.exp(s - m_new)
        l_ref[...] = alpha * l_ref[...] + jnp.sum(pr, axis=1, keepdims=True)
        acc_ref[...] = alpha * acc_ref[...] + v_contract(pr.astype(BF16))
        m_ref[...] = m_new

    qs = qs_ref[...]
    kcat = jnp.concatenate([r[0, 0].astype(BF16) for r in kT_refs], axis=1)
    vcat = jnp.concatenate([r[0, 0].astype(BF16) for r in vT_refs], axis=1)
    update(_dot(qs, kcat) * (DIFF_QK ** -0.5), lambda pr: _dot_nt(pr, vcat))

    @pl.when(p == pl.num_programs(1) - 1)
    def _():
        s = _dot_nt(qs, kn_ref[0].astype(BF16)) * (DIFF_QK ** -0.5)
        t = lax.broadcasted_iota(I32, s.shape, 0) % 8
        c = lax.broadcasted_iota(I32, s.shape, 1)
        s = jnp.where((c <= t) & (c < n_new), s, NEG_BIG)
        vnT = vnT_ref[0].astype(BF16)
        update(s, lambda pr: _dot_nt(pr, vnT))
        o_ref[0] = _diff_finalize(acc_ref, l_ref, lam_ref[0, 0], osc_ref, g_ref, ones_ref, out_scale, 8, True)


def _page_specs(l, nps, rows):
    return [pl.BlockSpec((1, 1, rows, PAGE), lambda b, p, pt, i=i: (l, pt[b, p * nps + i], 0, 0))
            for i in range(nps)]


def _diff_dec(page_table, lam, q8, ckT, cvT, kn, vnT, g_tiled, ones_bd, l, out_scale, n_new):
    B, NP = page_table.shape
    nps = math.gcd(NP, PAGES_PER_STEP)
    gs = pltpu.PrefetchScalarGridSpec(
        num_scalar_prefetch=1, grid=(B, NP // nps),
        in_specs=[pl.BlockSpec(memory_space=pltpu.SMEM),
                  pl.BlockSpec((1, 8, 256), lambda b, p, pt: (b, 0, 0))]
                 + _page_specs(l, nps, 256) + _page_specs(l, nps, 256)
                 + [pl.BlockSpec((1, 128, 256), lambda b, p, pt: (b, 0, 0)),
                    pl.BlockSpec((1, 256, 128), lambda b, p, pt: (b, 0, 0)),
                    pl.BlockSpec((None, 1, 256), lambda b, p, pt: (l, 0, 0)),
                    pl.BlockSpec((256, 256), lambda b, p, pt: (0, 0))],
        out_specs=pl.BlockSpec((1, 8, 256), lambda b, p, pt: (b, 0, 0)),
        scratch_shapes=[pltpu.VMEM((64, 256), BF16), pltpu.VMEM((64, 1), F32), pltpu.VMEM((64, 1), F32),
                        pltpu.VMEM((64, 256), F32), pltpu.VMEM((8, 256), F32)])
    return pl.pallas_call(
        functools.partial(_diff_dec_kernel, n_new=n_new, out_scale=out_scale, nps=nps),
        grid_spec=gs, out_shape=jax.ShapeDtypeStruct((B, 8, 256), F32),
        compiler_params=_cparams("arbitrary", "arbitrary"),
        name="diff_decode",
    )(page_table, lam, q8, *([ckT] * nps), *([cvT] * nps), kn, vnT, g_tiled, ones_bd)


def _idx_dec_kernel(pt_ref, iq_ref, w_ref, *refs, n_new, nps):
    ikT_refs = refs[:nps]
    iknT_ref, keys_ref, keysn_ref = refs[nps:]
    p = pl.program_id(1)
    iq = iq_ref[0].astype(BF16)
    w = w_ref[0]

    def scores(ikT):
        sc = _dot(iq, ikT.astype(BF16)) * (IDX_DIM ** -0.5)
        tot = jnp.sum((w * jnp.maximum(sc, 0.0)).reshape(8, 8, PAGE), axis=1)
        return jnp.where(tot == 0.0, 0.0, tot)

    for i in range(nps):
        keys_ref[0, i] = _sort_key(scores(ikT_refs[i][0, 0]))

    @pl.when(p == pl.num_programs(1) - 1)
    def _():
        tot = scores(iknT_ref[0])
        t = lax.broadcasted_iota(I32, tot.shape, 0)
        c = lax.broadcasted_iota(I32, tot.shape, 1)
        keysn_ref[0] = _sort_key(jnp.where((c <= t) & (c < n_new), tot, -jnp.inf))


def _idx_dec(page_table, iq64, w64, cikT, iknT, l, n_new):
    B, NP = page_table.shape
    nps = math.gcd(NP, PAGES_PER_STEP)
    gs = pltpu.PrefetchScalarGridSpec(
        num_scalar_prefetch=1, grid=(B, NP // nps),
        in_specs=[pl.BlockSpec((1, 64, IDX_DIM), lambda b, p, pt: (b, 0, 0)),
                  pl.BlockSpec((1, 64, 1), lambda b, p, pt: (b, 0, 0))]
                 + _page_specs(l, nps, IDX_DIM)
                 + [pl.BlockSpec((1, IDX_DIM, 128), lambda b, p, pt: (b, 0, 0))],
        out_specs=[pl.BlockSpec((1, nps, 8, PAGE), lambda b, p, pt: (b, p, 0, 0)),
                   pl.BlockSpec((1, 8, 128), lambda b, p, pt: (b, 0, 0))])
    return pl.pallas_call(
        functools.partial(_idx_dec_kernel, n_new=n_new, nps=nps),
        grid_spec=gs,
        out_shape=[jax.ShapeDtypeStruct((B, NP, 8, PAGE), I32), jax.ShapeDtypeStruct((B, 8, 128), I32)],
        compiler_params=_cparams("arbitrary", "arbitrary"),
        name="idx_decode",
    )(page_table, iq64, w64, *([cikT] * nps), iknT)


def _dsa_dec_kernel(pt_ref, q_ref, keys_ref, keysn_ref, *refs, topk, n_pages, n_col_bits, nps):
    kT_refs, vT_refs = refs[:nps], refs[nps:2 * nps]
    kn_ref, vnT_ref, o_ref, qs_ref, tg_ref, m_ref, l_ref, acc_ref = refs[2 * nps:]
    p = pl.program_id(1)
    lane = lax.broadcasted_iota(I32, (8, 256), 1)
    colp = lax.broadcasted_iota(I32, (8, PAGE), 1)

    @pl.when(p == 0)
    def _():
        q = q_ref[0]
        for h in range(DSA_HEADS):
            qs_ref[8 * h:8 * h + 8, :] = jnp.where((lane // 64) == h, q, 0.0).astype(BF16)
        m_ref[...] = jnp.full(m_ref.shape, NEG_BIG, F32)
        l_ref[...] = jnp.zeros(l_ref.shape, F32)
        acc_ref[...] = jnp.zeros(acc_ref.shape, F32)

        page_col = (lax.broadcasted_iota(I32, (n_pages, 8, PAGE), 0) * PAGE
                    + lax.broadcasted_iota(I32, (n_pages, 8, PAGE), 2))

        def count_ge(c):
            acc = (jnp.sum(jnp.where(keys_ref[0] >= c, 1.0, 0.0), axis=0)
                   + jnp.where(keysn_ref[0] >= c, 1.0, 0.0))
            return jnp.sum(acc, axis=1, keepdims=True)

        def count_tie_below(tau, c):
            acc = (jnp.sum(jnp.where(keys_ref[0] == tau, jnp.where(page_col < c, 1.0, 0.0), 0.0), axis=0)
                   + jnp.where(keysn_ref[0] == tau, jnp.where(colp + n_pages * PAGE < c, 1.0, 0.0), 0.0))
            return jnp.sum(acc, axis=1, keepdims=True)

        tau, g = _topk_threshold(count_ge, count_tie_below, topk, n_col_bits, (8, 1))
        tg_ref[:, 0:1] = tau
        tg_ref[:, 1:2] = g

    tau = tg_ref[:, 0:1]
    g = tg_ref[:, 1:2]

    def update(s, key8, thr8, v_contract):
        sel = jnp.concatenate([key8] * DSA_HEADS, axis=0) >= jnp.concatenate([thr8] * DSA_HEADS, axis=0)
        m_prev = m_ref[...]
        m_new = jnp.maximum(m_prev, jnp.max(jnp.where(sel, s, NEG_BIG), axis=1, keepdims=True))
        alpha = jnp.exp(m_prev - m_new)
        pr = jnp.where(sel, jnp.exp(s - m_new), 0.0)
        l_ref[...] = alpha * l_ref[...] + jnp.sum(pr, axis=1, keepdims=True)
        acc_ref[...] = alpha * acc_ref[...] + v_contract(pr.astype(BF16))
        m_ref[...] = m_new

    qs = qs_ref[...]
    kcat = jnp.concatenate([r[0, 0].astype(BF16) for r in kT_refs], axis=1)
    vcat = jnp.concatenate([r[0, 0].astype(BF16) for r in vT_refs], axis=1)
    s = _dot(qs, kcat) * (DSA_HD ** -0.5)
    key8 = jnp.concatenate([keys_ref[0, p * nps + i] for i in range(nps)], axis=1)
    col = lax.broadcasted_iota(I32, key8.shape, 1) + p * (nps * PAGE)
    thr8 = tau + jnp.where(col > g, 1, 0)

    update(s, key8, thr8, lambda pr: _dot_nt(pr, vcat))

    @pl.when(p == pl.num_programs(1) - 1)
    def _():
        thrn = tau + jnp.where(colp + n_pages * PAGE > g, 1, 0)
        s = _dot_nt(qs, kn_ref[0].astype(BF16)) * (DSA_HD ** -0.5)
        vnT = vnT_ref[0].astype(BF16)
        update(s, keysn_ref[0], thrn, lambda pr: _dot_nt(pr, vnT))
        for h in range(DSA_HEADS):
            o_ref[0, :, 64 * h:64 * h + 64] = (acc_ref[8 * h:8 * h + 8, 64 * h:64 * h + 64]
                                               / l_ref[8 * h:8 * h + 8, :])


def _dsa_dec(page_table, q8, keys, keysn, ckT, cvT, kn, vnT, l, topk):
    B, NP = page_table.shape
    nps = math.gcd(NP, PAGES_PER_STEP)
    n_keys = NP * PAGE + 128
    gs = pltpu.PrefetchScalarGridSpec(
        num_scalar_prefetch=1, grid=(B, NP // nps),
        in_specs=[pl.BlockSpec((1, 8, 256), lambda b, p, pt: (b, 0, 0)),
                  pl.BlockSpec((1, NP, 8, PAGE), lambda b, p, pt: (b, 0, 0, 0)),
                  pl.BlockSpec((1, 8, 128), lambda b, p, pt: (b, 0, 0))]
                 + _page_specs(l, nps, 256) + _page_specs(l, nps, 256)
                 + [pl.BlockSpec((1, 128, 256), lambda b, p, pt: (b, 0, 0)),
                    pl.BlockSpec((1, 256, 128), lambda b, p, pt: (b, 0, 0))],
        out_specs=pl.BlockSpec((1, 8, 256), lambda b, p, pt: (b, 0, 0)),
        scratch_shapes=[pltpu.VMEM((32, 256), BF16), pltpu.VMEM((8, 128), I32), pltpu.VMEM((32, 1), F32),
                        pltpu.VMEM((32, 1), F32), pltpu.VMEM((32, 256), F32)])
    return pl.pallas_call(
        functools.partial(_dsa_dec_kernel, topk=topk, n_pages=NP, n_col_bits=(n_keys - 1).bit_length(), nps=nps),
        grid_spec=gs, out_shape=jax.ShapeDtypeStruct((B, 8, 256), F32),
        compiler_params=_cparams("arbitrary", "arbitrary"),
        name="dsa_decode",
    )(page_table, q8, keys, keysn, *([ckT] * nps), *([cvT] * nps), kn, vnT)


def _rope_tables(pos, half):
    inv = ROPE_THETA ** (-jnp.arange(half, dtype=F32) / half)
    ang = pos.astype(F32)[:, None] * inv[None, :]
    cos, sin = jnp.cos(ang), jnp.sin(ang)
    reps = 128 // (2 * half)
    return (jnp.tile(jnp.concatenate([cos, cos], axis=1), (1, reps)),
            jnp.tile(jnp.concatenate([-sin, sin], axis=1), (1, reps)))


def _pack_w_in(w_in):
    names = (("lru_x", 256), ("lru_gate", 256), ("diff_q", 256), ("diff_k", 256), ("diff_v", 256),
             ("dsa_q", 256), ("dsa_k", 256), ("dsa_v", 256), ("idx_q", 256), ("idx_k", 32), ("idx_w", 8),
             ("gla_q", 128), ("gla_k", 128), ("gla_v", 256), ("gla_g", 256), ("gla_a", 16))
    seg, off = {}, 0
    for n, w in names:
        seg[n] = w_in[..., off:off + w]
        off += w
    pad = jnp.zeros(w_in.shape[:-1] + (128 - 24,), w_in.dtype)
    cols = [seg["lru_x"], seg["lru_gate"], seg["diff_q"], seg["diff_k"],
            seg["dsa_q"], seg["dsa_k"], seg["idx_q"]] + [seg["idx_k"]] * IDX_HEADS + [
            seg["gla_q"], seg["gla_k"], seg["gla_v"], seg["gla_g"], seg["idx_w"], seg["gla_a"], pad]
    wvT = jnp.concatenate([seg["diff_v"], seg["dsa_v"]], axis=-1).transpose(0, 2, 1)
    return jnp.concatenate(cols, axis=-1).astype(BF16), wvT.astype(BF16)


def _block_diag(w):
    depth, nb, bw, _ = w.shape
    eye = jnp.eye(nb, dtype=w.dtype)
    return jnp.einsum("lnij,nm->lnimj", w, eye).reshape(depth, nb * bw, nb * bw)


def _group_ones(n, group):
    r = jnp.arange(n) // group
    return (r[:, None] == r[None, :]).astype(BF16)


def _prompt_layer(x, mod, l, W, tabs, final, tm=256, tq=256):
    G, T, _ = x.shape
    sh1, sc1, g1, sh2, sc2, g2 = [m[:, None, :] for m in jnp.split(mod, 6, axis=-1)]
    (lru, qb, kb, vbT, qc, kc, vcT, iq, ikt, gqk, gv, gg, misc) = _proj(
        x, sh1, sc1, W["norm1"], W["w_in"], W["wvT"], l, tabs, tm)
    zeros = lambda *s: jnp.zeros(s, F32)
    o_a, h_last, cs8 = _lru_seq(lru, W["lru"], l, zeros(G, 1, 256), zeros(G, 8, 256), tm)
    o_b = _diff_attn(W["lam"][l], qb, kb, vbT, W["subln"], l, W["out_scale"][l], tq)
    o_c = _dsa_attn(qc, kc, vcT, iq, ikt, misc, min(TOPK_MAX, T // 4), tq)
    from_T = lambda a: a.reshape(G, 4, 64, T).transpose(0, 3, 1, 2)
    o_d, s_last = _gla(gqk, gv, gg, misc, W["gla"], l, zeros(G, GLA_HEADS, GLA_DK, GLA_DV),
                       W["ones64"], W["bd_mask"], tm, GLA_CHUNK, None)
    y, fcs8 = _ffn(x, (o_a, o_b, o_c, o_d), (g1, sh2, sc2, g2), W["ffn"], l, zeros(G, 8, D_FF),
                   W["final_g"], tm, True, final)
    new = dict(
        diff_k=kb.reshape(G, T, DIFF_HEADS, 2, DIFF_QK), diff_v=from_T(vbT),
        dsa_k=kc.reshape(G, T, DSA_HEADS, DSA_HD), dsa_v=from_T(vcT),
        idx_k=ikt[:, :, :IDX_DIM], lru_h=h_last[:, 0], lru_conv=cs8[:, 8 - (LRU_CONV - 1):],
        gla=s_last, ffn_conv=fcs8[:, 8 - (FFN_CONV - 1):])
    return y, new


def _pad_rows(a, n):
    return jnp.pad(a, ((0, 0), (0, n - a.shape[1])) + ((0, 0),) * (a.ndim - 2))


def _sample_layer(xr, mod, l, W, tabs, caches, states, page_table, final, B, Tn):
    n = Tn * B
    tm_major = lambda a: jnp.tile(a, (Tn, 1))[None]
    sh1, sc1, g1, sh2, sc2, g2 = [tm_major(m) for m in jnp.split(mod, 6, axis=-1)]
    outs = _proj(xr, sh1, sc1, W["norm1"], W["w_in"], W["wvT"], l, tabs, n)
    (lru, qb, kb, vbT, qc, kc, vcT, iq, ikt, gqk, gv, gg, misc) = [o[0] for o in outs]
    to_b = lambda a: a.reshape(Tn, B, -1).transpose(1, 0, 2)
    to_t = lambda a: a.transpose(1, 0, 2).reshape(n, -1)
    to_bT = lambda a: a.reshape(256, Tn, B).transpose(2, 0, 1)
    pad_lanes = lambda a: jnp.pad(a, ((0, 0), (0, 0), (0, 128 - a.shape[2])))

    conv_prev = states["lru_conv"][l].transpose(1, 0, 2).reshape((LRU_CONV - 1) * B, 256)
    xx = jnp.concatenate([conv_prev, lru[:, :256]], axis=0)
    o_a, h_last = _lru_rows(xx, lru[:, 256:], W["lru"], l, states["lru_h"][l], B, Tn)
    new_conv = xx[Tn * B:].reshape(LRU_CONV - 1, B, 256).transpose(1, 0, 2)

    kb_b, vb_bT = to_b(kb), to_bT(vbT)
    o_b = _diff_dec(page_table, W["lam"][l], _pad_rows(to_b(qb), 8), caches["diff_kT"], caches["diff_vT"],
                    _pad_rows(kb_b, 128), pad_lanes(vb_bT), W["subln"], W["ones64"], l,
                    W["out_scale"][l], Tn)[:, :Tn]

    past_len = page_table.shape[1] * PAGE
    topk = min(TOPK_MAX, (past_len + Tn) // 4)
    kc_b, vc_bT = to_b(kc), to_bT(vcT)
    ik_b = to_b(ikt)[:, :, :IDX_DIM]
    iq64 = _pad_rows(to_b(iq), 8).reshape(B, 8 * IDX_HEADS, IDX_DIM)
    w64 = (_pad_rows(to_b(misc)[:, :, :IDX_HEADS], 8) * (IDX_HEADS ** -0.5)).reshape(B, 8 * IDX_HEADS, 1)
    iknT = _pad_rows(ik_b, 128).transpose(0, 2, 1)
    keys, keysn = _idx_dec(page_table, iq64, w64, caches["idx_kT"], iknT, l, Tn)
    o_c = _dsa_dec(page_table, _pad_rows(to_b(qc), 8), keys, keysn, caches["dsa_kT"], caches["dsa_vT"],
                   _pad_rows(kc_b, 128), pad_lanes(vc_bT), l, topk)[:, :Tn]

    padc = lambda a: _pad_rows(to_b(a), GLA_CHUNK)
    o_d, s_last = _gla(padc(gqk), padc(gv), padc(gg), padc(misc), W["gla"], l, states["gla"][l],
                       W["ones64"], W["bd_mask"], GLA_CHUNK, GLA_CHUNK, Tn)
    o_d = o_d[:, :Tn]

    ffn_prev = states["ffn_conv"][l].transpose(1, 0, 2).reshape(1, (FFN_CONV - 1) * B, D_FF)
    o4 = (o_a[None], to_t(o_b)[None], to_t(o_c)[None], to_t(o_d)[None])
    y, fcs = _ffn(xr, o4, (g1, sh2, sc2, g2), W["ffn"], l, ffn_prev, W["final_g"], n, False, final, nb=B)
    new = dict(
        diff_k=kb_b.reshape(B, Tn, DIFF_HEADS, 2, DIFF_QK),
        diff_v=vb_bT.transpose(0, 2, 1).reshape(B, Tn, DIFF_HEADS, 64),
        dsa_k=kc_b.reshape(B, Tn, DSA_HEADS, DSA_HD),
        dsa_v=vc_bT.transpose(0, 2, 1).reshape(B, Tn, DSA_HEADS, DSA_HD),
        idx_k=ik_b, lru_h=h_last, lru_conv=new_conv, gla=s_last,
        ffn_conv=fcs[0].reshape(FFN_CONV - 1, B, D_FF).transpose(1, 0, 2))
    return y, new


def kernel(x_prompt, x_sample, cache_diff_k, cache_diff_v, cache_dsa_k, cache_dsa_v, cache_idx_k, state_lru_h, state_lru_conv, state_gla, state_ffn_conv, page_table, c_prompt, c_sample, w_ada, b_ada, norm1_g, w_in, lru_conv_w, lru_conv_b, lru_wa, lru_ba, lru_wx, lru_bx, lru_lambda, diff_lq1, diff_lk1, diff_lq2, diff_lk2, diff_subln_g, gla_wa2, gla_ba, gla_norm_g, w_out, norm2_g, ffn_w_gate, ffn_w_up, ffn_conv_w, ffn_conv_b, ffn_w_down, final_norm_g):
    depth = w_in.shape[0]
    Bp, T, D = x_prompt.shape
    Bs, Tn, _ = x_sample.shape
    n_pool = cache_diff_k.shape[1]
    past_len = page_table.shape[1] * PAGE

    lam_init = jnp.asarray([0.8 - 0.6 * math.exp(-0.3 * l) for l in range(depth)], F32)
    lam = (jnp.exp(jnp.sum(diff_lq1 * diff_lk1, axis=-1)) - jnp.exp(jnp.sum(diff_lq2 * diff_lk2, axis=-1))
           + lam_init).reshape(depth, 1, 1)
    row3 = lambda a: a.reshape(depth, 1, a.shape[-1])
    wa2_pad = jnp.zeros((depth, 128, 128), F32).at[:, IDX_HEADS:IDX_HEADS + gla_wa2.shape[1], :].set(gla_wa2)
    w_packed, wvT = _pack_w_in(w_in)
    W = dict(
        norm1=row3(norm1_g), w_in=w_packed, wvT=wvT,
        lru=dict(conv_w=lru_conv_w, conv_b=row3(lru_conv_b), wa=_block_diag(lru_wa).astype(BF16),
                 wx=_block_diag(lru_wx).astype(BF16), ba=row3(lru_ba), bx=row3(lru_bx), lam=row3(lru_lambda)),
        lam=lam, out_scale=[1.0 - (0.8 - 0.6 * math.exp(-0.3 * l)) for l in range(depth)],
        subln=row3(jnp.tile(diff_subln_g, (1, DIFF_HEADS))), ones64=_group_ones(256, 64),
        bd_mask=(jnp.arange(128)[:, None] // GLA_DK == jnp.arange(256)[None, :] // GLA_DV).astype(F32),
        gla=dict(wa=wa2_pad.astype(BF16), ba=row3(gla_ba), gn=row3(jnp.tile(gla_norm_g, (1, GLA_HEADS)))),
        ffn=dict(n2=row3(norm2_g), wout=w_out.astype(BF16), wg=ffn_w_gate.astype(BF16),
                 wu=ffn_w_up.astype(BF16), wd=ffn_w_down.astype(BF16), cw=ffn_conv_w, cb=row3(ffn_conv_b)),
        final_g=final_norm_g.reshape(1, D))

    mods = _ada(jnp.concatenate([c_prompt, c_sample], axis=0), w_ada, b_ada)

    pos_p = jnp.arange(T, dtype=I32)
    pos_s = jnp.repeat(past_len + jnp.arange(Tn, dtype=I32), Bs)
    tabs_p = _rope_tables(pos_p, 16) + _rope_tables(pos_p, 32)
    tabs_s = _rope_tables(pos_s, 16) + _rope_tables(pos_s, 32)

    caches = dict(
        diff_kT=cache_diff_k.transpose(0, 1, 3, 4, 5, 2).reshape(depth, n_pool, 256, PAGE),
        diff_vT=cache_diff_v.transpose(0, 1, 3, 4, 2).reshape(depth, n_pool, 256, PAGE),
        dsa_kT=cache_dsa_k.transpose(0, 1, 3, 4, 2).reshape(depth, n_pool, 256, PAGE),
        dsa_vT=cache_dsa_v.transpose(0, 1, 3, 4, 2).reshape(depth, n_pool, 256, PAGE),
        idx_kT=cache_idx_k.transpose(0, 1, 3, 2))
    states = dict(lru_h=state_lru_h, lru_conv=state_lru_conv, gla=state_gla, ffn_conv=state_ffn_conv)

    xp = x_prompt
    xs = x_sample.transpose(1, 0, 2).reshape(1, Tn * Bs, D)
    st_p, st_s = [], []
    for l in range(depth):
        final = l == depth - 1
        xp, new_p = _prompt_layer(xp, mods[l, :Bp], l, W, tabs_p, final)
        xs, new_s = _sample_layer(xs, mods[l, Bp:], l, W, tabs_s, caches, states, page_table, final, Bs, Tn)
        st_p.append(new_p)
        st_s.append(new_s)
    y_prompt = xp
    y_sample = xs.reshape(Tn, Bs, D).transpose(1, 0, 2)
    keys = ("diff_k", "diff_v", "dsa_k", "dsa_v", "idx_k", "lru_h", "lru_conv", "gla", "ffn_conv")
    P = [jnp.stack([s[k] for s in st_p]) for k in keys]
    S = [jnp.stack([s[k] for s in st_s]) for k in keys]
    return (y_prompt, y_sample, *P, *S)
```

```python
import functools
import math

import jax
import jax.numpy as jnp
from jax import lax
from jax.experimental import pallas as pl
from jax.experimental.pallas import tpu as pltpu

F32 = jnp.float32
BF16 = jnp.bfloat16
I32 = jnp.int32

D_MODEL = 1024
GROUP_W = 256
LRU_CONV = 4
LRU_C = 8.0
DIFF_HEADS = 4
DIFF_QK = 32
DSA_HEADS = 4
DSA_HD = 64
IDX_HEADS = 8
IDX_DIM = 32
TOPK_MAX = 256
GLA_HEADS = 4
GLA_DK = 32
GLA_DV = 64
GLA_TAU = 16.0
GLA_CHUNK = 64
D_FF = 2816
FFN_CONV = 3
ROPE_THETA = 10000.0
EPS = 1e-6
PAGE = 128

P_LRU, P_DIFF_QK, P_DSA_QK, P_IDX, P_GLA_QK, P_GLA_V, P_GLA_G, P_MISC, P_END = (
    0, 512, 1024, 1536, 2048, 2304, 2560, 2816, 2944)
PAGES_PER_STEP = 32
HEADS_PER_STAGE = 8

VMEM_LIMIT_BYTES = 56 * 1024 * 1024
NEG_BIG = -1e30
LOG2E = 1.4426950408889634
KEY_NEG_INF = -2 ** 31 + 0x7FFFFF
BIG_COL = 2 ** 30


def _cparams(*sem):
    return pltpu.CompilerParams(dimension_semantics=sem, vmem_limit_bytes=VMEM_LIMIT_BYTES)


def _rms(x, g):
    return x * lax.rsqrt(jnp.mean(x * x, axis=-1, keepdims=True) + EPS) * g


def _silu(x):
    return x * jax.nn.sigmoid(x)


def _softplus(z):
    return jnp.maximum(z, 0.0) + jnp.log1p(jnp.exp(-jnp.abs(z)))


def _dot(a, b):
    return jnp.dot(a, b, preferred_element_type=F32)


def _dot_nt(a, b):
    return lax.dot_general(a, b, (((1,), (1,)), ((), ())), preferred_element_type=F32)


def _group_mean64(x2, ones_bd):
    hi = x2.astype(BF16)
    lo = (x2 - hi.astype(F32)).astype(BF16)
    return (_dot(hi, ones_bd) + _dot(lo, ones_bd)) * (1.0 / 64.0)


def _rope(x, cos, sin_signed, half):
    n = x.shape[1]
    reps = n // 128
    c = jnp.concatenate([cos] * reps, axis=1) if reps > 1 else cos
    s = jnp.concatenate([sin_signed] * reps, axis=1) if reps > 1 else sin_signed
    lane = lax.broadcasted_iota(I32, x.shape, 1)
    first = (lane % (2 * half)) < half
    swapped = jnp.where(first, pltpu.roll(x, n - half, 1), pltpu.roll(x, half, 1))
    return x * c + swapped * s


def _shift_rows(x, s, prev8):
    r = pltpu.roll(x, s, 0)
    p = pltpu.roll(prev8, s, 0)
    row = lax.broadcasted_iota(I32, (8, x.shape[1]), 0)
    top = jnp.where(row < s, p, r[:8])
    return jnp.concatenate([top, r[8:]], axis=0)


def _cumsum_rows(x):
    n = x.shape[0]
    row = lax.broadcasted_iota(I32, x.shape, 0)
    s = 1
    while s < n:
        x = x + jnp.where(row >= s, pltpu.roll(x, s, 0), 0.0)
        s *= 2
    return x


def _key_to_float(key):
    return pltpu.bitcast(key ^ ((key >> 31) & 0x7FFFFFFF), F32)


def _ada_kernel(c_ref, w_ref, b_ref, o_ref):
    c = c_ref[...]
    o_ref[...] = _dot(_silu(c).astype(BF16), w_ref[...].astype(BF16)) + b_ref[...]


def _ada(c_all, w_ada, b_ada):
    depth = w_ada.shape[0]
    n = c_all.shape[0]
    tn = 1024
    return pl.pallas_call(
        _ada_kernel,
        grid=(depth, 6 * D_MODEL // tn),
        in_specs=[pl.BlockSpec((n, D_MODEL), lambda l, j: (0, 0)),
                  pl.BlockSpec((None, D_MODEL, tn), lambda l, j: (l, 0, j)),
                  pl.BlockSpec((None, 1, tn), lambda l, j: (l, 0, j))],
        out_specs=pl.BlockSpec((None, n, tn), lambda l, j: (l, 0, j)),
        out_shape=jax.ShapeDtypeStruct((depth, n, 6 * D_MODEL), F32),
        compiler_params=_cparams("arbitrary", "arbitrary"),
        name="ada_mod",
    )(c_all, w_ada, b_ada.reshape(depth, 1, 6 * D_MODEL))


def _proj_kernel(x_ref, sh_ref, sc_ref, g_ref, w_ref, wvT_ref, c32_ref, s32_ref, c64_ref, s64_ref,
                 lru_ref, qb_ref, kb_ref, vbT_ref, qc_ref, kc_ref, vcT_ref, iq_ref, ikt_ref,
                 gqk_ref, gv_ref, gg_ref, misc_ref):
    x = x_ref[0]
    h = _rms(x, g_ref[...]) * (1.0 + sc_ref[0]) + sh_ref[0]
    hb = h.astype(BF16)

    def mm(a, b):
        return _dot(hb, w_ref[:, a:b])

    lru_ref[0] = mm(P_LRU, P_DIFF_QK)
    c32, s32, c64, s64 = c32_ref[...], s32_ref[...], c64_ref[...], s64_ref[...]
    qk = _rope(mm(P_DIFF_QK, P_DSA_QK), c32, s32, 16)
    qb_ref[0] = qk[:, :256]
    kb_ref[0] = qk[:, 256:]
    qk = _rope(mm(P_DSA_QK, P_IDX), c64, s64, 32)
    qc_ref[0] = qk[:, :256]
    kc_ref[0] = qk[:, 256:]
    qk = _rope(mm(P_IDX, P_GLA_QK), c32, s32, 16)
    iq_ref[0] = qk[:, :256]
    ikt_ref[0] = qk[:, 256:]
    gqk_ref[0] = mm(P_GLA_QK, P_GLA_V)
    gv_ref[0] = mm(P_GLA_V, P_GLA_G)
    gg_ref[0] = mm(P_GLA_G, P_MISC)
    misc_ref[0] = mm(P_MISC, P_END)
    vbT_ref[0] = _dot_nt(wvT_ref[0:256, :], hb)
    vcT_ref[0] = _dot_nt(wvT_ref[256:512, :], hb)


_PROJ_WIDTHS = (512, 256, 256, None, 256, 256, None, 256, 256, 256, 256, 256, 128)


def _proj(x, sh, sc, norm_g, w_packed, wvT, l, tabs, tm):
    G, T, D = x.shape
    per_row = sh.shape[1] != 1
    mod_spec = (pl.BlockSpec((1, tm, D), lambda b, i: (b, i, 0)) if per_row
                else pl.BlockSpec((1, 1, D), lambda b, i: (b, 0, 0)))
    tab_spec = pl.BlockSpec((tm, 128), lambda b, i: (i, 0))
    out_specs = [pl.BlockSpec((1, 256, tm), lambda b, i: (b, 0, i)) if w is None
                 else pl.BlockSpec((1, tm, w), lambda b, i: (b, i, 0)) for w in _PROJ_WIDTHS]
    out_shape = [jax.ShapeDtypeStruct((G, 256, T) if w is None else (G, T, w), F32) for w in _PROJ_WIDTHS]
    return pl.pallas_call(
        _proj_kernel,
        grid=(G, T // tm),
        in_specs=[pl.BlockSpec((1, tm, D), lambda b, i: (b, i, 0)), mod_spec, mod_spec,
                  pl.BlockSpec((None, 1, D), lambda b, i: (l, 0, 0)),
                  pl.BlockSpec((None, D, P_END), lambda b, i: (l, 0, 0)),
                  pl.BlockSpec((None, 512, D), lambda b, i: (l, 0, 0)),
                  tab_spec, tab_spec, tab_spec, tab_spec],
        out_specs=out_specs, out_shape=out_shape,
        compiler_params=_cparams("arbitrary", "arbitrary"),
        name="in_proj",
    )(x, sh, sc, norm_g, w_packed, wvT, *tabs)


def _lru_gates(xa, gate_in, wa_ref, wx_ref, ba_ref, bx_ref, lam_ref):
    xb = xa.astype(BF16)
    ga = jax.nn.sigmoid(_dot(xb, wa_ref[...]) + ba_ref[...])
    gx = jax.nn.sigmoid(_dot(xb, wx_ref[...]) + bx_ref[...])
    log_a = (-LRU_C) * ga * _softplus(-lam_ref[...])
    a = jnp.exp(log_a)
    t = jnp.tanh(log_a)
    mult = jnp.sqrt(-2.0 * t / (1.0 - t))
    return a, mult, gx


def _lru_seq_kernel(lru_ref, cw_ref, cb_ref, wa_ref, wx_ref, ba_ref, bx_ref, lam_ref, h0_ref, buf_ref,
                    o_ref, hl_ref, cs_ref, hc_ref, cbuf_ref, *, tm):
    i = pl.program_id(1)

    @pl.when(i == 0)
    def _():
        hc_ref[...] = jnp.broadcast_to(h0_ref[0], hc_ref.shape)
        cbuf_ref[...] = buf_ref[0]

    xg = lru_ref[0]
    x = xg[:, :256]
    gate = xg[:, 256:]
    prev8 = cbuf_ref[...]
    w = cw_ref[...]
    xa = (cb_ref[...] + x * w[3:4] + _shift_rows(x, 1, prev8) * w[2:3]
          + _shift_rows(x, 2, prev8) * w[1:2] + _shift_rows(x, 3, prev8) * w[0:1])
    cbuf_ref[...] = x[tm - 8:]
    cs_ref[0] = x[tm - 8:]

    a, mult, gx = _lru_gates(xa, gate, wa_ref, wx_ref, ba_ref, bx_ref, lam_ref)
    row = lax.broadcasted_iota(I32, (tm, 256), 0)
    mult = jnp.where((row + i * tm) == 0, 1.0, mult)
    u = mult * gx * xa
    s = 1
    while s < tm:
        keep = row >= s
        a_sh = jnp.where(keep, pltpu.roll(a, s, 0), 1.0)
        u_sh = jnp.where(keep, pltpu.roll(u, s, 0), 0.0)
        u = a * u_sh + u
        a = a * a_sh
        s *= 2
    h = a * hc_ref[0:1, :] + u
    hlast = h[tm - 1:tm]
    hc_ref[...] = jnp.broadcast_to(hlast, hc_ref.shape)
    hl_ref[0] = hlast
    o_ref[0] = h * jax.nn.gelu(gate, approximate=True)


def _lru_seq(lru, lw, l, h0, buf8, tm):
    G, T, _ = lru.shape
    wspec = lambda shape: pl.BlockSpec((None,) + shape, lambda b, i: (l,) + (0,) * len(shape))
    return pl.pallas_call(
        functools.partial(_lru_seq_kernel, tm=tm),
        grid=(G, T // tm),
        in_specs=[pl.BlockSpec((1, tm, 512), lambda b, i: (b, i, 0)),
                  wspec((LRU_CONV, 256)), wspec((1, 256)), wspec((256, 256)), wspec((256, 256)),
                  wspec((1, 256)), wspec((1, 256)), wspec((1, 256)),
                  pl.BlockSpec((1, 1, 256), lambda b, i: (b, 0, 0)),
                  pl.BlockSpec((1, 8, 256), lambda b, i: (b, 0, 0))],
        out_specs=[pl.BlockSpec((1, tm, 256), lambda b, i: (b, i, 0)),
                   pl.BlockSpec((1, 1, 256), lambda b, i: (b, 0, 0)),
                   pl.BlockSpec((1, 8, 256), lambda b, i: (b, 0, 0))],
        out_shape=[jax.ShapeDtypeStruct((G, T, 256), F32), jax.ShapeDtypeStruct((G, 1, 256), F32),
                   jax.ShapeDtypeStruct((G, 8, 256), F32)],
        scratch_shapes=[pltpu.VMEM((8, 256), F32), pltpu.VMEM((8, 256), F32)],
        compiler_params=_cparams("arbitrary", "arbitrary"),
        name="lru_seq",
    )(lru, lw["conv_w"], lw["conv_b"], lw["wa"], lw["wx"], lw["ba"], lw["bx"], lw["lam"], h0, buf8)


def _lru_rows_kernel(xx_ref, gate_ref, cw_ref, cb_ref, wa_ref, wx_ref, ba_ref, bx_ref, lam_ref, h0_ref,
                     o_ref, hl_ref, *, nb, nt):
    xx = xx_ref[...]
    w = cw_ref[...]
    n = nt * nb
    xa = cb_ref[...]
    for j in range(LRU_CONV):
        xa = xa + xx[j * nb:j * nb + n] * w[j:j + 1]
    gate = gate_ref[...]
    a, mult, gx = _lru_gates(xa, gate, wa_ref, wx_ref, ba_ref, bx_ref, lam_ref)
    u = mult * gx * xa
    h = h0_ref[...]
    for t in range(nt):
        h = a[t * nb:(t + 1) * nb] * h + u[t * nb:(t + 1) * nb]
        o_ref[t * nb:(t + 1) * nb, :] = h * jax.nn.gelu(gate[t * nb:(t + 1) * nb], approximate=True)
    hl_ref[...] = h


def _lru_rows(xx, gate, lw, l, h0, nb, nt):
    n = nt * nb
    full = lambda a: pl.BlockSpec(a.shape, lambda i: (0,) * a.ndim)
    wspec = lambda shape: pl.BlockSpec((None,) + shape, lambda i: (l,) + (0,) * len(shape))
    return pl.pallas_call(
        functools.partial(_lru_rows_kernel, nb=nb, nt=nt),
        grid=(1,),
        in_specs=[full(xx), full(gate), wspec((LRU_CONV, 256)), wspec((1, 256)), wspec((256, 256)),
                  wspec((256, 256)), wspec((1, 256)), wspec((1, 256)), wspec((1, 256)), full(h0)],
        out_specs=[pl.BlockSpec((n, 256), lambda i: (0, 0)), pl.BlockSpec((nb, 256), lambda i: (0, 0))],
        out_shape=[jax.ShapeDtypeStruct((n, 256), F32), jax.ShapeDtypeStruct((nb, 256), F32)],
        compiler_params=_cparams("arbitrary"),
        name="lru_rows",
    )(xx, gate, lw["conv_w"], lw["conv_b"], lw["wa"], lw["wx"], lw["ba"], lw["bx"], lw["lam"], h0)


def _diff_finalize(acc_ref, l_ref, lam, osc_ref, g_ref, ones_ref, out_scale, rows, wide):
    for h in range(DIFF_HEADS):
        r1 = slice((2 * h) * rows, (2 * h + 1) * rows)
        r2 = slice((2 * h + 1) * rows, (2 * h + 2) * rows)
        cs = slice(64 * h, 64 * h + 64) if wide else slice(0, 64)
        o1 = acc_ref[r1, cs] / l_ref[r1, :]
        o2 = acc_ref[r2, cs] / l_ref[r2, :]
        osc_ref[:, 64 * h:64 * h + 64] = o1 - lam * o2
    o = osc_ref[...]
    ms = _group_mean64(o * o, ones_ref[...])
    return o * lax.rsqrt(ms + EPS) * g_ref[...] * out_scale


def _softmax_steps(ss, sel, m_ref, l_ref, acc_ref, idxs, vTs, c):
    n = range(len(ss))
    m_prev = [m_ref[i:i + 1, :] for i in idxs]
    l_prev = [l_ref[i:i + 1, :] for i in idxs]
    if sel is not None:
        ss = [jnp.where(sel, s, NEG_BIG) for s in ss]
    m_new = [jnp.maximum(m_prev[a], jnp.max(ss[a], axis=0, keepdims=True)) for a in n]
    alpha = [jnp.exp2((m_prev[a] - m_new[a]) * c) for a in n]
    p = [jnp.exp2((ss[a] - m_new[a]) * c) for a in n]
    l_new = [alpha[a] * l_prev[a] + jnp.sum(p[a], axis=0, keepdims=True) for a in n]
    pv = [_dot(vTs[a], p[a].astype(BF16)) for a in n]
    acc_new = [alpha[a] * acc_ref[idxs[a]] + pv[a] for a in n]
    for a in n:
        i = idxs[a]
        m_ref[i:i + 1, :] = m_new[a]
        l_ref[i:i + 1, :] = l_new[a]
        acc_ref[i] = acc_new[a]


def _diff_attn_kernel(lam_ref, q_ref, k_ref, vT_ref, g_ref, o_ref,
                      qmT_ref, m_ref, l_ref, acc_ref, osc_ref, *, tq, out_scale):
    i = pl.program_id(1)
    qT = q_ref[0].T
    feat = lax.broadcasted_iota(I32, (256, tq), 0)
    for hm in range(8):
        qmT_ref[hm] = jnp.where((feat // DIFF_QK) == hm, qT, 0.0).astype(BF16)
    m_ref[...] = jnp.full(m_ref.shape, NEG_BIG, F32)
    l_ref[...] = jnp.zeros(l_ref.shape, F32)
    acc_ref[...] = jnp.zeros(acc_ref.shape, F32)
    krow = lax.broadcasted_iota(I32, (tq, tq), 0)
    qcol = lax.broadcasted_iota(I32, (tq, tq), 1)
    c = (DIFF_QK ** -0.5) * LOG2E

    def block(j, masked):
        r0 = pl.multiple_of(j * tq, tq)
        kj = k_ref[0, pl.ds(r0, tq), :].astype(BF16)
        vjT = vT_ref[0, :, pl.ds(r0, tq)].astype(BF16)
        for h0 in range(0, 8, HEADS_PER_STAGE):
            hms = list(range(h0, h0 + HEADS_PER_STAGE))
            ss = [_dot(kj, qmT_ref[hm]) for hm in hms]
            if masked:
                ss = [jnp.where(krow <= qcol, s, NEG_BIG) for s in ss]
            vTs = [vjT[64 * (hm // 2):64 * (hm // 2) + 64, :] for hm in hms]
            _softmax_steps(ss, None, m_ref, l_ref, acc_ref, hms, vTs, c)

    def loop_body(j, carry):
        block(j, False)
        return carry

    lax.fori_loop(0, i, loop_body, 0)
    block(i, True)
    lam = lam_ref[0, 0]
    for h in range(DIFF_HEADS):
        o = (acc_ref[2 * h] / l_ref[2 * h:2 * h + 1, :]
             - lam * (acc_ref[2 * h + 1] / l_ref[2 * h + 1:2 * h + 2, :]))
        ms = jnp.mean(o * o, axis=0, keepdims=True)
        osc_ref[64 * h:64 * h + 64, :] = o * lax.rsqrt(ms + EPS)
    o_ref[0] = osc_ref[...].T * g_ref[...] * out_scale


def _diff_attn(lam, q, k, vT, g_tiled, l, out_scale, tq):
    G, T, _ = q.shape
    return pl.pallas_call(
        functools.partial(_diff_attn_kernel, tq=tq, out_scale=out_scale),
        grid=(G, T // tq),
        in_specs=[pl.BlockSpec(memory_space=pltpu.SMEM),
                  pl.BlockSpec((1, tq, 256), lambda b, i: (b, i, 0)),
                  pl.BlockSpec((1, T, 256), lambda b, i: (b, 0, 0)),
                  pl.BlockSpec((1, 256, T), lambda b, i: (b, 0, 0)),
                  pl.BlockSpec((None, 1, 256), lambda b, i: (l, 0, 0))],
        out_specs=pl.BlockSpec((1, tq, 256), lambda b, i: (b, i, 0)),
        out_shape=jax.ShapeDtypeStruct((G, T, 256), F32),
        scratch_shapes=[pltpu.VMEM((8, 256, tq), BF16), pltpu.VMEM((8, tq), F32),
                        pltpu.VMEM((8, tq), F32), pltpu.VMEM((8, 64, tq), F32),
                        pltpu.VMEM((256, tq), F32)],
        compiler_params=_cparams("arbitrary", "arbitrary"),
        name="diff_attn",
    )(lam, q, k, vT, g_tiled)


def _topk_threshold(count_ge, count_tie_below, topk, n_col_bits, shape):
    kf = float(topk)
    zero = jnp.zeros(shape, I32)
    c0 = count_ge(jnp.zeros(shape, F32))
    tau = jnp.where(c0 >= kf, zero, zero + KEY_NEG_INF)
    cnt = c0

    def bit_body(b, carry):
        tau, cnt = carry
        cand = tau + lax.shift_left(jnp.int32(1), 30 - b)
        c = count_ge(_key_to_float(cand))
        up = c >= kf
        return jnp.where(up, cand, tau), jnp.where(up, c, cnt)

    tau, cnt = lax.fori_loop(0, 31, bit_body, (tau, cnt))
    t0 = _key_to_float(tau)
    t1 = _key_to_float(tau + 1)
    has_tie = jnp.where(tau > KEY_NEG_INF, jnp.where(cnt > kf, 1.0, 0.0), 0.0)

    def tie_search():
        need = kf - count_ge(t1)

        def tie_body(b, g):
            cand = g + lax.shift_left(jnp.int32(1), n_col_bits - 1 - b)
            return jnp.where(count_tie_below(t0, cand) < need, cand, g)
        return lax.fori_loop(0, n_col_bits, tie_body, zero)

    g_tie = lax.cond(jnp.max(has_tie) > 0.0, tie_search, lambda: zero)
    g = jnp.where(tau == KEY_NEG_INF, -1, jnp.where(has_tie > 0.0, g_tie, BIG_COL))
    return t0, t1, g


def _dsa_kernel(q_ref, k_ref, vT_ref, iq_ref, ikt_ref, misc_ref, o_ref,
                qmT_ref, iqmT_ref, sc_ref, m_ref, l_ref, acc_ref, osc_ref, *, tq, topk, n_col_bits):
    i = pl.program_id(1)
    nblk = i + 1
    feat = lax.broadcasted_iota(I32, (256, tq), 0)
    iqT = iq_ref[0].T
    for h in range(IDX_HEADS):
        iqmT_ref[h] = jnp.where((feat // IDX_DIM) == h, iqT, 0.0).astype(BF16)
    qT = q_ref[0].T
    for h in range(DSA_HEADS):
        qmT_ref[h] = jnp.where((feat // DSA_HD) == h, qT, 0.0).astype(BF16)
    wT = misc_ref[0].T[0:IDX_HEADS, :] * ((IDX_HEADS ** -0.5) * (IDX_DIM ** -0.5))
    krow = lax.broadcasted_iota(I32, (tq, tq), 0)
    qcol = lax.broadcasted_iota(I32, (tq, tq), 1)

    def score_block(j, masked):
        r0 = pl.multiple_of(j * tq, tq)
        ikj = ikt_ref[0, pl.ds(r0, tq), :].astype(BF16)
        scs = [_dot(ikj, iqmT_ref[h]) for h in range(IDX_HEADS)]
        tot = wT[0:1, :] * jnp.maximum(scs[0], 0.0)
        for h in range(1, IDX_HEADS):
            tot = tot + wT[h:h + 1, :] * jnp.maximum(scs[h], 0.0)
        tot = jnp.where(tot == 0.0, 0.0, tot)
        if masked:
            tot = jnp.where(krow <= qcol, tot, -jnp.inf)
        sc_ref[j] = tot

    def score_body(j, carry):
        score_block(j, False)
        return carry

    lax.fori_loop(0, i, score_body, 0)
    score_block(i, True)

    @pl.when(nblk % 2 == 1)
    def _():
        sc_ref[nblk] = jnp.full((tq, tq), -jnp.inf, F32)

    def count_rows(hit_fn):
        def body(jj, acc):
            a = jnp.sum(hit_fn(2 * jj).reshape(tq // 8, 8, tq), axis=0)
            b = jnp.sum(hit_fn(2 * jj + 1).reshape(tq // 8, 8, tq), axis=0)
            return acc + (a + b)
        acc = lax.fori_loop(0, (nblk + 1) // 2, body, jnp.zeros((8, tq), F32))
        return jnp.sum(acc, axis=0, keepdims=True)

    def count_ge(c):
        return count_rows(lambda j: jnp.where(sc_ref[j] >= c, 1.0, 0.0))

    def count_tie_below(t, c):
        return count_rows(lambda j: jnp.where(sc_ref[j] == t, jnp.where(krow + j * tq < c, 1.0, 0.0), 0.0))

    t0, t1, g = _topk_threshold(count_ge, count_tie_below, topk, n_col_bits, (1, tq))

    m_ref[...] = jnp.full(m_ref.shape, NEG_BIG, F32)
    l_ref[...] = jnp.zeros(l_ref.shape, F32)
    acc_ref[...] = jnp.zeros(acc_ref.shape, F32)
    c = (DSA_HD ** -0.5) * LOG2E

    def att_body(j, carry):
        r0 = pl.multiple_of(j * tq, tq)
        sel = sc_ref[j] >= jnp.where(krow + j * tq > g, t1, t0)
        kj = k_ref[0, pl.ds(r0, tq), :].astype(BF16)
        vjT = vT_ref[0, :, pl.ds(r0, tq)].astype(BF16)
        hs = list(range(DSA_HEADS))
        ss = [_dot(kj, qmT_ref[h]) for h in hs]
        _softmax_steps(ss, sel, m_ref, l_ref, acc_ref, hs, [vjT[64 * h:64 * h + 64, :] for h in hs], c)
        return carry

    lax.fori_loop(0, nblk, att_body, 0)
    for h in range(DSA_HEADS):
        osc_ref[64 * h:64 * h + 64, :] = acc_ref[h] / l_ref[h:h + 1, :]
    o_ref[0] = osc_ref[...].T


def _dsa_attn(q, k, vT, iq, ikt, misc, topk, tq):
    G, T, _ = q.shape
    nb = T // tq
    return pl.pallas_call(
        functools.partial(_dsa_kernel, tq=tq, topk=topk, n_col_bits=max(1, (T - 1).bit_length())),
        grid=(G, nb),
        in_specs=[pl.BlockSpec((1, tq, 256), lambda b, i: (b, i, 0)),
                  pl.BlockSpec((1, T, 256), lambda b, i: (b, 0, 0)),
                  pl.BlockSpec((1, 256, T), lambda b, i: (b, 0, 0)),
                  pl.BlockSpec((1, tq, 256), lambda b, i: (b, i, 0)),
                  pl.BlockSpec((1, T, 256), lambda b, i: (b, 0, 0)),
                  pl.BlockSpec((1, tq, 128), lambda b, i: (b, i, 0))],
        out_specs=pl.BlockSpec((1, tq, 256), lambda b, i: (b, i, 0)),
        out_shape=jax.ShapeDtypeStruct((G, T, 256), F32),
        scratch_shapes=[pltpu.VMEM((DSA_HEADS, 256, tq), BF16), pltpu.VMEM((IDX_HEADS, 256, tq), BF16),
                        pltpu.VMEM((nb + 1, tq, tq), F32), pltpu.VMEM((8, tq), F32),
                        pltpu.VMEM((8, tq), F32), pltpu.VMEM((DSA_HEADS, 64, tq), F32),
                        pltpu.VMEM((256, tq), F32)],
        compiler_params=_cparams("arbitrary", "arbitrary"),
        name="dsa_attn",
    )(q, k, vT, iq, ikt, misc)


def _gla_kernel(qk_ref, v_ref, g_ref, misc_ref, wa_ref, ba_ref, gn_ref, ones_ref, bd_ref, s0_ref,
                gsel_ref, negq_ref, negk_ref, pair_ref, o_ref, sfin_ref, s_ref, *, tm, chunk, n_valid):
    i = pl.program_id(1)

    @pl.when(i == 0)
    def _():
        s_ref[...] = jnp.zeros(s_ref.shape, F32)
        for h in range(GLA_HEADS):
            s_ref[32 * h:32 * h + 32, 64 * h:64 * h + 64] = s0_ref[0, h]

    crow = lax.broadcasted_iota(I32, (chunk, 128), 0)
    lane_h = lax.broadcasted_iota(I32, (chunk, 128), 1) // GLA_DK
    vlane_h = lax.broadcasted_iota(I32, (chunk, 256), 1) // GLA_DV
    n_levels = chunk.bit_length() - 1

    cs = range(tm // chunk)
    rows = [slice(c * chunk, (c + 1) * chunk) for c in cs]
    bd = bd_ref[...]
    bdb = bd.astype(BF16)
    a_lin = [_dot(misc_ref[0, r, :].astype(BF16), wa_ref[...]) + ba_ref[...] for r in rows]
    la = [(jnp.minimum(a, 0.0) - jnp.log1p(jnp.exp(-jnp.abs(a)))) * (1.0 / GLA_TAU) for a in a_lin]
    q = [qk_ref[0, r, 0:128] * (GLA_DK ** -0.5) for r in rows]
    k = [qk_ref[0, r, 128:256] for r in rows]
    if n_valid is not None:
        live = [(crow + c * chunk) < n_valid for c in cs]
        la = [jnp.where(live[c], la[c], 0.0) for c in cs]
        k = [jnp.where(live[c], k[c], 0.0) for c in cs]
    v = [v_ref[0, r, :] for r in rows]
    vb = [x.astype(BF16) for x in v]
    b = [_cumsum_rows(x) for x in la]
    b_hi = [x.astype(BF16) for x in b]
    r1 = [b[c] - b_hi[c].astype(F32) for c in cs]
    b_mid = [x.astype(BF16) for x in r1]
    b_lo = [(r1[c] - b_mid[c].astype(F32)).astype(BF16) for c in cs]
    gsel = gsel_ref[...]
    ref_rows = [_dot(gsel, b_hi[c]) + _dot(gsel, b_mid[c]) + _dot(gsel, b_lo[c]) for c in cs]
    att = [jnp.zeros((GLA_HEADS * chunk, chunk), F32) for _ in cs]
    for j in range(n_levels):
        rj = [ref_rows[c][j * chunk:(j + 1) * chunk] for c in cs]
        qt = [q[c] * jnp.exp((b[c] - rj[c]) + negq_ref[j]) for c in cs]
        kh = [k[c] * jnp.exp((rj[c] - b[c]) + negk_ref[j]) for c in cs]
        qs = [jnp.concatenate([jnp.where(lane_h == h, qt[c], 0.0) for h in range(GLA_HEADS)], axis=0) for c in cs]
        att = [att[c] + _dot_nt(qs[c].astype(BF16), kh[c].astype(BF16)) * pair_ref[j] for c in cs]
    x = [_dot(att[c].astype(BF16), vb[c]) for c in cs]
    o = [_dot((q[c] * k[c]).astype(BF16), bdb) * v[c] for c in cs]
    for h in range(GLA_HEADS):
        o = [o[c] + jnp.where(vlane_h == h, x[c][h * chunk:(h + 1) * chunk], 0.0) for c in cs]
    bT = [x.T for x in b]
    kT = [x.T for x in k]
    bendT = [x[:, chunk - 1:chunk] for x in bT]
    upd = [_dot((kT[c] * jnp.exp(bendT[c] - bT[c])).astype(BF16), vb[c]) * bd for c in cs]
    decay = [jnp.exp(x) for x in bendT]
    qe = [(q[c] * jnp.exp(b[c])).astype(BF16) for c in cs]
    S = s_ref[...]
    for c in cs:
        o[c] = o[c] + _dot(qe[c], S.astype(BF16))
        S = decay[c] * S + upd[c]
    s_ref[...] = S
    ms = [_group_mean64(x * x, ones_ref[...]) for x in o]
    for c in cs:
        o_ref[0, rows[c], :] = o[c] * lax.rsqrt(ms[c] + EPS) * gn_ref[...] * _silu(g_ref[0, rows[c], :])

    @pl.when(i == pl.num_programs(1) - 1)
    def _():
        for h in range(GLA_HEADS):
            sfin_ref[0, h] = s_ref[32 * h:32 * h + 32, 64 * h:64 * h + 64]


def _gla_tables(chunk):
    t = jnp.arange(chunk)
    gsel, negq, negk, pair = [], [], [], []
    m = chunk // 2
    while m >= 1:
        blk, second = t // (2 * m), (t % (2 * m)) >= m
        gsel.append((t[None, :] == (blk * 2 * m + m - 1)[:, None]).astype(BF16))
        negq.append(jnp.broadcast_to(jnp.where(second, 0.0, -jnp.inf)[:, None], (chunk, 128)))
        negk.append(jnp.broadcast_to(jnp.where(second, -jnp.inf, 0.0)[:, None], (chunk, 128)))
        pair.append(jnp.tile((blk[:, None] == blk[None, :]).astype(F32), (GLA_HEADS, 1)))
        m //= 2
    return (jnp.concatenate(gsel, axis=0), jnp.stack(negq).astype(F32), jnp.stack(negk).astype(F32),
            jnp.stack(pair))


def _gla(gqk, gv, gg, misc, gw, l, s0, ones_bd, bd_mask, tm, chunk, n_valid):
    G, T, _ = gqk.shape
    wspec = lambda shape: pl.BlockSpec((None,) + shape, lambda b, i: (l,) + (0,) * len(shape))
    tables = _gla_tables(chunk)
    const = lambda a: pl.BlockSpec(a.shape, lambda b, i: (0,) * a.ndim)
    return pl.pallas_call(
        functools.partial(_gla_kernel, tm=tm, chunk=chunk, n_valid=n_valid),
        grid=(G, T // tm),
        in_specs=[pl.BlockSpec((1, tm, 256), lambda b, i: (b, i, 0)),
                  pl.BlockSpec((1, tm, 256), lambda b, i: (b, i, 0)),
                  pl.BlockSpec((1, tm, 256), lambda b, i: (b, i, 0)),
                  pl.BlockSpec((1, tm, 128), lambda b, i: (b, i, 0)),
                  wspec((128, 128)), wspec((1, 128)), wspec((1, 256)),
                  pl.BlockSpec((256, 256), lambda b, i: (0, 0)),
                  pl.BlockSpec((128, 256), lambda b, i: (0, 0)),
                  pl.BlockSpec((1, GLA_HEADS, GLA_DK, GLA_DV), lambda b, i: (b, 0, 0, 0))]
                 + [const(a) for a in tables],
        out_specs=[pl.BlockSpec((1, tm, 256), lambda b, i: (b, i, 0)),
                   pl.BlockSpec((1, GLA_HEADS, GLA_DK, GLA_DV), lambda b, i: (b, 0, 0, 0))],
        out_shape=[jax.ShapeDtypeStruct((G, T, 256), F32),
                   jax.ShapeDtypeStruct((G, GLA_HEADS, GLA_DK, GLA_DV), F32)],
        scratch_shapes=[pltpu.VMEM((128, 256), F32)],
        compiler_params=_cparams("arbitrary", "arbitrary"),
        name="gla",
    )(gqk, gv, gg, misc, gw["wa"], gw["ba"], gw["gn"], ones_bd, bd_mask, s0, *tables)


def _ffn_kernel(*refs, tm, seq, final, nb):
    (x_ref, oa_ref, ob_ref, oc_ref, od_ref, g1_ref, sh2_ref, sc2_ref, g2_ref, n2_ref,
     wout_ref, wg_ref, wu_ref, wd_ref, cw_ref, cb_ref, buf_ref, fng_ref, y_ref, cs_ref) = refs[:20]
    x = x_ref[0]
    mix = _dot(oa_ref[0].astype(BF16), wout_ref[0:256, :])
    mix = mix + _dot(ob_ref[0].astype(BF16), wout_ref[256:512, :])
    mix = mix + _dot(oc_ref[0].astype(BF16), wout_ref[512:768, :])
    mix = mix + _dot(od_ref[0].astype(BF16), wout_ref[768:1024, :])
    x1 = x + g1_ref[0] * mix
    h2 = _rms(x1, n2_ref[...]) * (1.0 + sc2_ref[0]) + sh2_ref[0]
    hb = h2.astype(BF16)
    gate = _dot(hb, wg_ref[...])
    w = cw_ref[...]
    if seq:
        cbuf_ref = refs[20]

        @pl.when(pl.program_id(1) == 0)
        def _():
            cbuf_ref[...] = buf_ref[0]

        prev8 = cbuf_ref[...]
        z = cb_ref[...] + gate * w[2:3] + _shift_rows(gate, 1, prev8) * w[1:2] + _shift_rows(gate, 2, prev8) * w[0:1]
        cbuf_ref[...] = gate[tm - 8:]
        cs_ref[0] = gate[tm - 8:]
    else:
        xx = jnp.concatenate([buf_ref[0], gate], axis=0)
        z = cb_ref[...] + xx[0:tm] * w[0:1] + xx[nb:nb + tm] * w[1:2] + xx[2 * nb:2 * nb + tm] * w[2:3]
        cs_ref[0] = xx[tm:tm + 2 * nb]
    up = _dot(hb, wu_ref[...])
    y = _dot((_silu(z) * up).astype(BF16), wd_ref[...])
    x2 = x1 + g2_ref[0] * y
    if final:
        x2 = _rms(x2, fng_ref[...])
    y_ref[0] = x2


def _ffn(x, o4, mods, fw, l, buf, final_g, tm, seq, final, nb=0):
    G, T, D = x.shape
    per_row = mods[0].shape[1] != 1
    mod_spec = (pl.BlockSpec((1, tm, D), lambda b, i: (b, i, 0)) if per_row
                else pl.BlockSpec((1, 1, D), lambda b, i: (b, 0, 0)))
    o_spec = pl.BlockSpec((1, tm, 256), lambda b, i: (b, i, 0))
    once = pl.Buffered(1)
    wspec = lambda shape: pl.BlockSpec((None,) + shape, lambda b, i: (l,) + (0,) * len(shape), pipeline_mode=once)
    nbuf = buf.shape[1]
    return pl.pallas_call(
        functools.partial(_ffn_kernel, tm=tm, seq=seq, final=final, nb=nb),
        grid=(G, T // tm),
        in_specs=[pl.BlockSpec((1, tm, D), lambda b, i: (b, i, 0)), o_spec, o_spec, o_spec, o_spec,
                  mod_spec, mod_spec, mod_spec, mod_spec,
                  wspec((1, D)), wspec((D, D)), wspec((D, D_FF)), wspec((D, D_FF)), wspec((D_FF, D)),
                  wspec((FFN_CONV, D_FF)), wspec((1, D_FF)),
                  pl.BlockSpec((1, nbuf, D_FF), lambda b, i: (b, 0, 0)),
                  pl.BlockSpec((1, D), lambda b, i: (0, 0))],
        out_specs=[pl.BlockSpec((1, tm, D), lambda b, i: (b, i, 0)),
                   pl.BlockSpec((1, nbuf, D_FF), lambda b, i: (b, 0, 0))],
        out_shape=[jax.ShapeDtypeStruct((G, T, D), F32), jax.ShapeDtypeStruct((G, nbuf, D_FF), F32)],
        scratch_shapes=[pltpu.VMEM((8, D_FF), F32)] if seq else [],
        compiler_params=_cparams("arbitrary", "arbitrary"),
        name="out_ffn",
    )(x, *o4, *mods, fw["n2"], fw["wout"], fw["wg"], fw["wu"], fw["wd"], fw["cw"], fw["cb"], buf, final_g)


def _diff_dec_kernel(pt_ref, lam_ref, q_ref, *refs, n_new, out_scale, nps):
    kT_refs, vT_refs = refs[:nps], refs[nps:2 * nps]
    kn_ref, vnT_ref, g_ref, ones_ref, o_ref, qs_ref, m_ref, l_ref, acc_ref, osc_ref = refs[2 * nps:]
    p = pl.program_id(1)
    lane = lax.broadcasted_iota(I32, (8, 256), 1)

    @pl.when(p == 0)
    def _():
        q = q_ref[0]
        for hm in range(8):
            qs_ref[8 * hm:8 * hm + 8, :] = jnp.where((lane // 32) == hm, q, 0.0).astype(BF16)
        m_ref[...] = jnp.full(m_ref.shape, NEG_BIG, F32)
        l_ref[...] = jnp.zeros(l_ref.shape, F32)
        acc_ref[...] = jnp.zeros(acc_ref.shape, F32)

    def update(s, v_contract):
        m_prev = m_ref[...]
        m_new = jnp.maximum(m_prev, jnp.max(s, axis=1, keepdims=True))
        alpha = jnp.exp(m_prev - m_new)
        pr = jnp.exp(s - m_new)
        l_ref[...] = alpha * l_ref[...] + jnp.sum(pr, axis=1, keepdims=True)
        acc_ref[...] = alpha * acc_ref[...] + v_contract(pr.astype(BF16))
        m_ref[...] = m_new

    qs = qs_ref[...]
    kcat = jnp.concatenate([r[0, 0].astype(BF16) for r in kT_refs], axis=1)
    vcat = jnp.concatenate([r[0, 0].astype(BF16) for r in vT_refs], axis=1)
    update(_dot(qs, kcat) * (DIFF_QK ** -0.5), lambda pr: _dot_nt(pr, vcat))

    @pl.when(p == pl.num_programs(1) - 1)
    def _():
        s = _dot_nt(qs, kn_ref[0].astype(BF16)) * (DIFF_QK ** -0.5)
        t = lax.broadcasted_iota(I32, s.shape, 0) % 8
        c = lax.broadcasted_iota(I32, s.shape, 1)
        s = jnp.where((c <= t) & (c < n_new), s, NEG_BIG)
        vnT = vnT_ref[0].astype(BF16)
        update(s, lambda pr: _dot_nt(pr, vnT))
        o_ref[0] = _diff_finalize(acc_ref, l_ref, lam_ref[0, 0], osc_ref, g_ref, ones_ref, out_scale, 8, True)


def _page_specs(l, nps, rows):
    return [pl.BlockSpec((1, 1, rows, PAGE), lambda b, p, pt, i=i: (l, pt[b, p * nps + i], 0, 0))
            for i in range(nps)]


def _diff_dec(page_table, lam, q8, ckT, cvT, kn, vnT, g_tiled, ones_bd, l, out_scale, n_new):
    B, NP = page_table.shape
    nps = math.gcd(NP, PAGES_PER_STEP)
    gs = pltpu.PrefetchScalarGridSpec(
        num_scalar_prefetch=1, grid=(B, NP // nps),
        in_specs=[pl.BlockSpec(memory_space=pltpu.SMEM),
                  pl.BlockSpec((1, 8, 256), lambda b, p, pt: (b, 0, 0))]
                 + _page_specs(l, nps, 256) + _page_specs(l, nps, 256)
                 + [pl.BlockSpec((1, 128, 256), lambda b, p, pt: (b, 0, 0)),
                    pl.BlockSpec((1, 256, 128), lambda b, p, pt: (b, 0, 0)),
                    pl.BlockSpec((None, 1, 256), lambda b, p, pt: (l, 0, 0)),
                    pl.BlockSpec((256, 256), lambda b, p, pt: (0, 0))],
        out_specs=pl.BlockSpec((1, 8, 256), lambda b, p, pt: (b, 0, 0)),
        scratch_shapes=[pltpu.VMEM((64, 256), BF16), pltpu.VMEM((64, 1), F32), pltpu.VMEM((64, 1), F32),
                        pltpu.VMEM((64, 256), F32), pltpu.VMEM((8, 256), F32)])
    return pl.pallas_call(
        functools.partial(_diff_dec_kernel, n_new=n_new, out_scale=out_scale, nps=nps),
        grid_spec=gs, out_shape=jax.ShapeDtypeStruct((B, 8, 256), F32),
        compiler_params=_cparams("arbitrary", "arbitrary"),
        name="diff_decode",
    )(page_table, lam, q8, *([ckT] * nps), *([cvT] * nps), kn, vnT, g_tiled, ones_bd)


def _idx_dec_kernel(pt_ref, iq_ref, w_ref, *refs, n_new, nps):
    ikT_refs = refs[:nps]
    iknT_ref, keys_ref, keysn_ref = refs[nps:]
    p = pl.program_id(1)
    iq = iq_ref[0].astype(BF16)
    w = w_ref[0]

    def scores(ikT):
        sc = _dot(iq, ikT.astype(BF16)) * (IDX_DIM ** -0.5)
        tot = jnp.sum((w * jnp.maximum(sc, 0.0)).reshape(8, 8, PAGE), axis=1)
        return jnp.where(tot == 0.0, 0.0, tot)

    for i in range(nps):
        keys_ref[0, i] = scores(ikT_refs[i][0, 0])

    @pl.when(p == pl.num_programs(1) - 1)
    def _():
        tot = scores(iknT_ref[0])
        t = lax.broadcasted_iota(I32, tot.shape, 0)
        c = lax.broadcasted_iota(I32, tot.shape, 1)
        keysn_ref[0] = jnp.where((c <= t) & (c < n_new), tot, -jnp.inf)


def _idx_dec(page_table, iq64, w64, cikT, iknT, l, n_new):
    B, NP = page_table.shape
    nps = math.gcd(NP, PAGES_PER_STEP)
    gs = pltpu.PrefetchScalarGridSpec(
        num_scalar_prefetch=1, grid=(B, NP // nps),
        in_specs=[pl.BlockSpec((1, 64, IDX_DIM), lambda b, p, pt: (b, 0, 0)),
                  pl.BlockSpec((1, 64, 1), lambda b, p, pt: (b, 0, 0))]
                 + _page_specs(l, nps, IDX_DIM)
                 + [pl.BlockSpec((1, IDX_DIM, 128), lambda b, p, pt: (b, 0, 0))],
        out_specs=[pl.BlockSpec((1, nps, 8, PAGE), lambda b, p, pt: (b, p, 0, 0)),
                   pl.BlockSpec((1, 8, 128), lambda b, p, pt: (b, 0, 0))])
    return pl.pallas_call(
        functools.partial(_idx_dec_kernel, n_new=n_new, nps=nps),
        grid_spec=gs,
        out_shape=[jax.ShapeDtypeStruct((B, NP, 8, PAGE), F32), jax.ShapeDtypeStruct((B, 8, 128), F32)],
        compiler_params=_cparams("arbitrary", "arbitrary"),
        name="idx_decode",
    )(page_table, iq64, w64, *([cikT] * nps), iknT)


def _dec_thr_kernel(keys_ref, keysn_ref, thr_ref, g_ref, *, topk, n_pages, n_col_bits):
    nb = keys_ref.shape[0]
    new_col = lax.broadcasted_iota(I32, (nb, 8, PAGE), 2) + n_pages * PAGE

    def count_ge(c):
        acc = (jnp.sum(jnp.where(keys_ref[...] >= c[:, None], 1.0, 0.0), axis=1)
               + jnp.where(keysn_ref[...] >= c, 1.0, 0.0))
        return jnp.sum(acc, axis=2, keepdims=True)

    def count_tie_below(tau, c):
        page_col = (lax.broadcasted_iota(I32, (nb, n_pages, 8, PAGE), 1) * PAGE
                    + lax.broadcasted_iota(I32, (nb, n_pages, 8, PAGE), 3))
        hit = jnp.where(keys_ref[...] == tau[:, None], jnp.where(page_col < c[:, None], 1.0, 0.0), 0.0)
        acc = (jnp.sum(hit, axis=1)
               + jnp.where(keysn_ref[...] == tau, jnp.where(new_col < c, 1.0, 0.0), 0.0))
        return jnp.sum(acc, axis=2, keepdims=True)

    t0, t1, g = _topk_threshold(count_ge, count_tie_below, topk, n_col_bits, (nb, 8, 1))
    thr_ref[...] = jnp.zeros(thr_ref.shape, F32)
    thr_ref[:, :, 0:1] = t0
    thr_ref[:, :, 1:2] = t1
    g_ref[...] = jnp.broadcast_to(g, g_ref.shape)


def _dec_thr(keys, keysn, topk):
    B, NP = keys.shape[:2]
    nb = math.gcd(B, 8)
    n_keys = NP * PAGE + 128
    return pl.pallas_call(
        functools.partial(_dec_thr_kernel, topk=topk, n_pages=NP, n_col_bits=(n_keys - 1).bit_length()),
        grid=(B // nb,),
        in_specs=[pl.BlockSpec((nb, NP, 8, PAGE), lambda b: (b, 0, 0, 0)),
                  pl.BlockSpec((nb, 8, 128), lambda b: (b, 0, 0))],
        out_specs=[pl.BlockSpec((nb, 8, 128), lambda b: (b, 0, 0)), pl.BlockSpec((nb, 8, 128), lambda b: (b, 0, 0))],
        out_shape=[jax.ShapeDtypeStruct((B, 8, 128), F32), jax.ShapeDtypeStruct((B, 8, 128), I32)],
        compiler_params=_cparams("arbitrary"),
        name="decode_threshold",
    )(keys, keysn)


def _dsa_dec_kernel(pt_ref, q_ref, keys_ref, keysn_ref, thr_ref, g_ref, *refs, n_pages, nps):
    kT_refs, vT_refs = refs[:nps], refs[nps:2 * nps]
    kn_ref, vnT_ref, o_ref, qs_ref, m_ref, l_ref, acc_ref = refs[2 * nps:]
    p = pl.program_id(1)
    lane = lax.broadcasted_iota(I32, (8, 256), 1)
    colp = lax.broadcasted_iota(I32, (8, PAGE), 1)

    @pl.when(p == 0)
    def _():
        q = q_ref[0]
        for h in range(DSA_HEADS):
            qs_ref[8 * h:8 * h + 8, :] = jnp.where((lane // 64) == h, q, 0.0).astype(BF16)
        m_ref[...] = jnp.full(m_ref.shape, NEG_BIG, F32)
        l_ref[...] = jnp.zeros(l_ref.shape, F32)
        acc_ref[...] = jnp.zeros(acc_ref.shape, F32)

    t0 = thr_ref[0, :, 0:1]
    t1 = thr_ref[0, :, 1:2]
    g = g_ref[0, :, 0:1]

    def update(s, key8, thr8, v_contract):
        sel = jnp.concatenate([key8] * DSA_HEADS, axis=0) >= jnp.concatenate([thr8] * DSA_HEADS, axis=0)
        m_prev = m_ref[...]
        m_new = jnp.maximum(m_prev, jnp.max(jnp.where(sel, s, NEG_BIG), axis=1, keepdims=True))
        alpha = jnp.exp(m_prev - m_new)
        pr = jnp.where(sel, jnp.exp(s - m_new), 0.0)
        l_ref[...] = alpha * l_ref[...] + jnp.sum(pr, axis=1, keepdims=True)
        acc_ref[...] = alpha * acc_ref[...] + v_contract(pr.astype(BF16))
        m_ref[...] = m_new

    qs = qs_ref[...]
    kcat = jnp.concatenate([r[0, 0].astype(BF16) for r in kT_refs], axis=1)
    vcat = jnp.concatenate([r[0, 0].astype(BF16) for r in vT_refs], axis=1)
    s = _dot(qs, kcat) * (DSA_HD ** -0.5)
    key8 = jnp.concatenate([keys_ref[0, p * nps + i] for i in range(nps)], axis=1)
    col = lax.broadcasted_iota(I32, key8.shape, 1) + p * (nps * PAGE)
    thr8 = jnp.where(col > g, t1, t0)

    update(s, key8, thr8, lambda pr: _dot_nt(pr, vcat))

    @pl.when(p == pl.num_programs(1) - 1)
    def _():
        thrn = jnp.where(colp + n_pages * PAGE > g, t1, t0)
        s = _dot_nt(qs, kn_ref[0].astype(BF16)) * (DSA_HD ** -0.5)
        vnT = vnT_ref[0].astype(BF16)
        update(s, keysn_ref[0], thrn, lambda pr: _dot_nt(pr, vnT))
        for h in range(DSA_HEADS):
            o_ref[0, :, 64 * h:64 * h + 64] = (acc_ref[8 * h:8 * h + 8, 64 * h:64 * h + 64]
                                               / l_ref[8 * h:8 * h + 8, :])


def _dsa_dec(page_table, q8, keys, keysn, ckT, cvT, kn, vnT, l, topk):
    B, NP = page_table.shape
    nps = math.gcd(NP, PAGES_PER_STEP)
    thr, gcol = _dec_thr(keys, keysn, topk)
    gs = pltpu.PrefetchScalarGridSpec(
        num_scalar_prefetch=1, grid=(B, NP // nps),
        in_specs=[pl.BlockSpec((1, 8, 256), lambda b, p, pt: (b, 0, 0)),
                  pl.BlockSpec((1, NP, 8, PAGE), lambda b, p, pt: (b, 0, 0, 0)),
                  pl.BlockSpec((1, 8, 128), lambda b, p, pt: (b, 0, 0)),
                  pl.BlockSpec((1, 8, 128), lambda b, p, pt: (b, 0, 0)),
                  pl.BlockSpec((1, 8, 128), lambda b, p, pt: (b, 0, 0))]
                 + _page_specs(l, nps, 256) + _page_specs(l, nps, 256)
                 + [pl.BlockSpec((1, 128, 256), lambda b, p, pt: (b, 0, 0)),
                    pl.BlockSpec((1, 256, 128), lambda b, p, pt: (b, 0, 0))],
        out_specs=pl.BlockSpec((1, 8, 256), lambda b, p, pt: (b, 0, 0)),
        scratch_shapes=[pltpu.VMEM((32, 256), BF16), pltpu.VMEM((32, 1), F32),
                        pltpu.VMEM((32, 1), F32), pltpu.VMEM((32, 256), F32)])
    return pl.pallas_call(
        functools.partial(_dsa_dec_kernel, n_pages=NP, nps=nps),
        grid_spec=gs, out_shape=jax.ShapeDtypeStruct((B, 8, 256), F32),
        compiler_params=_cparams("arbitrary", "arbitrary"),
        name="dsa_decode",
    )(page_table, q8, keys, keysn, thr, gcol, *([ckT] * nps), *([cvT] * nps), kn, vnT)


def _rope_tables(pos, half):
    inv = ROPE_THETA ** (-jnp.arange(half, dtype=F32) / half)
    ang = pos.astype(F32)[:, None] * inv[None, :]
    cos, sin = jnp.cos(ang), jnp.sin(ang)
    reps = 128 // (2 * half)
    return (jnp.tile(jnp.concatenate([cos, cos], axis=1), (1, reps)),
            jnp.tile(jnp.concatenate([-sin, sin], axis=1), (1, reps)))


def _pack_w_in(w_in):
    names = (("lru_x", 256), ("lru_gate", 256), ("diff_q", 256), ("diff_k", 256), ("diff_v", 256),
             ("dsa_q", 256), ("dsa_k", 256), ("dsa_v", 256), ("idx_q", 256), ("idx_k", 32), ("idx_w", 8),
             ("gla_q", 128), ("gla_k", 128), ("gla_v", 256), ("gla_g", 256), ("gla_a", 16))
    seg, off = {}, 0
    for n, w in names:
        seg[n] = w_in[..., off:off + w]
        off += w
    pad = jnp.zeros(w_in.shape[:-1] + (128 - 24,), w_in.dtype)
    cols = [seg["lru_x"], seg["lru_gate"], seg["diff_q"], seg["diff_k"],
            seg["dsa_q"], seg["dsa_k"], seg["idx_q"]] + [seg["idx_k"]] * IDX_HEADS + [
            seg["gla_q"], seg["gla_k"], seg["gla_v"], seg["gla_g"], seg["idx_w"], seg["gla_a"], pad]
    wvT = jnp.concatenate([seg["diff_v"], seg["dsa_v"]], axis=-1).transpose(0, 2, 1)
    return jnp.concatenate(cols, axis=-1).astype(BF16), wvT.astype(BF16)


def _block_diag(w):
    depth, nb, bw, _ = w.shape
    eye = jnp.eye(nb, dtype=w.dtype)
    return jnp.einsum("lnij,nm->lnimj", w, eye).reshape(depth, nb * bw, nb * bw)


def _group_ones(n, group):
    r = jnp.arange(n) // group
    return (r[:, None] == r[None, :]).astype(BF16)


def _prompt_layer(x, mod, l, W, tabs, final, tm=256, tq=256):
    G, T, _ = x.shape
    sh1, sc1, g1, sh2, sc2, g2 = [m[:, None, :] for m in jnp.split(mod, 6, axis=-1)]
    (lru, qb, kb, vbT, qc, kc, vcT, iq, ikt, gqk, gv, gg, misc) = _proj(
        x, sh1, sc1, W["norm1"], W["w_in"], W["wvT"], l, tabs, tm)
    zeros = lambda *s: jnp.zeros(s, F32)
    o_a, h_last, cs8 = _lru_seq(lru, W["lru"], l, zeros(G, 1, 256), zeros(G, 8, 256), tm)
    o_b = _diff_attn(W["lam"][l], qb, kb, vbT, W["subln"], l, W["out_scale"][l], tq)
    o_c = _dsa_attn(qc, kc, vcT, iq, ikt, misc, min(TOPK_MAX, T // 4), tq)
    from_T = lambda a: a.reshape(G, 4, 64, T).transpose(0, 3, 1, 2)
    o_d, s_last = _gla(gqk, gv, gg, misc, W["gla"], l, zeros(G, GLA_HEADS, GLA_DK, GLA_DV),
                       W["ones64"], W["bd_mask"], tm, GLA_CHUNK, None)
    y, fcs8 = _ffn(x, (o_a, o_b, o_c, o_d), (g1, sh2, sc2, g2), W["ffn"], l, zeros(G, 8, D_FF),
                   W["final_g"], tm, True, final)
    new = dict(
        diff_k=kb.reshape(G, T, DIFF_HEADS, 2, DIFF_QK), diff_v=from_T(vbT),
        dsa_k=kc.reshape(G, T, DSA_HEADS, DSA_HD), dsa_v=from_T(vcT),
        idx_k=ikt[:, :, :IDX_DIM], lru_h=h_last[:, 0], lru_conv=cs8[:, 8 - (LRU_CONV - 1):],
        gla=s_last, ffn_conv=fcs8[:, 8 - (FFN_CONV - 1):])
    return y, new


def _pad_rows(a, n):
    return jnp.pad(a, ((0, 0), (0, n - a.shape[1])) + ((0, 0),) * (a.ndim - 2))


def _sample_layer(xr, mod, l, W, tabs, caches, states, page_table, final, B, Tn):
    n = Tn * B
    tm_major = lambda a: jnp.tile(a, (Tn, 1))[None]
    sh1, sc1, g1, sh2, sc2, g2 = [tm_major(m) for m in jnp.split(mod, 6, axis=-1)]
    outs = _proj(xr, sh1, sc1, W["norm1"], W["w_in"], W["wvT"], l, tabs, n)
    (lru, qb, kb, vbT, qc, kc, vcT, iq, ikt, gqk, gv, gg, misc) = [o[0] for o in outs]
    to_b = lambda a: a.reshape(Tn, B, -1).transpose(1, 0, 2)
    to_t = lambda a: a.transpose(1, 0, 2).reshape(n, -1)
    to_bT = lambda a: a.reshape(256, Tn, B).transpose(2, 0, 1)
    pad_lanes = lambda a: jnp.pad(a, ((0, 0), (0, 0), (0, 128 - a.shape[2])))

    conv_prev = states["lru_conv"][l].transpose(1, 0, 2).reshape((LRU_CONV - 1) * B, 256)
    xx = jnp.concatenate([conv_prev, lru[:, :256]], axis=0)
    o_a, h_last = _lru_rows(xx, lru[:, 256:], W["lru"], l, states["lru_h"][l], B, Tn)
    new_conv = xx[Tn * B:].reshape(LRU_CONV - 1, B, 256).transpose(1, 0, 2)

    kb_b, vb_bT = to_b(kb), to_bT(vbT)
    o_b = _diff_dec(page_table, W["lam"][l], _pad_rows(to_b(qb), 8), caches["diff_kT"], caches["diff_vT"],
                    _pad_rows(kb_b, 128), pad_lanes(vb_bT), W["subln"], W["ones64"], l,
                    W["out_scale"][l], Tn)[:, :Tn]

    past_len = page_table.shape[1] * PAGE
    topk = min(TOPK_MAX, (past_len + Tn) // 4)
    kc_b, vc_bT = to_b(kc), to_bT(vcT)
    ik_b = to_b(ikt)[:, :, :IDX_DIM]
    iq64 = _pad_rows(to_b(iq), 8).reshape(B, 8 * IDX_HEADS, IDX_DIM)
    w64 = (_pad_rows(to_b(misc)[:, :, :IDX_HEADS], 8) * (IDX_HEADS ** -0.5)).reshape(B, 8 * IDX_HEADS, 1)
    iknT = _pad_rows(ik_b, 128).transpose(0, 2, 1)
    keys, keysn = _idx_dec(page_table, iq64, w64, caches["idx_kT"], iknT, l, Tn)
    o_c = _dsa_dec(page_table, _pad_rows(to_b(qc), 8), keys, keysn, caches["dsa_kT"], caches["dsa_vT"],
                   _pad_rows(kc_b, 128), pad_lanes(vc_bT), l, topk)[:, :Tn]

    padc = lambda a: _pad_rows(to_b(a), GLA_CHUNK)
    o_d, s_last = _gla(padc(gqk), padc(gv), padc(gg), padc(misc), W["gla"], l, states["gla"][l],
                       W["ones64"], W["bd_mask"], GLA_CHUNK, GLA_CHUNK, Tn)
    o_d = o_d[:, :Tn]

    ffn_prev = states["ffn_conv"][l].transpose(1, 0, 2).reshape(1, (FFN_CONV - 1) * B, D_FF)
    o4 = (o_a[None], to_t(o_b)[None], to_t(o_c)[None], to_t(o_d)[None])
    y, fcs = _ffn(xr, o4, (g1, sh2, sc2, g2), W["ffn"], l, ffn_prev, W["final_g"], n, False, final, nb=B)
    new = dict(
        diff_k=kb_b.reshape(B, Tn, DIFF_HEADS, 2, DIFF_QK),
        diff_v=vb_bT.transpose(0, 2, 1).reshape(B, Tn, DIFF_HEADS, 64),
        dsa_k=kc_b.reshape(B, Tn, DSA_HEADS, DSA_HD),
        dsa_v=vc_bT.transpose(0, 2, 1).reshape(B, Tn, DSA_HEADS, DSA_HD),
        idx_k=ik_b, lru_h=h_last, lru_conv=new_conv, gla=s_last,
        ffn_conv=fcs[0].reshape(FFN_CONV - 1, B, D_FF).transpose(1, 0, 2))
    return y, new


def kernel(x_prompt, x_sample, cache_diff_k, cache_diff_v, cache_dsa_k, cache_dsa_v, cache_idx_k, state_lru_h, state_lru_conv, state_gla, state_ffn_conv, page_table, c_prompt, c_sample, w_ada, b_ada, norm1_g, w_in, lru_conv_w, lru_conv_b, lru_wa, lru_ba, lru_wx, lru_bx, lru_lambda, diff_lq1, diff_lk1, diff_lq2, diff_lk2, diff_subln_g, gla_wa2, gla_ba, gla_norm_g, w_out, norm2_g, ffn_w_gate, ffn_w_up, ffn_conv_w, ffn_conv_b, ffn_w_down, final_norm_g):
    depth = w_in.shape[0]
    Bp, T, D = x_prompt.shape
    Bs, Tn, _ = x_sample.shape
    n_pool = cache_diff_k.shape[1]
    past_len = page_table.shape[1] * PAGE

    lam_init = jnp.asarray([0.8 - 0.6 * math.exp(-0.3 * l) for l in range(depth)], F32)
    lam = (jnp.exp(jnp.sum(diff_lq1 * diff_lk1, axis=-1)) - jnp.exp(jnp.sum(diff_lq2 * diff_lk2, axis=-1))
           + lam_init).reshape(depth, 1, 1)
    row3 = lambda a: a.reshape(depth, 1, a.shape[-1])
    wa2_pad = jnp.zeros((depth, 128, 128), F32).at[:, IDX_HEADS:IDX_HEADS + gla_wa2.shape[1], :].set(gla_wa2)
    w_packed, wvT = _pack_w_in(w_in)
    W = dict(
        norm1=row3(norm1_g), w_in=w_packed, wvT=wvT,
        lru=dict(conv_w=lru_conv_w, conv_b=row3(lru_conv_b), wa=_block_diag(lru_wa).astype(BF16),
                 wx=_block_diag(lru_wx).astype(BF16), ba=row3(lru_ba), bx=row3(lru_bx), lam=row3(lru_lambda)),
        lam=lam, out_scale=[1.0 - (0.8 - 0.6 * math.exp(-0.3 * l)) for l in range(depth)],
        subln=row3(jnp.tile(diff_subln_g, (1, DIFF_HEADS))), ones64=_group_ones(256, 64),
        bd_mask=(jnp.arange(128)[:, None] // GLA_DK == jnp.arange(256)[None, :] // GLA_DV).astype(F32),
        gla=dict(wa=wa2_pad.astype(BF16), ba=row3(gla_ba), gn=row3(jnp.tile(gla_norm_g, (1, GLA_HEADS)))),
        ffn=dict(n2=row3(norm2_g), wout=w_out.astype(BF16), wg=ffn_w_gate.astype(BF16),
                 wu=ffn_w_up.astype(BF16), wd=ffn_w_down.astype(BF16), cw=ffn_conv_w, cb=row3(ffn_conv_b)),
        final_g=final_norm_g.reshape(1, D))

    mods = _ada(jnp.concatenate([c_prompt, c_sample], axis=0), w_ada, b_ada)

    pos_p = jnp.arange(T, dtype=I32)
    pos_s = jnp.repeat(past_len + jnp.arange(Tn, dtype=I32), Bs)
    tabs_p = _rope_tables(pos_p, 16) + _rope_tables(pos_p, 32)
    tabs_s = _rope_tables(pos_s, 16) + _rope_tables(pos_s, 32)

    caches = dict(
        diff_kT=cache_diff_k.transpose(0, 1, 3, 4, 5, 2).reshape(depth, n_pool, 256, PAGE),
        diff_vT=cache_diff_v.transpose(0, 1, 3, 4, 2).reshape(depth, n_pool, 256, PAGE),
        dsa_kT=cache_dsa_k.transpose(0, 1, 3, 4, 2).reshape(depth, n_pool, 256, PAGE),
        dsa_vT=cache_dsa_v.transpose(0, 1, 3, 4, 2).reshape(depth, n_pool, 256, PAGE),
        idx_kT=cache_idx_k.transpose(0, 1, 3, 2))
    states = dict(lru_h=state_lru_h, lru_conv=state_lru_conv, gla=state_gla, ffn_conv=state_ffn_conv)

    xp = x_prompt
    xs = x_sample.transpose(1, 0, 2).reshape(1, Tn * Bs, D)
    st_p, st_s = [], []
    for l in range(depth):
        final = l == depth - 1
        xp, new_p = _prompt_layer(xp, mods[l, :Bp], l, W, tabs_p, final)
        xs, new_s = _sample_layer(xs, mods[l, Bp:], l, W, tabs_s, caches, states, page_table, final, Bs, Tn)
        st_p.append(new_p)
        st_s.append(new_s)
    y_prompt = xp
    y_sample = xs.reshape(Tn, Bs, D).transpose(1, 0, 2)
    keys = ("diff_k", "diff_v", "dsa_k", "dsa_v", "idx_k", "lru_h", "lru_conv", "gla", "ffn_conv")
    P = [jnp.stack([s[k] for s in st_p]) for k in keys]
    S = [jnp.stack([s[k] for s in st_s]) for k in keys]
    return (y_prompt, y_sample, *P, *S)
```

```python
import functools
import math

import jax
import jax.numpy as jnp
from jax import lax
from jax.experimental import pallas as pl
from jax.experimental.pallas import tpu as pltpu

F32 = jnp.float32
BF16 = jnp.bfloat16
I32 = jnp.int32

D_MODEL = 1024
GROUP_W = 256
LRU_CONV = 4
LRU_C = 8.0
DIFF_HEADS = 4
DIFF_QK = 32
DSA_HEADS = 4
DSA_HD = 64
IDX_HEADS = 8
IDX_DIM = 32
TOPK_MAX = 256
GLA_HEADS = 4
GLA_DK = 32
GLA_DV = 64
GLA_TAU = 16.0
GLA_CHUNK = 64
D_FF = 2816
FFN_CONV = 3
ROPE_THETA = 10000.0
EPS = 1e-6
PAGE = 128

P_LRU, P_DIFF_QK, P_DSA_QK, P_IDX, P_GLA_QK, P_GLA_V, P_GLA_G, P_MISC, P_END = (
    0, 512, 1024, 1536, 2048, 2304, 2560, 2816, 2944)
SEQ_TILE = 256
ATTN_TILE = 256
DENSE_TILE = 512
PAGES_PER_STEP = 32
HEADS_PER_STAGE = 8

VMEM_LIMIT_BYTES = 56 * 1024 * 1024
NEG_BIG = -1e30
LOG2E = 1.4426950408889634
KEY_NEG_INF = -2 ** 31 + 0x7FFFFF
BIG_COL = 2 ** 30


def _cparams(*sem):
    return pltpu.CompilerParams(dimension_semantics=sem, vmem_limit_bytes=VMEM_LIMIT_BYTES)


def _rms(x, g):
    return x * lax.rsqrt(jnp.mean(x * x, axis=-1, keepdims=True) + EPS) * g


def _silu(x):
    return x * jax.nn.sigmoid(x)


def _softplus(z):
    return jnp.maximum(z, 0.0) + jnp.log1p(jnp.exp(-jnp.abs(z)))


def _dot(a, b):
    return jnp.dot(a, b, preferred_element_type=F32)


def _dot_nt(a, b):
    return lax.dot_general(a, b, (((1,), (1,)), ((), ())), preferred_element_type=F32)


def _group_mean64(x2, ones_bd):
    hi = x2.astype(BF16)
    lo = (x2 - hi.astype(F32)).astype(BF16)
    return (_dot(hi, ones_bd) + _dot(lo, ones_bd)) * (1.0 / 64.0)


def _rope(x, cos, sin_signed, half):
    n = x.shape[1]
    reps = n // 128
    c = jnp.concatenate([cos] * reps, axis=1) if reps > 1 else cos
    s = jnp.concatenate([sin_signed] * reps, axis=1) if reps > 1 else sin_signed
    lane = lax.broadcasted_iota(I32, x.shape, 1)
    first = (lane % (2 * half)) < half
    swapped = jnp.where(first, pltpu.roll(x, n - half, 1), pltpu.roll(x, half, 1))
    return x * c + swapped * s


def _shift_rows(x, s, prev8):
    r = pltpu.roll(x, s, 0)
    p = pltpu.roll(prev8, s, 0)
    row = lax.broadcasted_iota(I32, (8, x.shape[1]), 0)
    top = jnp.where(row < s, p, r[:8])
    return jnp.concatenate([top, r[8:]], axis=0)


def _cumsum_rows(x):
    n = x.shape[0]
    row = lax.broadcasted_iota(I32, x.shape, 0)
    s = 1
    while s < n:
        x = x + jnp.where(row >= s, pltpu.roll(x, s, 0), 0.0)
        s *= 2
    return x


def _key_to_float(key):
    return pltpu.bitcast(key ^ ((key >> 31) & 0x7FFFFFFF), F32)


def _ada_kernel(c_ref, w_ref, b_ref, o_ref):
    c = c_ref[...]
    o_ref[...] = _dot(_silu(c).astype(BF16), w_ref[...].astype(BF16)) + b_ref[...]


def _ada(c_all, w_ada, b_ada):
    depth = w_ada.shape[0]
    n = c_all.shape[0]
    tn = 1024
    return pl.pallas_call(
        _ada_kernel,
        grid=(depth, 6 * D_MODEL // tn),
        in_specs=[pl.BlockSpec((n, D_MODEL), lambda l, j: (0, 0)),
                  pl.BlockSpec((None, D_MODEL, tn), lambda l, j: (l, 0, j)),
                  pl.BlockSpec((None, 1, tn), lambda l, j: (l, 0, j))],
        out_specs=pl.BlockSpec((None, n, tn), lambda l, j: (l, 0, j)),
        out_shape=jax.ShapeDtypeStruct((depth, n, 6 * D_MODEL), F32),
        compiler_params=_cparams("arbitrary", "arbitrary"),
        name="ada_mod",
    )(c_all, w_ada, b_ada.reshape(depth, 1, 6 * D_MODEL))


def _proj_kernel(x_ref, sh_ref, sc_ref, g_ref, w_ref, wvT_ref, c32_ref, s32_ref, c64_ref, s64_ref,
                 lru_ref, qb_ref, kb_ref, vbT_ref, qc_ref, kc_ref, vcT_ref, iq_ref, ikt_ref,
                 gqk_ref, gv_ref, gg_ref, misc_ref):
    x = x_ref[0]
    h = _rms(x, g_ref[...]) * (1.0 + sc_ref[0]) + sh_ref[0]
    hb = h.astype(BF16)

    def mm(a, b):
        return _dot(hb, w_ref[:, a:b])

    lru_ref[0] = mm(P_LRU, P_DIFF_QK)
    c32, s32, c64, s64 = c32_ref[...], s32_ref[...], c64_ref[...], s64_ref[...]
    qk = _rope(mm(P_DIFF_QK, P_DSA_QK), c32, s32, 16)
    qb_ref[0] = qk[:, :256]
    kb_ref[0] = qk[:, 256:]
    qk = _rope(mm(P_DSA_QK, P_IDX), c64, s64, 32)
    qc_ref[0] = qk[:, :256]
    kc_ref[0] = qk[:, 256:]
    qk = _rope(mm(P_IDX, P_GLA_QK), c32, s32, 16)
    iq_ref[0] = qk[:, :256]
    ikt_ref[0] = qk[:, 256:]
    gqk_ref[0] = mm(P_GLA_QK, P_GLA_V)
    gv_ref[0] = mm(P_GLA_V, P_GLA_G)
    gg_ref[0] = mm(P_GLA_G, P_MISC)
    misc_ref[0] = mm(P_MISC, P_END)
    vbT_ref[0] = _dot_nt(wvT_ref[0:256, :], hb)
    vcT_ref[0] = _dot_nt(wvT_ref[256:512, :], hb)


_PROJ_WIDTHS = (512, 256, 256, None, 256, 256, None, 256, 256, 256, 256, 256, 128)


def _proj(x, sh, sc, norm_g, w_packed, wvT, l, tabs, tm):
    G, T, D = x.shape
    per_row = sh.shape[1] != 1
    mod_spec = (pl.BlockSpec((1, tm, D), lambda b, i: (b, i, 0)) if per_row
                else pl.BlockSpec((1, 1, D), lambda b, i: (b, 0, 0)))
    tab_spec = pl.BlockSpec((tm, 128), lambda b, i: (i, 0))
    out_specs = [pl.BlockSpec((1, 256, tm), lambda b, i: (b, 0, i)) if w is None
                 else pl.BlockSpec((1, tm, w), lambda b, i: (b, i, 0)) for w in _PROJ_WIDTHS]
    out_shape = [jax.ShapeDtypeStruct((G, 256, T) if w is None else (G, T, w), F32) for w in _PROJ_WIDTHS]
    return pl.pallas_call(
        _proj_kernel,
        grid=(G, T // tm),
        in_specs=[pl.BlockSpec((1, tm, D), lambda b, i: (b, i, 0)), mod_spec, mod_spec,
                  pl.BlockSpec((None, 1, D), lambda b, i: (l, 0, 0)),
                  pl.BlockSpec((None, D, P_END), lambda b, i: (l, 0, 0)),
                  pl.BlockSpec((None, 512, D), lambda b, i: (l, 0, 0)),
                  tab_spec, tab_spec, tab_spec, tab_spec],
        out_specs=out_specs, out_shape=out_shape,
        compiler_params=_cparams("arbitrary", "arbitrary"),
        name="in_proj",
    )(x, sh, sc, norm_g, w_packed, wvT, *tabs)


def _lru_gates(xa, gate_in, wa_ref, wx_ref, ba_ref, bx_ref, lam_ref):
    xb = xa.astype(BF16)
    ga = jax.nn.sigmoid(_dot(xb, wa_ref[...]) + ba_ref[...])
    gx = jax.nn.sigmoid(_dot(xb, wx_ref[...]) + bx_ref[...])
    log_a = (-LRU_C) * ga * _softplus(-lam_ref[...])
    a = jnp.exp(log_a)
    t = jnp.tanh(log_a)
    mult = jnp.sqrt(-2.0 * t / (1.0 - t))
    return a, mult, gx


def _lru_seq_kernel(lru_ref, cw_ref, cb_ref, wa_ref, wx_ref, ba_ref, bx_ref, lam_ref, h0_ref, buf_ref,
                    o_ref, hl_ref, cs_ref, hc_ref, cbuf_ref, *, tm):
    i = pl.program_id(1)

    @pl.when(i == 0)
    def _():
        hc_ref[...] = jnp.broadcast_to(h0_ref[0], hc_ref.shape)
        cbuf_ref[...] = buf_ref[0]

    xg = lru_ref[0]
    x = xg[:, :256]
    gate = xg[:, 256:]
    prev8 = cbuf_ref[...]
    w = cw_ref[...]
    xa = (cb_ref[...] + x * w[3:4] + _shift_rows(x, 1, prev8) * w[2:3]
          + _shift_rows(x, 2, prev8) * w[1:2] + _shift_rows(x, 3, prev8) * w[0:1])
    cbuf_ref[...] = x[tm - 8:]
    cs_ref[0] = x[tm - 8:]

    a, mult, gx = _lru_gates(xa, gate, wa_ref, wx_ref, ba_ref, bx_ref, lam_ref)
    row = lax.broadcasted_iota(I32, (tm, 256), 0)
    mult = jnp.where((row + i * tm) == 0, 1.0, mult)
    u = mult * gx * xa
    s = 1
    while s < tm:
        keep = row >= s
        a_sh = jnp.where(keep, pltpu.roll(a, s, 0), 1.0)
        u_sh = jnp.where(keep, pltpu.roll(u, s, 0), 0.0)
        u = a * u_sh + u
        a = a * a_sh
        s *= 2
    h = a * hc_ref[0:1, :] + u
    hlast = h[tm - 1:tm]
    hc_ref[...] = jnp.broadcast_to(hlast, hc_ref.shape)
    hl_ref[0] = hlast
    o_ref[0] = h * jax.nn.gelu(gate, approximate=True)


def _lru_seq(lru, lw, l, h0, buf8, tm):
    G, T, _ = lru.shape
    wspec = lambda shape: pl.BlockSpec((None,) + shape, lambda b, i: (l,) + (0,) * len(shape))
    return pl.pallas_call(
        functools.partial(_lru_seq_kernel, tm=tm),
        grid=(G, T // tm),
        in_specs=[pl.BlockSpec((1, tm, 512), lambda b, i: (b, i, 0)),
                  wspec((LRU_CONV, 256)), wspec((1, 256)), wspec((256, 256)), wspec((256, 256)),
                  wspec((1, 256)), wspec((1, 256)), wspec((1, 256)),
                  pl.BlockSpec((1, 1, 256), lambda b, i: (b, 0, 0)),
                  pl.BlockSpec((1, 8, 256), lambda b, i: (b, 0, 0))],
        out_specs=[pl.BlockSpec((1, tm, 256), lambda b, i: (b, i, 0)),
                   pl.BlockSpec((1, 1, 256), lambda b, i: (b, 0, 0)),
                   pl.BlockSpec((1, 8, 256), lambda b, i: (b, 0, 0))],
        out_shape=[jax.ShapeDtypeStruct((G, T, 256), F32), jax.ShapeDtypeStruct((G, 1, 256), F32),
                   jax.ShapeDtypeStruct((G, 8, 256), F32)],
        scratch_shapes=[pltpu.VMEM((8, 256), F32), pltpu.VMEM((8, 256), F32)],
        compiler_params=_cparams("arbitrary", "arbitrary"),
        name="lru_seq",
    )(lru, lw["conv_w"], lw["conv_b"], lw["wa"], lw["wx"], lw["ba"], lw["bx"], lw["lam"], h0, buf8)


def _lru_rows_kernel(xx_ref, gate_ref, cw_ref, cb_ref, wa_ref, wx_ref, ba_ref, bx_ref, lam_ref, h0_ref,
                     o_ref, hl_ref, *, nb, nt):
    xx = xx_ref[...]
    w = cw_ref[...]
    n = nt * nb
    xa = cb_ref[...]
    for j in range(LRU_CONV):
        xa = xa + xx[j * nb:j * nb + n] * w[j:j + 1]
    gate = gate_ref[...]
    a, mult, gx = _lru_gates(xa, gate, wa_ref, wx_ref, ba_ref, bx_ref, lam_ref)
    u = mult * gx * xa
    h = h0_ref[...]
    for t in range(nt):
        h = a[t * nb:(t + 1) * nb] * h + u[t * nb:(t + 1) * nb]
        o_ref[t * nb:(t + 1) * nb, :] = h * jax.nn.gelu(gate[t * nb:(t + 1) * nb], approximate=True)
    hl_ref[...] = h


def _lru_rows(xx, gate, lw, l, h0, nb, nt):
    n = nt * nb
    full = lambda a: pl.BlockSpec(a.shape, lambda i: (0,) * a.ndim)
    wspec = lambda shape: pl.BlockSpec((None,) + shape, lambda i: (l,) + (0,) * len(shape))
    return pl.pallas_call(
        functools.partial(_lru_rows_kernel, nb=nb, nt=nt),
        grid=(1,),
        in_specs=[full(xx), full(gate), wspec((LRU_CONV, 256)), wspec((1, 256)), wspec((256, 256)),
                  wspec((256, 256)), wspec((1, 256)), wspec((1, 256)), wspec((1, 256)), full(h0)],
        out_specs=[pl.BlockSpec((n, 256), lambda i: (0, 0)), pl.BlockSpec((nb, 256), lambda i: (0, 0))],
        out_shape=[jax.ShapeDtypeStruct((n, 256), F32), jax.ShapeDtypeStruct((nb, 256), F32)],
        compiler_params=_cparams("arbitrary"),
        name="lru_rows",
    )(xx, gate, lw["conv_w"], lw["conv_b"], lw["wa"], lw["wx"], lw["ba"], lw["bx"], lw["lam"], h0)


def _diff_finalize(acc_ref, l_ref, lam, osc_ref, g_ref, ones_ref, out_scale, rows, wide):
    for h in range(DIFF_HEADS):
        r1 = slice((2 * h) * rows, (2 * h + 1) * rows)
        r2 = slice((2 * h + 1) * rows, (2 * h + 2) * rows)
        cs = slice(64 * h, 64 * h + 64) if wide else slice(0, 64)
        o1 = acc_ref[r1, cs] / l_ref[r1, :]
        o2 = acc_ref[r2, cs] / l_ref[r2, :]
        osc_ref[:, 64 * h:64 * h + 64] = o1 - lam * o2
    o = osc_ref[...]
    ms = _group_mean64(o * o, ones_ref[...])
    return o * lax.rsqrt(ms + EPS) * g_ref[...] * out_scale


def _softmax_steps(ss, sel, m_ref, l_ref, acc_ref, idxs, vTs, c):
    n = range(len(ss))
    m_prev = [m_ref[i:i + 1, :] for i in idxs]
    l_prev = [l_ref[i:i + 1, :] for i in idxs]
    if sel is not None:
        ss = [jnp.where(sel, s, NEG_BIG) for s in ss]
    m_new = [jnp.maximum(m_prev[a], jnp.max(ss[a], axis=0, keepdims=True)) for a in n]
    alpha = [jnp.exp2((m_prev[a] - m_new[a]) * c) for a in n]
    p = [jnp.exp2((ss[a] - m_new[a]) * c) for a in n]
    l_new = [alpha[a] * l_prev[a] + jnp.sum(p[a], axis=0, keepdims=True) for a in n]
    pv = [_dot(vTs[a], p[a].astype(BF16)) for a in n]
    acc_new = [alpha[a] * acc_ref[idxs[a]] + pv[a] for a in n]
    for a in n:
        i = idxs[a]
        m_ref[i:i + 1, :] = m_new[a]
        l_ref[i:i + 1, :] = l_new[a]
        acc_ref[i] = acc_new[a]


def _diff_attn_kernel(lam_ref, q_ref, k_ref, vT_ref, g_ref, o_ref,
                      qmT_ref, m_ref, l_ref, acc_ref, osc_ref, *, tq, out_scale):
    i = pl.program_id(1)
    qT = q_ref[0].T
    feat = lax.broadcasted_iota(I32, (256, tq), 0)
    for hm in range(8):
        qmT_ref[hm] = jnp.where((feat // DIFF_QK) == hm, qT, 0.0).astype(BF16)
    m_ref[...] = jnp.full(m_ref.shape, NEG_BIG, F32)
    l_ref[...] = jnp.zeros(l_ref.shape, F32)
    acc_ref[...] = jnp.zeros(acc_ref.shape, F32)
    krow = lax.broadcasted_iota(I32, (tq, tq), 0)
    qcol = lax.broadcasted_iota(I32, (tq, tq), 1)
    c = (DIFF_QK ** -0.5) * LOG2E

    def block(j, masked):
        r0 = pl.multiple_of(j * tq, tq)
        kj = k_ref[0, pl.ds(r0, tq), :].astype(BF16)
        vjT = vT_ref[0, :, pl.ds(r0, tq)].astype(BF16)
        for h0 in range(0, 8, HEADS_PER_STAGE):
            hms = list(range(h0, h0 + HEADS_PER_STAGE))
            ss = [_dot(kj, qmT_ref[hm]) for hm in hms]
            if masked:
                ss = [jnp.where(krow <= qcol, s, NEG_BIG) for s in ss]
            vTs = [vjT[64 * (hm // 2):64 * (hm // 2) + 64, :] for hm in hms]
            _softmax_steps(ss, None, m_ref, l_ref, acc_ref, hms, vTs, c)

    def loop_body(j, carry):
        block(j, False)
        return carry

    lax.fori_loop(0, i, loop_body, 0)
    block(i, True)
    lam = lam_ref[0, 0]
    for h in range(DIFF_HEADS):
        o = (acc_ref[2 * h] / l_ref[2 * h:2 * h + 1, :]
             - lam * (acc_ref[2 * h + 1] / l_ref[2 * h + 1:2 * h + 2, :]))
        ms = jnp.mean(o * o, axis=0, keepdims=True)
        osc_ref[64 * h:64 * h + 64, :] = o * lax.rsqrt(ms + EPS)
    o_ref[0] = osc_ref[...].T * g_ref[...] * out_scale


def _diff_attn(lam, q, k, vT, g_tiled, l, out_scale, tq):
    G, T, _ = q.shape
    return pl.pallas_call(
        functools.partial(_diff_attn_kernel, tq=tq, out_scale=out_scale),
        grid=(G, T // tq),
        in_specs=[pl.BlockSpec(memory_space=pltpu.SMEM),
                  pl.BlockSpec((1, tq, 256), lambda b, i: (b, i, 0)),
                  pl.BlockSpec((1, T, 256), lambda b, i: (b, 0, 0)),
                  pl.BlockSpec((1, 256, T), lambda b, i: (b, 0, 0)),
                  pl.BlockSpec((None, 1, 256), lambda b, i: (l, 0, 0))],
        out_specs=pl.BlockSpec((1, tq, 256), lambda b, i: (b, i, 0)),
        out_shape=jax.ShapeDtypeStruct((G, T, 256), F32),
        scratch_shapes=[pltpu.VMEM((8, 256, tq), BF16), pltpu.VMEM((8, tq), F32),
                        pltpu.VMEM((8, tq), F32), pltpu.VMEM((8, 64, tq), F32),
                        pltpu.VMEM((256, tq), F32)],
        compiler_params=_cparams("arbitrary", "arbitrary"),
        name="diff_attn",
    )(lam, q, k, vT, g_tiled)


def _topk_threshold(count_ge, count_tie_below, topk, n_col_bits, shape):
    kf = float(topk)
    zero = jnp.zeros(shape, I32)
    c0 = count_ge(jnp.zeros(shape, F32))
    tau = jnp.where(c0 >= kf, zero, zero + KEY_NEG_INF)
    cnt = c0

    def bit_body(b, carry):
        tau, cnt = carry
        cand = tau + lax.shift_left(jnp.int32(1), 30 - b)
        c = count_ge(_key_to_float(cand))
        up = c >= kf
        return jnp.where(up, cand, tau), jnp.where(up, c, cnt)

    tau, cnt = lax.fori_loop(0, 31, bit_body, (tau, cnt))
    t0 = _key_to_float(tau)
    t1 = _key_to_float(tau + 1)
    has_tie = jnp.where(tau > KEY_NEG_INF, jnp.where(cnt > kf, 1.0, 0.0), 0.0)

    def tie_search():
        need = kf - count_ge(t1)

        def tie_body(b, g):
            cand = g + lax.shift_left(jnp.int32(1), n_col_bits - 1 - b)
            return jnp.where(count_tie_below(t0, cand) < need, cand, g)
        return lax.fori_loop(0, n_col_bits, tie_body, zero)

    g_tie = lax.cond(jnp.max(has_tie) > 0.0, tie_search, lambda: zero)
    g = jnp.where(tau == KEY_NEG_INF, -1, jnp.where(has_tie > 0.0, g_tie, BIG_COL))
    return t0, t1, g


def _dsa_kernel(q_ref, k_ref, vT_ref, iq_ref, misc_ref, iqn_ref, miscn_ref, ikt_ref, o_ref,
                qmT_ref, iqmT_ref, scs_ref, m_ref, l_ref, acc_ref, osc_ref, *, tq, topk, n_col_bits):
    i = pl.program_id(1)
    nblk = i + 1
    sc_ref = scs_ref.at[i % 2]
    scn_ref = scs_ref.at[(i + 1) % 2]
    feat = lax.broadcasted_iota(I32, (256, tq), 0)
    krow = lax.broadcasted_iota(I32, (tq, tq), 0)
    qcol = lax.broadcasted_iota(I32, (tq, tq), 1)
    neg_inf_block = jnp.full((tq, tq), -jnp.inf, F32)

    def indexer_operands(iq_r, misc_r):
        iqT = iq_r[0].T
        for h in range(IDX_HEADS):
            iqmT_ref[h] = jnp.where((feat // IDX_DIM) == h, iqT, 0.0).astype(BF16)
        return misc_r[0].T[0:IDX_HEADS, :] * ((IDX_HEADS ** -0.5) * (IDX_DIM ** -0.5))

    def indexer_scores(scs, wT, masked):
        tot = wT[0:1, :] * jnp.maximum(scs[0], 0.0)
        for h in range(1, IDX_HEADS):
            tot = tot + wT[h:h + 1, :] * jnp.maximum(scs[h], 0.0)
        tot = jnp.where(tot == 0.0, 0.0, tot)
        return jnp.where(krow <= qcol, tot, -jnp.inf) if masked else tot

    def diag_scores(j, wT):
        ikj = ikt_ref[0, pl.ds(pl.multiple_of(j * tq, tq), tq), :].astype(BF16)
        return indexer_scores([_dot(ikj, iqmT_ref[h]) for h in range(IDX_HEADS)], wT, True)

    @pl.when(i == 0)
    def _():
        sc_ref[0] = diag_scores(0, indexer_operands(iq_ref, misc_ref))
        sc_ref[1] = neg_inf_block

    def count_rows(hit_fn):
        def body(jj, acc):
            a = jnp.sum(hit_fn(2 * jj).reshape(tq // 8, 8, tq), axis=0)
            b = jnp.sum(hit_fn(2 * jj + 1).reshape(tq // 8, 8, tq), axis=0)
            return acc + (a + b)
        acc = lax.fori_loop(0, (nblk + 1) // 2, body, jnp.zeros((8, tq), F32))
        return jnp.sum(acc, axis=0, keepdims=True)

    def count_ge(c):
        return count_rows(lambda j: jnp.where(sc_ref[j] >= c, 1.0, 0.0))

    def count_tie_below(t, c):
        return count_rows(lambda j: jnp.where(sc_ref[j] == t, jnp.where(krow + j * tq < c, 1.0, 0.0), 0.0))

    t0, t1, g = _topk_threshold(count_ge, count_tie_below, topk, n_col_bits, (1, tq))

    qT = q_ref[0].T
    for h in range(DSA_HEADS):
        qmT_ref[h] = jnp.where((feat // DSA_HD) == h, qT, 0.0).astype(BF16)
    wTn = indexer_operands(iqn_ref, miscn_ref)
    m_ref[...] = jnp.full(m_ref.shape, NEG_BIG, F32)
    l_ref[...] = jnp.zeros(l_ref.shape, F32)
    acc_ref[...] = jnp.zeros(acc_ref.shape, F32)
    c = (DSA_HD ** -0.5) * LOG2E
    hs = list(range(DSA_HEADS))

    def att_body(j, carry):
        r0 = pl.multiple_of(j * tq, tq)
        sel = sc_ref[j] >= jnp.where(krow + j * tq > g, t1, t0)
        kj = k_ref[0, pl.ds(r0, tq), :].astype(BF16)
        vjT = vT_ref[0, :, pl.ds(r0, tq)].astype(BF16)
        ikj = ikt_ref[0, pl.ds(r0, tq), :].astype(BF16)
        ss = [_dot(kj, qmT_ref[h]) for h in hs]
        scs = [_dot(ikj, iqmT_ref[h]) for h in range(IDX_HEADS)]
        _softmax_steps(ss, sel, m_ref, l_ref, acc_ref, hs, [vjT[64 * h:64 * h + 64, :] for h in hs], c)
        scn_ref[j] = indexer_scores(scs, wTn, False)
        return carry

    lax.fori_loop(0, nblk, att_body, 0)

    @pl.when(i + 1 < pl.num_programs(1))
    def _():
        scn_ref[nblk] = diag_scores(nblk, wTn)

        @pl.when(nblk % 2 == 0)
        def _():
            scn_ref[nblk + 1] = neg_inf_block

    for h in range(DSA_HEADS):
        osc_ref[64 * h:64 * h + 64, :] = acc_ref[h] / l_ref[h:h + 1, :]
    o_ref[0] = osc_ref[...].T


def _dsa_attn(q, k, vT, iq, ikt, misc, topk, tq):
    G, T, _ = q.shape
    nb = T // tq
    return pl.pallas_call(
        functools.partial(_dsa_kernel, tq=tq, topk=topk, n_col_bits=max(1, (T - 1).bit_length())),
        grid=(G, nb),
        in_specs=[pl.BlockSpec((1, tq, 256), lambda b, i: (b, i, 0)),
                  pl.BlockSpec((1, T, 256), lambda b, i: (b, 0, 0)),
                  pl.BlockSpec((1, 256, T), lambda b, i: (b, 0, 0)),
                  pl.BlockSpec((1, tq, 256), lambda b, i: (b, i, 0)),
                  pl.BlockSpec((1, tq, 128), lambda b, i: (b, i, 0)),
                  pl.BlockSpec((1, tq, 256), lambda b, i: (b, jnp.minimum(i + 1, nb - 1), 0)),
                  pl.BlockSpec((1, tq, 128), lambda b, i: (b, jnp.minimum(i + 1, nb - 1), 0)),
                  pl.BlockSpec((1, T, 256), lambda b, i: (b, 0, 0))],
        out_specs=pl.BlockSpec((1, tq, 256), lambda b, i: (b, i, 0)),
        out_shape=jax.ShapeDtypeStruct((G, T, 256), F32),
        scratch_shapes=[pltpu.VMEM((DSA_HEADS, 256, tq), BF16), pltpu.VMEM((IDX_HEADS, 256, tq), BF16),
                        pltpu.VMEM((2, nb + 1, tq, tq), F32), pltpu.VMEM((8, tq), F32),
                        pltpu.VMEM((8, tq), F32), pltpu.VMEM((DSA_HEADS, 64, tq), F32),
                        pltpu.VMEM((256, tq), F32)],
        compiler_params=_cparams("arbitrary", "arbitrary"),
        name="dsa_attn",
    )(q, k, vT, iq, misc, iq, misc, ikt)


def _gla_kernel(qk_ref, v_ref, g_ref, misc_ref, wa_ref, ba_ref, gn_ref, ones_ref, bd_ref, s0_ref,
                gsel_ref, negq_ref, negk_ref, pair_ref, o_ref, sfin_ref, s_ref, *, tm, chunk, n_valid):
    i = pl.program_id(1)

    @pl.when(i == 0)
    def _():
        s_ref[...] = jnp.zeros(s_ref.shape, F32)
        for h in range(GLA_HEADS):
            s_ref[32 * h:32 * h + 32, 64 * h:64 * h + 64] = s0_ref[0, h]

    crow = lax.broadcasted_iota(I32, (chunk, 128), 0)
    lane_h = lax.broadcasted_iota(I32, (chunk, 128), 1) // GLA_DK
    vlane_h = lax.broadcasted_iota(I32, (chunk, 256), 1) // GLA_DV
    n_levels = chunk.bit_length() - 1

    cs = range(tm // chunk)
    rows = [slice(c * chunk, (c + 1) * chunk) for c in cs]
    bd = bd_ref[...]
    bdb = bd.astype(BF16)
    a_lin = [_dot(misc_ref[0, r, :].astype(BF16), wa_ref[...]) + ba_ref[...] for r in rows]
    la = [(jnp.minimum(a, 0.0) - jnp.log1p(jnp.exp(-jnp.abs(a)))) * (1.0 / GLA_TAU) for a in a_lin]
    q = [qk_ref[0, r, 0:128] * (GLA_DK ** -0.5) for r in rows]
    k = [qk_ref[0, r, 128:256] for r in rows]
    if n_valid is not None:
        live = [(crow + c * chunk) < n_valid for c in cs]
        la = [jnp.where(live[c], la[c], 0.0) for c in cs]
        k = [jnp.where(live[c], k[c], 0.0) for c in cs]
    v = [v_ref[0, r, :] for r in rows]
    vb = [x.astype(BF16) for x in v]
    b = [_cumsum_rows(x) for x in la]
    b_hi = [x.astype(BF16) for x in b]
    r1 = [b[c] - b_hi[c].astype(F32) for c in cs]
    b_mid = [x.astype(BF16) for x in r1]
    b_lo = [(r1[c] - b_mid[c].astype(F32)).astype(BF16) for c in cs]
    gsel = gsel_ref[...]
    ref_rows = [_dot(gsel, b_hi[c]) + _dot(gsel, b_mid[c]) + _dot(gsel, b_lo[c]) for c in cs]
    att = [jnp.zeros((GLA_HEADS * chunk, chunk), F32) for _ in cs]
    for j in range(n_levels):
        rj = [ref_rows[c][j * chunk:(j + 1) * chunk] for c in cs]
        qt = [q[c] * jnp.exp((b[c] - rj[c]) + negq_ref[j]) for c in cs]
        kh = [k[c] * jnp.exp((rj[c] - b[c]) + negk_ref[j]) for c in cs]
        qs = [jnp.concatenate([jnp.where(lane_h == h, qt[c], 0.0) for h in range(GLA_HEADS)], axis=0) for c in cs]
        att = [att[c] + _dot_nt(qs[c].astype(BF16), kh[c].astype(BF16)) * pair_ref[j] for c in cs]
    x = [_dot(att[c].astype(BF16), vb[c]) for c in cs]
    o = [_dot((q[c] * k[c]).astype(BF16), bdb) * v[c] for c in cs]
    for h in range(GLA_HEADS):
        o = [o[c] + jnp.where(vlane_h == h, x[c][h * chunk:(h + 1) * chunk], 0.0) for c in cs]
    bT = [x.T for x in b]
    kT = [x.T for x in k]
    bendT = [x[:, chunk - 1:chunk] for x in bT]
    upd = [_dot((kT[c] * jnp.exp(bendT[c] - bT[c])).astype(BF16), vb[c]) * bd for c in cs]
    decay = [jnp.exp(x) for x in bendT]
    qe = [(q[c] * jnp.exp(b[c])).astype(BF16) for c in cs]
    S = s_ref[...]
    for c in cs:
        o[c] = o[c] + _dot(qe[c], S.astype(BF16))
        S = decay[c] * S + upd[c]
    s_ref[...] = S
    ms = [_group_mean64(x * x, ones_ref[...]) for x in o]
    for c in cs:
        o_ref[0, rows[c], :] = o[c] * lax.rsqrt(ms[c] + EPS) * gn_ref[...] * _silu(g_ref[0, rows[c], :])

    @pl.when(i == pl.num_programs(1) - 1)
    def _():
        for h in range(GLA_HEADS):
            sfin_ref[0, h] = s_ref[32 * h:32 * h + 32, 64 * h:64 * h + 64]


def _gla_tables(chunk):
    t = jnp.arange(chunk)
    gsel, negq, negk, pair = [], [], [], []
    m = chunk // 2
    while m >= 1:
        blk, second = t // (2 * m), (t % (2 * m)) >= m
        gsel.append((t[None, :] == (blk * 2 * m + m - 1)[:, None]).astype(BF16))
        negq.append(jnp.broadcast_to(jnp.where(second, 0.0, -jnp.inf)[:, None], (chunk, 128)))
        negk.append(jnp.broadcast_to(jnp.where(second, -jnp.inf, 0.0)[:, None], (chunk, 128)))
        pair.append(jnp.tile((blk[:, None] == blk[None, :]).astype(F32), (GLA_HEADS, 1)))
        m //= 2
    return (jnp.concatenate(gsel, axis=0), jnp.stack(negq).astype(F32), jnp.stack(negk).astype(F32),
            jnp.stack(pair))


def _gla(gqk, gv, gg, misc, gw, l, s0, ones_bd, bd_mask, tm, chunk, n_valid):
    G, T, _ = gqk.shape
    wspec = lambda shape: pl.BlockSpec((None,) + shape, lambda b, i: (l,) + (0,) * len(shape))
    tables = _gla_tables(chunk)
    const = lambda a: pl.BlockSpec(a.shape, lambda b, i: (0,) * a.ndim)
    return pl.pallas_call(
        functools.partial(_gla_kernel, tm=tm, chunk=chunk, n_valid=n_valid),
        grid=(G, T // tm),
        in_specs=[pl.BlockSpec((1, tm, 256), lambda b, i: (b, i, 0)),
                  pl.BlockSpec((1, tm, 256), lambda b, i: (b, i, 0)),
                  pl.BlockSpec((1, tm, 256), lambda b, i: (b, i, 0)),
                  pl.BlockSpec((1, tm, 128), lambda b, i: (b, i, 0)),
                  wspec((128, 128)), wspec((1, 128)), wspec((1, 256)),
                  pl.BlockSpec((256, 256), lambda b, i: (0, 0)),
                  pl.BlockSpec((128, 256), lambda b, i: (0, 0)),
                  pl.BlockSpec((1, GLA_HEADS, GLA_DK, GLA_DV), lambda b, i: (b, 0, 0, 0))]
                 + [const(a) for a in tables],
        out_specs=[pl.BlockSpec((1, tm, 256), lambda b, i: (b, i, 0)),
                   pl.BlockSpec((1, GLA_HEADS, GLA_DK, GLA_DV), lambda b, i: (b, 0, 0, 0))],
        out_shape=[jax.ShapeDtypeStruct((G, T, 256), F32),
                   jax.ShapeDtypeStruct((G, GLA_HEADS, GLA_DK, GLA_DV), F32)],
        scratch_shapes=[pltpu.VMEM((128, 256), F32)],
        compiler_params=_cparams("arbitrary", "arbitrary"),
        name="gla",
    )(gqk, gv, gg, misc, gw["wa"], gw["ba"], gw["gn"], ones_bd, bd_mask, s0, *tables)


def _ffn_kernel(*refs, tm, seq, final, nb):
    (x_ref, oa_ref, ob_ref, oc_ref, od_ref, g1_ref, sh2_ref, sc2_ref, g2_ref, n2_ref,
     wout_ref, wg_ref, wu_ref, wd_ref, cw_ref, cb_ref, buf_ref, fng_ref, y_ref, cs_ref) = refs[:20]
    x = x_ref[0]
    mix = _dot(oa_ref[0].astype(BF16), wout_ref[0:256, :])
    mix = mix + _dot(ob_ref[0].astype(BF16), wout_ref[256:512, :])
    mix = mix + _dot(oc_ref[0].astype(BF16), wout_ref[512:768, :])
    mix = mix + _dot(od_ref[0].astype(BF16), wout_ref[768:1024, :])
    x1 = x + g1_ref[0] * mix
    h2 = _rms(x1, n2_ref[...]) * (1.0 + sc2_ref[0]) + sh2_ref[0]
    hb = h2.astype(BF16)
    gate = _dot(hb, wg_ref[...])
    w = cw_ref[...]
    if seq:
        cbuf_ref = refs[20]

        @pl.when(pl.program_id(1) == 0)
        def _():
            cbuf_ref[...] = buf_ref[0]

        prev8 = cbuf_ref[...]
        z = cb_ref[...] + gate * w[2:3] + _shift_rows(gate, 1, prev8) * w[1:2] + _shift_rows(gate, 2, prev8) * w[0:1]
        cbuf_ref[...] = gate[tm - 8:]
        cs_ref[0] = gate[tm - 8:]
    else:
        xx = jnp.concatenate([buf_ref[0], gate], axis=0)
        z = cb_ref[...] + xx[0:tm] * w[0:1] + xx[nb:nb + tm] * w[1:2] + xx[2 * nb:2 * nb + tm] * w[2:3]
        cs_ref[0] = xx[tm:tm + 2 * nb]
    up = _dot(hb, wu_ref[...])
    y = _dot((_silu(z) * up).astype(BF16), wd_ref[...])
    x2 = x1 + g2_ref[0] * y
    if final:
        x2 = _rms(x2, fng_ref[...])
    y_ref[0] = x2


def _ffn(x, o4, mods, fw, l, buf, final_g, tm, seq, final, nb=0):
    G, T, D = x.shape
    per_row = mods[0].shape[1] != 1
    mod_spec = (pl.BlockSpec((1, tm, D), lambda b, i: (b, i, 0)) if per_row
                else pl.BlockSpec((1, 1, D), lambda b, i: (b, 0, 0)))
    o_spec = pl.BlockSpec((1, tm, 256), lambda b, i: (b, i, 0))
    once = pl.Buffered(1)
    wspec = lambda shape: pl.BlockSpec((None,) + shape, lambda b, i: (l,) + (0,) * len(shape), pipeline_mode=once)
    nbuf = buf.shape[1]
    return pl.pallas_call(
        functools.partial(_ffn_kernel, tm=tm, seq=seq, final=final, nb=nb),
        grid=(G, T // tm),
        in_specs=[pl.BlockSpec((1, tm, D), lambda b, i: (b, i, 0)), o_spec, o_spec, o_spec, o_spec,
                  mod_spec, mod_spec, mod_spec, mod_spec,
                  wspec((1, D)), wspec((D, D)), wspec((D, D_FF)), wspec((D, D_FF)), wspec((D_FF, D)),
                  wspec((FFN_CONV, D_FF)), wspec((1, D_FF)),
                  pl.BlockSpec((1, nbuf, D_FF), lambda b, i: (b, 0, 0)),
                  pl.BlockSpec((1, D), lambda b, i: (0, 0))],
        out_specs=[pl.BlockSpec((1, tm, D), lambda b, i: (b, i, 0)),
                   pl.BlockSpec((1, nbuf, D_FF), lambda b, i: (b, 0, 0))],
        out_shape=[jax.ShapeDtypeStruct((G, T, D), F32), jax.ShapeDtypeStruct((G, nbuf, D_FF), F32)],
        scratch_shapes=[pltpu.VMEM((8, D_FF), F32)] if seq else [],
        compiler_params=_cparams("arbitrary", "arbitrary"),
        name="out_ffn",
    )(x, *o4, *mods, fw["n2"], fw["wout"], fw["wg"], fw["wu"], fw["wd"], fw["cw"], fw["cb"], buf, final_g)


def _diff_dec_kernel(pt_ref, lam_ref, q_ref, *refs, n_new, out_scale, nps):
    kT_refs, vT_refs = refs[:nps], refs[nps:2 * nps]
    kn_ref, vnT_ref, g_ref, ones_ref, o_ref, qs_ref, m_ref, l_ref, acc_ref, osc_ref = refs[2 * nps:]
    p = pl.program_id(1)
    lane = lax.broadcasted_iota(I32, (8, 256), 1)

    @pl.when(p == 0)
    def _():
        q = q_ref[0]
        for hm in range(8):
            qs_ref[8 * hm:8 * hm + 8, :] = jnp.where((lane // 32) == hm, q, 0.0).astype(BF16)
        m_ref[...] = jnp.full(m_ref.shape, NEG_BIG, F32)
        l_ref[...] = jnp.zeros(l_ref.shape, F32)
        acc_ref[...] = jnp.zeros(acc_ref.shape, F32)

    def update(s, v_contract):
        m_prev = m_ref[...]
        m_new = jnp.maximum(m_prev, jnp.max(s, axis=1, keepdims=True))
        alpha = jnp.exp(m_prev - m_new)
        pr = jnp.exp(s - m_new)
        l_ref[...] = alpha * l_ref[...] + jnp.sum(pr, axis=1, keepdims=True)
        acc_ref[...] = alpha * acc_ref[...] + v_contract(pr.astype(BF16))
        m_ref[...] = m_new

    qs = qs_ref[...]
    kcat = jnp.concatenate([r[0, 0].astype(BF16) for r in kT_refs], axis=1)
    vcat = jnp.concatenate([r[0, 0].astype(BF16) for r in vT_refs], axis=1)
    update(_dot(qs, kcat) * (DIFF_QK ** -0.5), lambda pr: _dot_nt(pr, vcat))

    @pl.when(p == pl.num_programs(1) - 1)
    def _():
        s = _dot_nt(qs, kn_ref[0].astype(BF16)) * (DIFF_QK ** -0.5)
        t = lax.broadcasted_iota(I32, s.shape, 0) % 8
        c = lax.broadcasted_iota(I32, s.shape, 1)
        s = jnp.where((c <= t) & (c < n_new), s, NEG_BIG)
        vnT = vnT_ref[0].astype(BF16)
        update(s, lambda pr: _dot_nt(pr, vnT))
        o_ref[0] = _diff_finalize(acc_ref, l_ref, lam_ref[0, 0], osc_ref, g_ref, ones_ref, out_scale, 8, True)


def _page_specs(l, nps, rows):
    return [pl.BlockSpec((1, 1, rows, PAGE), lambda b, p, pt, i=i: (l, pt[b, p * nps + i], 0, 0))
            for i in range(nps)]


def _diff_dec(page_table, lam, q8, ckT, cvT, kn, vnT, g_tiled, ones_bd, l, out_scale, n_new):
    B, NP = page_table.shape
    nps = math.gcd(NP, PAGES_PER_STEP)
    gs = pltpu.PrefetchScalarGridSpec(
        num_scalar_prefetch=1, grid=(B, NP // nps),
        in_specs=[pl.BlockSpec(memory_space=pltpu.SMEM),
                  pl.BlockSpec((1, 8, 256), lambda b, p, pt: (b, 0, 0))]
                 + _page_specs(l, nps, 256) + _page_specs(l, nps, 256)
                 + [pl.BlockSpec((1, 128, 256), lambda b, p, pt: (b, 0, 0)),
                    pl.BlockSpec((1, 256, 128), lambda b, p, pt: (b, 0, 0)),
                    pl.BlockSpec((None, 1, 256), lambda b, p, pt: (l, 0, 0)),
                    pl.BlockSpec((256, 256), lambda b, p, pt: (0, 0))],
        out_specs=pl.BlockSpec((1, 8, 256), lambda b, p, pt: (b, 0, 0)),
        scratch_shapes=[pltpu.VMEM((64, 256), BF16), pltpu.VMEM((64, 1), F32), pltpu.VMEM((64, 1), F32),
                        pltpu.VMEM((64, 256), F32), pltpu.VMEM((8, 256), F32)])
    return pl.pallas_call(
        functools.partial(_diff_dec_kernel, n_new=n_new, out_scale=out_scale, nps=nps),
        grid_spec=gs, out_shape=jax.ShapeDtypeStruct((B, 8, 256), F32),
        compiler_params=_cparams("arbitrary", "arbitrary"),
        name="diff_decode",
    )(page_table, lam, q8, *([ckT] * nps), *([cvT] * nps), kn, vnT, g_tiled, ones_bd)


def _idx_dec_kernel(pt_ref, iq_ref, w_ref, *refs, n_new, nps):
    ikT_refs = refs[:nps]
    iknT_ref, keys_ref, keysn_ref = refs[nps:]
    p = pl.program_id(1)
    iq = iq_ref[0].astype(BF16)
    w = w_ref[0]

    def scores(ikT):
        sc = _dot(iq, ikT.astype(BF16)) * (IDX_DIM ** -0.5)
        tot = jnp.sum((w * jnp.maximum(sc, 0.0)).reshape(8, 8, PAGE), axis=1)
        return jnp.where(tot == 0.0, 0.0, tot)

    for i in range(nps):
        keys_ref[0, i] = scores(ikT_refs[i][0, 0])

    @pl.when(p == pl.num_programs(1) - 1)
    def _():
        tot = scores(iknT_ref[0])
        t = lax.broadcasted_iota(I32, tot.shape, 0)
        c = lax.broadcasted_iota(I32, tot.shape, 1)
        keysn_ref[0] = jnp.where((c <= t) & (c < n_new), tot, -jnp.inf)


def _idx_dec(page_table, iq64, w64, cikT, iknT, l, n_new):
    B, NP = page_table.shape
    nps = math.gcd(NP, PAGES_PER_STEP)
    gs = pltpu.PrefetchScalarGridSpec(
        num_scalar_prefetch=1, grid=(B, NP // nps),
        in_specs=[pl.BlockSpec((1, 64, IDX_DIM), lambda b, p, pt: (b, 0, 0)),
                  pl.BlockSpec((1, 64, 1), lambda b, p, pt: (b, 0, 0))]
                 + _page_specs(l, nps, IDX_DIM)
                 + [pl.BlockSpec((1, IDX_DIM, 128), lambda b, p, pt: (b, 0, 0))],
        out_specs=[pl.BlockSpec((1, nps, 8, PAGE), lambda b, p, pt: (b, p, 0, 0)),
                   pl.BlockSpec((1, 8, 128), lambda b, p, pt: (b, 0, 0))])
    return pl.pallas_call(
        functools.partial(_idx_dec_kernel, n_new=n_new, nps=nps),
        grid_spec=gs,
        out_shape=[jax.ShapeDtypeStruct((B, NP, 8, PAGE), F32), jax.ShapeDtypeStruct((B, 8, 128), F32)],
        compiler_params=_cparams("arbitrary", "arbitrary"),
        name="idx_decode",
    )(page_table, iq64, w64, *([cikT] * nps), iknT)


def _dec_thr_kernel(keys_ref, keysn_ref, thr_ref, g_ref, *, topk, n_pages, n_col_bits):
    nb = keys_ref.shape[0]
    new_col = lax.broadcasted_iota(I32, (nb, 8, PAGE), 2) + n_pages * PAGE

    def count_ge(c):
        acc = (jnp.sum(jnp.where(keys_ref[...] >= c[:, None], 1.0, 0.0), axis=1)
               + jnp.where(keysn_ref[...] >= c, 1.0, 0.0))
        return jnp.sum(acc, axis=2, keepdims=True)

    def count_tie_below(tau, c):
        page_col = (lax.broadcasted_iota(I32, (nb, n_pages, 8, PAGE), 1) * PAGE
                    + lax.broadcasted_iota(I32, (nb, n_pages, 8, PAGE), 3))
        hit = jnp.where(keys_ref[...] == tau[:, None], jnp.where(page_col < c[:, None], 1.0, 0.0), 0.0)
        acc = (jnp.sum(hit, axis=1)
               + jnp.where(keysn_ref[...] == tau, jnp.where(new_col < c, 1.0, 0.0), 0.0))
        return jnp.sum(acc, axis=2, keepdims=True)

    t0, t1, g = _topk_threshold(count_ge, count_tie_below, topk, n_col_bits, (nb, 8, 1))
    thr_ref[...] = jnp.zeros(thr_ref.shape, F32)
    thr_ref[:, :, 0:1] = t0
    thr_ref[:, :, 1:2] = t1
    g_ref[...] = jnp.broadcast_to(g, g_ref.shape)


def _dec_thr(keys, keysn, topk):
    B, NP = keys.shape[:2]
    nb = math.gcd(B, 8)
    n_keys = NP * PAGE + 128
    return pl.pallas_call(
        functools.partial(_dec_thr_kernel, topk=topk, n_pages=NP, n_col_bits=(n_keys - 1).bit_length()),
        grid=(B // nb,),
        in_specs=[pl.BlockSpec((nb, NP, 8, PAGE), lambda b: (b, 0, 0, 0)),
                  pl.BlockSpec((nb, 8, 128), lambda b: (b, 0, 0))],
        out_specs=[pl.BlockSpec((nb, 8, 128), lambda b: (b, 0, 0)), pl.BlockSpec((nb, 8, 128), lambda b: (b, 0, 0))],
        out_shape=[jax.ShapeDtypeStruct((B, 8, 128), F32), jax.ShapeDtypeStruct((B, 8, 128), I32)],
        compiler_params=_cparams("arbitrary"),
        name="decode_threshold",
    )(keys, keysn)


def _dsa_dec_kernel(pt_ref, q_ref, keys_ref, keysn_ref, thr_ref, g_ref, *refs, n_pages, nps):
    kT_refs, vT_refs = refs[:nps], refs[nps:2 * nps]
    kn_ref, vnT_ref, o_ref, qs_ref, m_ref, l_ref, acc_ref = refs[2 * nps:]
    p = pl.program_id(1)
    lane = lax.broadcasted_iota(I32, (8, 256), 1)
    colp = lax.broadcasted_iota(I32, (8, PAGE), 1)

    @pl.when(p == 0)
    def _():
        q = q_ref[0]
        for h in range(DSA_HEADS):
            qs_ref[8 * h:8 * h + 8, :] = jnp.where((lane // 64) == h, q, 0.0).astype(BF16)
        m_ref[...] = jnp.full(m_ref.shape, NEG_BIG, F32)
        l_ref[...] = jnp.zeros(l_ref.shape, F32)
        acc_ref[...] = jnp.zeros(acc_ref.shape, F32)

    t0 = thr_ref[0, :, 0:1]
    t1 = thr_ref[0, :, 1:2]
    g = g_ref[0, :, 0:1]

    def update(s, key8, thr8, v_contract):
        sel = jnp.concatenate([key8] * DSA_HEADS, axis=0) >= jnp.concatenate([thr8] * DSA_HEADS, axis=0)
        m_prev = m_ref[...]
        m_new = jnp.maximum(m_prev, jnp.max(jnp.where(sel, s, NEG_BIG), axis=1, keepdims=True))
        alpha = jnp.exp(m_prev - m_new)
        pr = jnp.where(sel, jnp.exp(s - m_new), 0.0)
        l_ref[...] = alpha * l_ref[...] + jnp.sum(pr, axis=1, keepdims=True)
        acc_ref[...] = alpha * acc_ref[...] + v_contract(pr.astype(BF16))
        m_ref[...] = m_new

    qs = qs_ref[...]
    kcat = jnp.concatenate([r[0, 0].astype(BF16) for r in kT_refs], axis=1)
    vcat = jnp.concatenate([r[0, 0].astype(BF16) for r in vT_refs], axis=1)
    s = _dot(qs, kcat) * (DSA_HD ** -0.5)
    key8 = jnp.concatenate([keys_ref[0, p * nps + i] for i in range(nps)], axis=1)
    col = lax.broadcasted_iota(I32, key8.shape, 1) + p * (nps * PAGE)
    thr8 = jnp.where(col > g, t1, t0)

    update(s, key8, thr8, lambda pr: _dot_nt(pr, vcat))

    @pl.when(p == pl.num_programs(1) - 1)
    def _():
        thrn = jnp.where(colp + n_pages * PAGE > g, t1, t0)
        s = _dot_nt(qs, kn_ref[0].astype(BF16)) * (DSA_HD ** -0.5)
        vnT = vnT_ref[0].astype(BF16)
        update(s, keysn_ref[0], thrn, lambda pr: _dot_nt(pr, vnT))
        for h in range(DSA_HEADS):
            o_ref[0, :, 64 * h:64 * h + 64] = (acc_ref[8 * h:8 * h + 8, 64 * h:64 * h + 64]
                                               / l_ref[8 * h:8 * h + 8, :])


def _dsa_dec(page_table, q8, keys, keysn, ckT, cvT, kn, vnT, l, topk):
    B, NP = page_table.shape
    nps = math.gcd(NP, PAGES_PER_STEP)
    thr, gcol = _dec_thr(keys, keysn, topk)
    gs = pltpu.PrefetchScalarGridSpec(
        num_scalar_prefetch=1, grid=(B, NP // nps),
        in_specs=[pl.BlockSpec((1, 8, 256), lambda b, p, pt: (b, 0, 0)),
                  pl.BlockSpec((1, NP, 8, PAGE), lambda b, p, pt: (b, 0, 0, 0)),
                  pl.BlockSpec((1, 8, 128), lambda b, p, pt: (b, 0, 0)),
                  pl.BlockSpec((1, 8, 128), lambda b, p, pt: (b, 0, 0)),
                  pl.BlockSpec((1, 8, 128), lambda b, p, pt: (b, 0, 0))]
                 + _page_specs(l, nps, 256) + _page_specs(l, nps, 256)
                 + [pl.BlockSpec((1, 128, 256), lambda b, p, pt: (b, 0, 0)),
                    pl.BlockSpec((1, 256, 128), lambda b, p, pt: (b, 0, 0))],
        out_specs=pl.BlockSpec((1, 8, 256), lambda b, p, pt: (b, 0, 0)),
        scratch_shapes=[pltpu.VMEM((32, 256), BF16), pltpu.VMEM((32, 1), F32),
                        pltpu.VMEM((32, 1), F32), pltpu.VMEM((32, 256), F32)])
    return pl.pallas_call(
        functools.partial(_dsa_dec_kernel, n_pages=NP, nps=nps),
        grid_spec=gs, out_shape=jax.ShapeDtypeStruct((B, 8, 256), F32),
        compiler_params=_cparams("arbitrary", "arbitrary"),
        name="dsa_decode",
    )(page_table, q8, keys, keysn, thr, gcol, *([ckT] * nps), *([cvT] * nps), kn, vnT)


def _rope_tables(pos, half):
    inv = ROPE_THETA ** (-jnp.arange(half, dtype=F32) / half)
    ang = pos.astype(F32)[:, None] * inv[None, :]
    cos, sin = jnp.cos(ang), jnp.sin(ang)
    reps = 128 // (2 * half)
    return (jnp.tile(jnp.concatenate([cos, cos], axis=1), (1, reps)),
            jnp.tile(jnp.concatenate([-sin, sin], axis=1), (1, reps)))


def _pack_w_in(w_in):
    names = (("lru_x", 256), ("lru_gate", 256), ("diff_q", 256), ("diff_k", 256), ("diff_v", 256),
             ("dsa_q", 256), ("dsa_k", 256), ("dsa_v", 256), ("idx_q", 256), ("idx_k", 32), ("idx_w", 8),
             ("gla_q", 128), ("gla_k", 128), ("gla_v", 256), ("gla_g", 256), ("gla_a", 16))
    seg, off = {}, 0
    for n, w in names:
        seg[n] = w_in[..., off:off + w]
        off += w
    pad = jnp.zeros(w_in.shape[:-1] + (128 - 24,), w_in.dtype)
    cols = [seg["lru_x"], seg["lru_gate"], seg["diff_q"], seg["diff_k"],
            seg["dsa_q"], seg["dsa_k"], seg["idx_q"]] + [seg["idx_k"]] * IDX_HEADS + [
            seg["gla_q"], seg["gla_k"], seg["gla_v"], seg["gla_g"], seg["idx_w"], seg["gla_a"], pad]
    wvT = jnp.concatenate([seg["diff_v"], seg["dsa_v"]], axis=-1).transpose(0, 2, 1)
    return jnp.concatenate(cols, axis=-1).astype(BF16), wvT.astype(BF16)


def _block_diag(w):
    depth, nb, bw, _ = w.shape
    eye = jnp.eye(nb, dtype=w.dtype)
    return jnp.einsum("lnij,nm->lnimj", w, eye).reshape(depth, nb * bw, nb * bw)


def _group_ones(n, group):
    r = jnp.arange(n) // group
    return (r[:, None] == r[None, :]).astype(BF16)


def _prompt_layer(x, mod, l, W, tabs, final):
    G, T, _ = x.shape
    tm, tq = min(SEQ_TILE, T), min(ATTN_TILE, T)
    td = min(DENSE_TILE, T)
    assert T % tm == 0 and T % tq == 0 and T % td == 0, "sequence length must be a multiple of every tile"
    sh1, sc1, g1, sh2, sc2, g2 = [m[:, None, :] for m in jnp.split(mod, 6, axis=-1)]
    (lru, qb, kb, vbT, qc, kc, vcT, iq, ikt, gqk, gv, gg, misc) = _proj(
        x, sh1, sc1, W["norm1"], W["w_in"], W["wvT"], l, tabs, td)
    zeros = lambda *s: jnp.zeros(s, F32)
    o_a, h_last, cs8 = _lru_seq(lru, W["lru"], l, zeros(G, 1, 256), zeros(G, 8, 256), tm)
    o_b = _diff_attn(W["lam"][l], qb, kb, vbT, W["subln"], l, W["out_scale"][l], tq)
    o_c = _dsa_attn(qc, kc, vcT, iq, ikt, misc, min(TOPK_MAX, T // 4), tq)
    from_T = lambda a: a.reshape(G, 4, 64, T).transpose(0, 3, 1, 2)
    o_d, s_last = _gla(gqk, gv, gg, misc, W["gla"], l, zeros(G, GLA_HEADS, GLA_DK, GLA_DV),
                       W["ones64"], W["bd_mask"], tm, GLA_CHUNK, None)
    y, fcs8 = _ffn(x, (o_a, o_b, o_c, o_d), (g1, sh2, sc2, g2), W["ffn"], l, zeros(G, 8, D_FF),
                   W["final_g"], td, True, final)
    new = dict(
        diff_k=kb.reshape(G, T, DIFF_HEADS, 2, DIFF_QK), diff_v=from_T(vbT),
        dsa_k=kc.reshape(G, T, DSA_HEADS, DSA_HD), dsa_v=from_T(vcT),
        idx_k=ikt[:, :, :IDX_DIM], lru_h=h_last[:, 0], lru_conv=cs8[:, 8 - (LRU_CONV - 1):],
        gla=s_last, ffn_conv=fcs8[:, 8 - (FFN_CONV - 1):])
    return y, new


def _pad_rows(a, n):
    return jnp.pad(a, ((0, 0), (0, n - a.shape[1])) + ((0, 0),) * (a.ndim - 2))


def _sample_layer(xr, mod, l, W, tabs, caches, states, page_table, final, B, Tn):
    n = Tn * B
    tm_major = lambda a: jnp.tile(a, (Tn, 1))[None]
    sh1, sc1, g1, sh2, sc2, g2 = [tm_major(m) for m in jnp.split(mod, 6, axis=-1)]
    outs = _proj(xr, sh1, sc1, W["norm1"], W["w_in"], W["wvT"], l, tabs, n)
    (lru, qb, kb, vbT, qc, kc, vcT, iq, ikt, gqk, gv, gg, misc) = [o[0] for o in outs]
    to_b = lambda a: a.reshape(Tn, B, -1).transpose(1, 0, 2)
    to_t = lambda a: a.transpose(1, 0, 2).reshape(n, -1)
    to_bT = lambda a: a.reshape(256, Tn, B).transpose(2, 0, 1)
    pad_lanes = lambda a: jnp.pad(a, ((0, 0), (0, 0), (0, 128 - a.shape[2])))

    conv_prev = states["lru_conv"][l].transpose(1, 0, 2).reshape((LRU_CONV - 1) * B, 256)
    xx = jnp.concatenate([conv_prev, lru[:, :256]], axis=0)
    o_a, h_last = _lru_rows(xx, lru[:, 256:], W["lru"], l, states["lru_h"][l], B, Tn)
    new_conv = xx[Tn * B:].reshape(LRU_CONV - 1, B, 256).transpose(1, 0, 2)

    kb_b, vb_bT = to_b(kb), to_bT(vbT)
    o_b = _diff_dec(page_table, W["lam"][l], _pad_rows(to_b(qb), 8), caches["diff_kT"], caches["diff_vT"],
                    _pad_rows(kb_b, 128), pad_lanes(vb_bT), W["subln"], W["ones64"], l,
                    W["out_scale"][l], Tn)[:, :Tn]

    past_len = page_table.shape[1] * PAGE
    topk = min(TOPK_MAX, (past_len + Tn) // 4)
    kc_b, vc_bT = to_b(kc), to_bT(vcT)
    ik_b = to_b(ikt)[:, :, :IDX_DIM]
    iq64 = _pad_rows(to_b(iq), 8).reshape(B, 8 * IDX_HEADS, IDX_DIM)
    w64 = (_pad_rows(to_b(misc)[:, :, :IDX_HEADS], 8) * (IDX_HEADS ** -0.5)).reshape(B, 8 * IDX_HEADS, 1)
    iknT = _pad_rows(ik_b, 128).transpose(0, 2, 1)
    keys, keysn = _idx_dec(page_table, iq64, w64, caches["idx_kT"], iknT, l, Tn)
    o_c = _dsa_dec(page_table, _pad_rows(to_b(qc), 8), keys, keysn, caches["dsa_kT"], caches["dsa_vT"],
                   _pad_rows(kc_b, 128), pad_lanes(vc_bT), l, topk)[:, :Tn]

    padc = lambda a: _pad_rows(to_b(a), GLA_CHUNK)
    o_d, s_last = _gla(padc(gqk), padc(gv), padc(gg), padc(misc), W["gla"], l, states["gla"][l],
                       W["ones64"], W["bd_mask"], GLA_CHUNK, GLA_CHUNK, Tn)
    o_d = o_d[:, :Tn]

    ffn_prev = states["ffn_conv"][l].transpose(1, 0, 2).reshape(1, (FFN_CONV - 1) * B, D_FF)
    o4 = (o_a[None], to_t(o_b)[None], to_t(o_c)[None], to_t(o_d)[None])
    y, fcs = _ffn(xr, o4, (g1, sh2, sc2, g2), W["ffn"], l, ffn_prev, W["final_g"], n, False, final, nb=B)
    new = dict(
        diff_k=kb_b.reshape(B, Tn, DIFF_HEADS, 2, DIFF_QK),
        diff_v=vb_bT.transpose(0, 2, 1).reshape(B, Tn, DIFF_HEADS, 64),
        dsa_k=kc_b.reshape(B, Tn, DSA_HEADS, DSA_HD),
        dsa_v=vc_bT.transpose(0, 2, 1).reshape(B, Tn, DSA_HEADS, DSA_HD),
        idx_k=ik_b, lru_h=h_last, lru_conv=new_conv, gla=s_last,
        ffn_conv=fcs[0].reshape(FFN_CONV - 1, B, D_FF).transpose(1, 0, 2))
    return y, new


def kernel(x_prompt, x_sample, cache_diff_k, cache_diff_v, cache_dsa_k, cache_dsa_v, cache_idx_k, state_lru_h, state_lru_conv, state_gla, state_ffn_conv, page_table, c_prompt, c_sample, w_ada, b_ada, norm1_g, w_in, lru_conv_w, lru_conv_b, lru_wa, lru_ba, lru_wx, lru_bx, lru_lambda, diff_lq1, diff_lk1, diff_lq2, diff_lk2, diff_subln_g, gla_wa2, gla_ba, gla_norm_g, w_out, norm2_g, ffn_w_gate, ffn_w_up, ffn_conv_w, ffn_conv_b, ffn_w_down, final_norm_g):
    depth = w_in.shape[0]
    Bp, T, D = x_prompt.shape
    Bs, Tn, _ = x_sample.shape
    n_pool = cache_diff_k.shape[1]
    past_len = page_table.shape[1] * PAGE

    lam_init = jnp.asarray([0.8 - 0.6 * math.exp(-0.3 * l) for l in range(depth)], F32)
    lam = (jnp.exp(jnp.sum(diff_lq1 * diff_lk1, axis=-1)) - jnp.exp(jnp.sum(diff_lq2 * diff_lk2, axis=-1))
           + lam_init).reshape(depth, 1, 1)
    row3 = lambda a: a.reshape(depth, 1, a.shape[-1])
    wa2_pad = jnp.zeros((depth, 128, 128), F32).at[:, IDX_HEADS:IDX_HEADS + gla_wa2.shape[1], :].set(gla_wa2)
    w_packed, wvT = _pack_w_in(w_in)
    W = dict(
        norm1=row3(norm1_g), w_in=w_packed, wvT=wvT,
        lru=dict(conv_w=lru_conv_w, conv_b=row3(lru_conv_b), wa=_block_diag(lru_wa).astype(BF16),
                 wx=_block_diag(lru_wx).astype(BF16), ba=row3(lru_ba), bx=row3(lru_bx), lam=row3(lru_lambda)),
        lam=lam, out_scale=[1.0 - (0.8 - 0.6 * math.exp(-0.3 * l)) for l in range(depth)],
        subln=row3(jnp.tile(diff_subln_g, (1, DIFF_HEADS))), ones64=_group_ones(256, 64),
        bd_mask=(jnp.arange(128)[:, None] // GLA_DK == jnp.arange(256)[None, :] // GLA_DV).astype(F32),
        gla=dict(wa=wa2_pad.astype(BF16), ba=row3(gla_ba), gn=row3(jnp.tile(gla_norm_g, (1, GLA_HEADS)))),
        ffn=dict(n2=row3(norm2_g), wout=w_out.astype(BF16), wg=ffn_w_gate.astype(BF16),
                 wu=ffn_w_up.astype(BF16), wd=ffn_w_down.astype(BF16), cw=ffn_conv_w, cb=row3(ffn_conv_b)),
        final_g=final_norm_g.reshape(1, D))

    mods = _ada(jnp.concatenate([c_prompt, c_sample], axis=0), w_ada, b_ada)

    pos_p = jnp.arange(T, dtype=I32)
    pos_s = jnp.repeat(past_len + jnp.arange(Tn, dtype=I32), Bs)
    tabs_p = _rope_tables(pos_p, 16) + _rope_tables(pos_p, 32)
    tabs_s = _rope_tables(pos_s, 16) + _rope_tables(pos_s, 32)

    caches = dict(
        diff_kT=cache_diff_k.transpose(0, 1, 3, 4, 5, 2).reshape(depth, n_pool, 256, PAGE),
        diff_vT=cache_diff_v.transpose(0, 1, 3, 4, 2).reshape(depth, n_pool, 256, PAGE),
        dsa_kT=cache_dsa_k.transpose(0, 1, 3, 4, 2).reshape(depth, n_pool, 256, PAGE),
        dsa_vT=cache_dsa_v.transpose(0, 1, 3, 4, 2).reshape(depth, n_pool, 256, PAGE),
        idx_kT=cache_idx_k.transpose(0, 1, 3, 2))
    states = dict(lru_h=state_lru_h, lru_conv=state_lru_conv, gla=state_gla, ffn_conv=state_ffn_conv)

    xp = x_prompt
    xs = x_sample.transpose(1, 0, 2).reshape(1, Tn * Bs, D)
    st_p, st_s = [], []
    for l in range(depth):
        final = l == depth - 1
        xp, new_p = _prompt_layer(xp, mods[l, :Bp], l, W, tabs_p, final)
        xs, new_s = _sample_layer(xs, mods[l, Bp:], l, W, tabs_s, caches, states, page_table, final, Bs, Tn)
        st_p.append(new_p)
        st_s.append(new_s)
    y_prompt = xp
    y_sample = xs.reshape(Tn, Bs, D).transpose(1, 0, 2)
    keys = ("diff_k", "diff_v", "dsa_k", "dsa_v", "idx_k", "lru_h", "lru_conv", "gla", "ffn_conv")
    P = [jnp.stack([s[k] for s in st_p]) for k in keys]
    S = [jnp.stack([s[k] for s in st_s]) for k in keys]
    return (y_prompt, y_sample, *P, *S)
```

```python
import functools
import math

import jax
import jax.numpy as jnp
from jax import lax
from jax.experimental import pallas as pl
from jax.experimental.pallas import tpu as pltpu

F32 = jnp.float32
BF16 = jnp.bfloat16
I32 = jnp.int32

D_MODEL = 1024
GROUP_W = 256
LRU_CONV = 4
LRU_C = 8.0
DIFF_HEADS = 4
DIFF_QK = 32
DSA_HEADS = 4
DSA_HD = 64
IDX_HEADS = 8
IDX_DIM = 32
TOPK_MAX = 256
GLA_HEADS = 4
GLA_DK = 32
GLA_DV = 64
GLA_TAU = 16.0
GLA_CHUNK = 64
D_FF = 2816
FFN_CONV = 3
ROPE_THETA = 10000.0
EPS = 1e-6
PAGE = 128

P_LRU, P_DIFF_QK, P_DSA_QK, P_IDX, P_GLA_QK, P_GLA_V, P_GLA_G, P_MISC, P_END = (
    0, 512, 1024, 1536, 2048, 2304, 2560, 2816, 2944)
SEQ_TILE = 256
ATTN_TILE = 256
DENSE_TILE = 512
PAGES_PER_STEP = 64
HEADS_PER_STAGE = 8

VMEM_LIMIT_BYTES = 56 * 1024 * 1024
NEG_BIG = -1e30
LOG2E = 1.4426950408889634
KEY_NEG_INF = -2 ** 31 + 0x7FFFFF
BIG_COL = 2 ** 30


def _cparams(*sem):
    return pltpu.CompilerParams(dimension_semantics=sem, vmem_limit_bytes=VMEM_LIMIT_BYTES)


def _rms(x, g):
    return x * lax.rsqrt(jnp.mean(x * x, axis=-1, keepdims=True) + EPS) * g


def _silu(x):
    return x * jax.nn.sigmoid(x)


def _softplus(z):
    return jnp.maximum(z, 0.0) + jnp.log1p(jnp.exp(-jnp.abs(z)))


def _dot(a, b):
    return jnp.dot(a, b, preferred_element_type=F32)


def _dot_nt(a, b):
    return lax.dot_general(a, b, (((1,), (1,)), ((), ())), preferred_element_type=F32)


def _group_mean64(x2, ones_bd):
    hi = x2.astype(BF16)
    lo = (x2 - hi.astype(F32)).astype(BF16)
    return (_dot(hi, ones_bd) + _dot(lo, ones_bd)) * (1.0 / 64.0)


def _rope(x, cos, sin_signed, half):
    n = x.shape[1]
    reps = n // 128
    c = jnp.concatenate([cos] * reps, axis=1) if reps > 1 else cos
    s = jnp.concatenate([sin_signed] * reps, axis=1) if reps > 1 else sin_signed
    lane = lax.broadcasted_iota(I32, x.shape, 1)
    first = (lane % (2 * half)) < half
    swapped = jnp.where(first, pltpu.roll(x, n - half, 1), pltpu.roll(x, half, 1))
    return x * c + swapped * s


def _shift_rows(x, s, prev8):
    r = pltpu.roll(x, s, 0)
    p = pltpu.roll(prev8, s, 0)
    row = lax.broadcasted_iota(I32, (8, x.shape[1]), 0)
    top = jnp.where(row < s, p, r[:8])
    return jnp.concatenate([top, r[8:]], axis=0)


def _cumsum_rows(x):
    n = x.shape[0]
    row = lax.broadcasted_iota(I32, x.shape, 0)
    s = 1
    while s < n:
        x = x + jnp.where(row >= s, pltpu.roll(x, s, 0), 0.0)
        s *= 2
    return x


def _key_to_float(key):
    return pltpu.bitcast(key ^ ((key >> 31) & 0x7FFFFFFF), F32)


def _ada_kernel(c_ref, w_ref, b_ref, o_ref):
    c = c_ref[...]
    o_ref[...] = _dot(_silu(c).astype(BF16), w_ref[...].astype(BF16)) + b_ref[...]


def _ada(c_all, w_ada, b_ada):
    depth = w_ada.shape[0]
    n = c_all.shape[0]
    tn = 1024
    return pl.pallas_call(
        _ada_kernel,
        grid=(depth, 6 * D_MODEL // tn),
        in_specs=[pl.BlockSpec((n, D_MODEL), lambda l, j: (0, 0)),
                  pl.BlockSpec((None, D_MODEL, tn), lambda l, j: (l, 0, j)),
                  pl.BlockSpec((None, 1, tn), lambda l, j: (l, 0, j))],
        out_specs=pl.BlockSpec((None, n, tn), lambda l, j: (l, 0, j)),
        out_shape=jax.ShapeDtypeStruct((depth, n, 6 * D_MODEL), F32),
        compiler_params=_cparams("arbitrary", "arbitrary"),
        name="ada_mod",
    )(c_all, w_ada, b_ada.reshape(depth, 1, 6 * D_MODEL))


def _proj_kernel(x_ref, sh_ref, sc_ref, g_ref, w_ref, wvT_ref, c32_ref, s32_ref, c64_ref, s64_ref,
                 lru_ref, qb_ref, kb_ref, vbT_ref, qc_ref, kc_ref, vcT_ref, iq_ref, ikt_ref,
                 gqk_ref, gv_ref, gg_ref, misc_ref):
    x = x_ref[0]
    h = _rms(x, g_ref[...]) * (1.0 + sc_ref[0]) + sh_ref[0]
    hb = h.astype(BF16)

    def mm(a, b):
        return _dot(hb, w_ref[:, a:b])

    lru_ref[0] = mm(P_LRU, P_DIFF_QK)
    c32, s32, c64, s64 = c32_ref[...], s32_ref[...], c64_ref[...], s64_ref[...]
    qk = _rope(mm(P_DIFF_QK, P_DSA_QK), c32, s32, 16)
    qb_ref[0] = qk[:, :256]
    kb_ref[0] = qk[:, 256:]
    qk = _rope(mm(P_DSA_QK, P_IDX), c64, s64, 32)
    qc_ref[0] = qk[:, :256]
    kc_ref[0] = qk[:, 256:]
    qk = _rope(mm(P_IDX, P_GLA_QK), c32, s32, 16)
    iq_ref[0] = qk[:, :256]
    ikt_ref[0] = qk[:, 256:]
    gqk_ref[0] = mm(P_GLA_QK, P_GLA_V)
    gv_ref[0] = mm(P_GLA_V, P_GLA_G)
    gg_ref[0] = mm(P_GLA_G, P_MISC)
    misc_ref[0] = mm(P_MISC, P_END)
    vbT_ref[0] = _dot_nt(wvT_ref[0:256, :], hb)
    vcT_ref[0] = _dot_nt(wvT_ref[256:512, :], hb)


_PROJ_WIDTHS = (512, 256, 256, None, 256, 256, None, 256, 256, 256, 256, 256, 128)


def _proj(x, sh, sc, norm_g, w_packed, wvT, l, tabs, tm):
    G, T, D = x.shape
    per_row = sh.shape[1] != 1
    mod_spec = (pl.BlockSpec((1, tm, D), lambda b, i: (b, i, 0)) if per_row
                else pl.BlockSpec((1, 1, D), lambda b, i: (b, 0, 0)))
    tab_spec = pl.BlockSpec((tm, 128), lambda b, i: (i, 0))
    out_specs = [pl.BlockSpec((1, 256, tm), lambda b, i: (b, 0, i)) if w is None
                 else pl.BlockSpec((1, tm, w), lambda b, i: (b, i, 0)) for w in _PROJ_WIDTHS]
    out_shape = [jax.ShapeDtypeStruct((G, 256, T) if w is None else (G, T, w), F32) for w in _PROJ_WIDTHS]
    return pl.pallas_call(
        _proj_kernel,
        grid=(G, T // tm),
        in_specs=[pl.BlockSpec((1, tm, D), lambda b, i: (b, i, 0)), mod_spec, mod_spec,
                  pl.BlockSpec((None, 1, D), lambda b, i: (l, 0, 0)),
                  pl.BlockSpec((None, D, P_END), lambda b, i: (l, 0, 0)),
                  pl.BlockSpec((None, 512, D), lambda b, i: (l, 0, 0)),
                  tab_spec, tab_spec, tab_spec, tab_spec],
        out_specs=out_specs, out_shape=out_shape,
        compiler_params=_cparams("arbitrary", "arbitrary"),
        name="in_proj",
    )(x, sh, sc, norm_g, w_packed, wvT, *tabs)


def _lru_gates(xa, gate_in, wa_ref, wx_ref, ba_ref, bx_ref, lam_ref):
    xb = xa.astype(BF16)
    ga = jax.nn.sigmoid(_dot(xb, wa_ref[...]) + ba_ref[...])
    gx = jax.nn.sigmoid(_dot(xb, wx_ref[...]) + bx_ref[...])
    log_a = (-LRU_C) * ga * _softplus(-lam_ref[...])
    a = jnp.exp(log_a)
    t = jnp.tanh(log_a)
    mult = jnp.sqrt(-2.0 * t / (1.0 - t))
    return a, mult, gx


def _lru_seq_kernel(lru_ref, cw_ref, cb_ref, wa_ref, wx_ref, ba_ref, bx_ref, lam_ref, h0_ref, buf_ref,
                    o_ref, hl_ref, cs_ref, hc_ref, cbuf_ref, *, tm):
    i = pl.program_id(1)

    @pl.when(i == 0)
    def _():
        hc_ref[...] = jnp.broadcast_to(h0_ref[0], hc_ref.shape)
        cbuf_ref[...] = buf_ref[0]

    xg = lru_ref[0]
    x = xg[:, :256]
    gate = xg[:, 256:]
    prev8 = cbuf_ref[...]
    w = cw_ref[...]
    xa = (cb_ref[...] + x * w[3:4] + _shift_rows(x, 1, prev8) * w[2:3]
          + _shift_rows(x, 2, prev8) * w[1:2] + _shift_rows(x, 3, prev8) * w[0:1])
    cbuf_ref[...] = x[tm - 8:]
    cs_ref[0] = x[tm - 8:]

    a, mult, gx = _lru_gates(xa, gate, wa_ref, wx_ref, ba_ref, bx_ref, lam_ref)
    row = lax.broadcasted_iota(I32, (tm, 256), 0)
    mult = jnp.where((row + i * tm) == 0, 1.0, mult)
    u = mult * gx * xa
    s = 1
    while s < tm:
        keep = row >= s
        a_sh = jnp.where(keep, pltpu.roll(a, s, 0), 1.0)
        u_sh = jnp.where(keep, pltpu.roll(u, s, 0), 0.0)
        u = a * u_sh + u
        a = a * a_sh
        s *= 2
    h = a * hc_ref[0:1, :] + u
    hlast = h[tm - 1:tm]
    hc_ref[...] = jnp.broadcast_to(hlast, hc_ref.shape)
    hl_ref[0] = hlast
    o_ref[0] = h * jax.nn.gelu(gate, approximate=True)


def _lru_seq(lru, lw, l, h0, buf8, tm):
    G, T, _ = lru.shape
    wspec = lambda shape: pl.BlockSpec((None,) + shape, lambda b, i: (l,) + (0,) * len(shape))
    return pl.pallas_call(
        functools.partial(_lru_seq_kernel, tm=tm),
        grid=(G, T // tm),
        in_specs=[pl.BlockSpec((1, tm, 512), lambda b, i: (b, i, 0)),
                  wspec((LRU_CONV, 256)), wspec((1, 256)), wspec((256, 256)), wspec((256, 256)),
                  wspec((1, 256)), wspec((1, 256)), wspec((1, 256)),
                  pl.BlockSpec((1, 1, 256), lambda b, i: (b, 0, 0)),
                  pl.BlockSpec((1, 8, 256), lambda b, i: (b, 0, 0))],
        out_specs=[pl.BlockSpec((1, tm, 256), lambda b, i: (b, i, 0)),
                   pl.BlockSpec((1, 1, 256), lambda b, i: (b, 0, 0)),
                   pl.BlockSpec((1, 8, 256), lambda b, i: (b, 0, 0))],
        out_shape=[jax.ShapeDtypeStruct((G, T, 256), F32), jax.ShapeDtypeStruct((G, 1, 256), F32),
                   jax.ShapeDtypeStruct((G, 8, 256), F32)],
        scratch_shapes=[pltpu.VMEM((8, 256), F32), pltpu.VMEM((8, 256), F32)],
        compiler_params=_cparams("arbitrary", "arbitrary"),
        name="lru_seq",
    )(lru, lw["conv_w"], lw["conv_b"], lw["wa"], lw["wx"], lw["ba"], lw["bx"], lw["lam"], h0, buf8)


def _lru_rows_kernel(xx_ref, gate_ref, cw_ref, cb_ref, wa_ref, wx_ref, ba_ref, bx_ref, lam_ref, h0_ref,
                     o_ref, hl_ref, *, nb, nt):
    xx = xx_ref[...]
    w = cw_ref[...]
    n = nt * nb
    xa = cb_ref[...]
    for j in range(LRU_CONV):
        xa = xa + xx[j * nb:j * nb + n] * w[j:j + 1]
    gate = gate_ref[...]
    a, mult, gx = _lru_gates(xa, gate, wa_ref, wx_ref, ba_ref, bx_ref, lam_ref)
    u = mult * gx * xa
    h = h0_ref[...]
    for t in range(nt):
        h = a[t * nb:(t + 1) * nb] * h + u[t * nb:(t + 1) * nb]
        o_ref[t * nb:(t + 1) * nb, :] = h * jax.nn.gelu(gate[t * nb:(t + 1) * nb], approximate=True)
    hl_ref[...] = h


def _lru_rows(xx, gate, lw, l, h0, nb, nt):
    n = nt * nb
    full = lambda a: pl.BlockSpec(a.shape, lambda i: (0,) * a.ndim)
    wspec = lambda shape: pl.BlockSpec((None,) + shape, lambda i: (l,) + (0,) * len(shape))
    return pl.pallas_call(
        functools.partial(_lru_rows_kernel, nb=nb, nt=nt),
        grid=(1,),
        in_specs=[full(xx), full(gate), wspec((LRU_CONV, 256)), wspec((1, 256)), wspec((256, 256)),
                  wspec((256, 256)), wspec((1, 256)), wspec((1, 256)), wspec((1, 256)), full(h0)],
        out_specs=[pl.BlockSpec((n, 256), lambda i: (0, 0)), pl.BlockSpec((nb, 256), lambda i: (0, 0))],
        out_shape=[jax.ShapeDtypeStruct((n, 256), F32), jax.ShapeDtypeStruct((nb, 256), F32)],
        compiler_params=_cparams("arbitrary"),
        name="lru_rows",
    )(xx, gate, lw["conv_w"], lw["conv_b"], lw["wa"], lw["wx"], lw["ba"], lw["bx"], lw["lam"], h0)


def _diff_finalize(acc_ref, l_ref, lam, osc_ref, g_ref, ones_ref, out_scale, rows, wide):
    for h in range(DIFF_HEADS):
        r1 = slice((2 * h) * rows, (2 * h + 1) * rows)
        r2 = slice((2 * h + 1) * rows, (2 * h + 2) * rows)
        cs = slice(64 * h, 64 * h + 64) if wide else slice(0, 64)
        o1 = acc_ref[r1, cs] / l_ref[r1, :]
        o2 = acc_ref[r2, cs] / l_ref[r2, :]
        osc_ref[:, 64 * h:64 * h + 64] = o1 - lam * o2
    o = osc_ref[...]
    ms = _group_mean64(o * o, ones_ref[...])
    return o * lax.rsqrt(ms + EPS) * g_ref[...] * out_scale


def _softmax_steps(ss, sel, m_ref, l_ref, acc_ref, idxs, vTs, c):
    n = range(len(ss))
    m_prev = [m_ref[i:i + 1, :] for i in idxs]
    l_prev = [l_ref[i:i + 1, :] for i in idxs]
    if sel is not None:
        ss = [jnp.where(sel, s, NEG_BIG) for s in ss]
    m_new = [jnp.maximum(m_prev[a], jnp.max(ss[a], axis=0, keepdims=True)) for a in n]
    alpha = [jnp.exp2((m_prev[a] - m_new[a]) * c) for a in n]
    p = [jnp.exp2((ss[a] - m_new[a]) * c) for a in n]
    l_new = [alpha[a] * l_prev[a] + jnp.sum(p[a], axis=0, keepdims=True) for a in n]
    pv = [_dot(vTs[a], p[a].astype(BF16)) for a in n]
    acc_new = [alpha[a] * acc_ref[idxs[a]] + pv[a] for a in n]
    for a in n:
        i = idxs[a]
        m_ref[i:i + 1, :] = m_new[a]
        l_ref[i:i + 1, :] = l_new[a]
        acc_ref[i] = acc_new[a]


def _diff_attn_kernel(lam_ref, q_ref, k_ref, vT_ref, g_ref, o_ref,
                      qmT_ref, m_ref, l_ref, acc_ref, osc_ref, *, tq, out_scale):
    i = pl.program_id(1)
    qT = q_ref[0].T
    feat = lax.broadcasted_iota(I32, (256, tq), 0)
    for hm in range(8):
        qmT_ref[hm] = jnp.where((feat // DIFF_QK) == hm, qT, 0.0).astype(BF16)
    m_ref[...] = jnp.full(m_ref.shape, NEG_BIG, F32)
    l_ref[...] = jnp.zeros(l_ref.shape, F32)
    acc_ref[...] = jnp.zeros(acc_ref.shape, F32)
    krow = lax.broadcasted_iota(I32, (tq, tq), 0)
    qcol = lax.broadcasted_iota(I32, (tq, tq), 1)
    c = (DIFF_QK ** -0.5) * LOG2E

    def block(j, masked):
        r0 = pl.multiple_of(j * tq, tq)
        kj = k_ref[0, pl.ds(r0, tq), :].astype(BF16)
        vjT = vT_ref[0, :, pl.ds(r0, tq)].astype(BF16)
        for h0 in range(0, 8, HEADS_PER_STAGE):
            hms = list(range(h0, h0 + HEADS_PER_STAGE))
            ss = [_dot(kj, qmT_ref[hm]) for hm in hms]
            if masked:
                ss = [jnp.where(krow <= qcol, s, NEG_BIG) for s in ss]
            vTs = [vjT[64 * (hm // 2):64 * (hm // 2) + 64, :] for hm in hms]
            _softmax_steps(ss, None, m_ref, l_ref, acc_ref, hms, vTs, c)

    def loop_body(j, carry):
        block(j, False)
        return carry

    lax.fori_loop(0, i, loop_body, 0)
    block(i, True)
    lam = lam_ref[0, 0]
    for h in range(DIFF_HEADS):
        o = (acc_ref[2 * h] / l_ref[2 * h:2 * h + 1, :]
             - lam * (acc_ref[2 * h + 1] / l_ref[2 * h + 1:2 * h + 2, :]))
        ms = jnp.mean(o * o, axis=0, keepdims=True)
        osc_ref[64 * h:64 * h + 64, :] = o * lax.rsqrt(ms + EPS)
    o_ref[0] = osc_ref[...].T * g_ref[...] * out_scale


def _diff_attn(lam, q, k, vT, g_tiled, l, out_scale, tq):
    G, T, _ = q.shape
    return pl.pallas_call(
        functools.partial(_diff_attn_kernel, tq=tq, out_scale=out_scale),
        grid=(G, T // tq),
        in_specs=[pl.BlockSpec(memory_space=pltpu.SMEM),
                  pl.BlockSpec((1, tq, 256), lambda b, i: (b, i, 0)),
                  pl.BlockSpec((1, T, 256), lambda b, i: (b, 0, 0)),
                  pl.BlockSpec((1, 256, T), lambda b, i: (b, 0, 0)),
                  pl.BlockSpec((None, 1, 256), lambda b, i: (l, 0, 0))],
        out_specs=pl.BlockSpec((1, tq, 256), lambda b, i: (b, i, 0)),
        out_shape=jax.ShapeDtypeStruct((G, T, 256), F32),
        scratch_shapes=[pltpu.VMEM((8, 256, tq), BF16), pltpu.VMEM((8, tq), F32),
                        pltpu.VMEM((8, tq), F32), pltpu.VMEM((8, 64, tq), F32),
                        pltpu.VMEM((256, tq), F32)],
        compiler_params=_cparams("arbitrary", "arbitrary"),
        name="diff_attn",
    )(lam, q, k, vT, g_tiled)


def _topk_threshold(count_ge, count_tie_below, topk, n_col_bits, shape):
    kf = float(topk)
    zero = jnp.zeros(shape, I32)
    c0 = count_ge(jnp.zeros(shape, F32))
    tau = jnp.where(c0 >= kf, zero, zero + KEY_NEG_INF)
    cnt = c0

    def bit_body(b, carry):
        tau, cnt = carry
        cand = tau + lax.shift_left(jnp.int32(1), 30 - b)
        c = count_ge(_key_to_float(cand))
        up = c >= kf
        return jnp.where(up, cand, tau), jnp.where(up, c, cnt)

    tau, cnt = lax.fori_loop(0, 31, bit_body, (tau, cnt))
    t0 = _key_to_float(tau)
    t1 = _key_to_float(tau + 1)
    has_tie = jnp.where(tau > KEY_NEG_INF, jnp.where(cnt > kf, 1.0, 0.0), 0.0)

    def tie_search():
        need = kf - count_ge(t1)

        def tie_body(b, g):
            cand = g + lax.shift_left(jnp.int32(1), n_col_bits - 1 - b)
            return jnp.where(count_tie_below(t0, cand) < need, cand, g)
        return lax.fori_loop(0, n_col_bits, tie_body, zero)

    g_tie = lax.cond(jnp.max(has_tie) > 0.0, tie_search, lambda: zero)
    g = jnp.where(tau == KEY_NEG_INF, -1, jnp.where(has_tie > 0.0, g_tie, BIG_COL))
    return t0, t1, g


def _dsa_kernel(q_ref, k_ref, vT_ref, iq_ref, misc_ref, iqn_ref, miscn_ref, ikt_ref, o_ref,
                qmT_ref, iqmT_ref, scs_ref, m_ref, l_ref, acc_ref, osc_ref, *, tq, topk, n_col_bits):
    i = pl.program_id(1)
    nblk = i + 1
    sc_ref = scs_ref.at[i % 2]
    scn_ref = scs_ref.at[(i + 1) % 2]
    feat = lax.broadcasted_iota(I32, (256, tq), 0)
    krow = lax.broadcasted_iota(I32, (tq, tq), 0)
    qcol = lax.broadcasted_iota(I32, (tq, tq), 1)
    neg_inf_block = jnp.full((tq, tq), -jnp.inf, F32)

    def indexer_operands(iq_r, misc_r):
        iqT = iq_r[0].T
        for h in range(IDX_HEADS):
            iqmT_ref[h] = jnp.where((feat // IDX_DIM) == h, iqT, 0.0).astype(BF16)
        return misc_r[0].T[0:IDX_HEADS, :] * ((IDX_HEADS ** -0.5) * (IDX_DIM ** -0.5))

    def indexer_scores(scs, wT, masked):
        tot = wT[0:1, :] * jnp.maximum(scs[0], 0.0)
        for h in range(1, IDX_HEADS):
            tot = tot + wT[h:h + 1, :] * jnp.maximum(scs[h], 0.0)
        tot = jnp.where(tot == 0.0, 0.0, tot)
        return jnp.where(krow <= qcol, tot, -jnp.inf) if masked else tot

    def diag_scores(j, wT):
        ikj = ikt_ref[0, pl.ds(pl.multiple_of(j * tq, tq), tq), :].astype(BF16)
        return indexer_scores([_dot(ikj, iqmT_ref[h]) for h in range(IDX_HEADS)], wT, True)

    @pl.when(i == 0)
    def _():
        sc_ref[0] = diag_scores(0, indexer_operands(iq_ref, misc_ref))
        sc_ref[1] = neg_inf_block

    def count_rows(hit_fn):
        def body(jj, acc):
            a = jnp.sum(hit_fn(2 * jj).reshape(tq // 8, 8, tq), axis=0)
            b = jnp.sum(hit_fn(2 * jj + 1).reshape(tq // 8, 8, tq), axis=0)
            return acc + (a + b)
        acc = lax.fori_loop(0, (nblk + 1) // 2, body, jnp.zeros((8, tq), F32))
        return jnp.sum(acc, axis=0, keepdims=True)

    def count_ge(c):
        return count_rows(lambda j: jnp.where(sc_ref[j] >= c, 1.0, 0.0))

    def count_tie_below(t, c):
        return count_rows(lambda j: jnp.where(sc_ref[j] == t, jnp.where(krow + j * tq < c, 1.0, 0.0), 0.0))

    t0, t1, g = _topk_threshold(count_ge, count_tie_below, topk, n_col_bits, (1, tq))

    qT = q_ref[0].T
    for h in range(DSA_HEADS):
        qmT_ref[h] = jnp.where((feat // DSA_HD) == h, qT, 0.0).astype(BF16)
    wTn = indexer_operands(iqn_ref, miscn_ref)
    m_ref[...] = jnp.full(m_ref.shape, NEG_BIG, F32)
    l_ref[...] = jnp.zeros(l_ref.shape, F32)
    acc_ref[...] = jnp.zeros(acc_ref.shape, F32)
    c = (DSA_HD ** -0.5) * LOG2E
    hs = list(range(DSA_HEADS))

    def att_body(j, carry):
        r0 = pl.multiple_of(j * tq, tq)
        sel = sc_ref[j] >= jnp.where(krow + j * tq > g, t1, t0)
        kj = k_ref[0, pl.ds(r0, tq), :].astype(BF16)
        vjT = vT_ref[0, :, pl.ds(r0, tq)].astype(BF16)
        ikj = ikt_ref[0, pl.ds(r0, tq), :].astype(BF16)
        ss = [_dot(kj, qmT_ref[h]) for h in hs]
        scs = [_dot(ikj, iqmT_ref[h]) for h in range(IDX_HEADS)]
        _softmax_steps(ss, sel, m_ref, l_ref, acc_ref, hs, [vjT[64 * h:64 * h + 64, :] for h in hs], c)
        scn_ref[j] = indexer_scores(scs, wTn, False)
        return carry

    lax.fori_loop(0, nblk, att_body, 0)

    @pl.when(i + 1 < pl.num_programs(1))
    def _():
        scn_ref[nblk] = diag_scores(nblk, wTn)

        @pl.when(nblk % 2 == 0)
        def _():
            scn_ref[nblk + 1] = neg_inf_block

    for h in range(DSA_HEADS):
        osc_ref[64 * h:64 * h + 64, :] = acc_ref[h] / l_ref[h:h + 1, :]
    o_ref[0] = osc_ref[...].T


def _dsa_attn(q, k, vT, iq, ikt, misc, topk, tq):
    G, T, _ = q.shape
    nb = T // tq
    return pl.pallas_call(
        functools.partial(_dsa_kernel, tq=tq, topk=topk, n_col_bits=max(1, (T - 1).bit_length())),
        grid=(G, nb),
        in_specs=[pl.BlockSpec((1, tq, 256), lambda b, i: (b, i, 0)),
                  pl.BlockSpec((1, T, 256), lambda b, i: (b, 0, 0)),
                  pl.BlockSpec((1, 256, T), lambda b, i: (b, 0, 0)),
                  pl.BlockSpec((1, tq, 256), lambda b, i: (b, i, 0)),
                  pl.BlockSpec((1, tq, 128), lambda b, i: (b, i, 0)),
                  pl.BlockSpec((1, tq, 256), lambda b, i: (b, jnp.minimum(i + 1, nb - 1), 0)),
                  pl.BlockSpec((1, tq, 128), lambda b, i: (b, jnp.minimum(i + 1, nb - 1), 0)),
                  pl.BlockSpec((1, T, 256), lambda b, i: (b, 0, 0))],
        out_specs=pl.BlockSpec((1, tq, 256), lambda b, i: (b, i, 0)),
        out_shape=jax.ShapeDtypeStruct((G, T, 256), F32),
        scratch_shapes=[pltpu.VMEM((DSA_HEADS, 256, tq), BF16), pltpu.VMEM((IDX_HEADS, 256, tq), BF16),
                        pltpu.VMEM((2, nb + 1, tq, tq), F32), pltpu.VMEM((8, tq), F32),
                        pltpu.VMEM((8, tq), F32), pltpu.VMEM((DSA_HEADS, 64, tq), F32),
                        pltpu.VMEM((256, tq), F32)],
        compiler_params=_cparams("arbitrary", "arbitrary"),
        name="dsa_attn",
    )(q, k, vT, iq, misc, iq, misc, ikt)


def _gla_kernel(qk_ref, v_ref, g_ref, misc_ref, wa_ref, ba_ref, gn_ref, ones_ref, bd_ref, s0_ref,
                gsel_ref, negq_ref, negk_ref, pair_ref, o_ref, sfin_ref, s_ref, *, tm, chunk, n_valid):
    i = pl.program_id(1)

    @pl.when(i == 0)
    def _():
        s_ref[...] = jnp.zeros(s_ref.shape, F32)
        for h in range(GLA_HEADS):
            s_ref[32 * h:32 * h + 32, 64 * h:64 * h + 64] = s0_ref[0, h]

    crow = lax.broadcasted_iota(I32, (chunk, 128), 0)
    lane_h = lax.broadcasted_iota(I32, (chunk, 128), 1) // GLA_DK
    vlane_h = lax.broadcasted_iota(I32, (chunk, 256), 1) // GLA_DV
    n_levels = chunk.bit_length() - 1

    cs = range(tm // chunk)
    rows = [slice(c * chunk, (c + 1) * chunk) for c in cs]
    bd = bd_ref[...]
    bdb = bd.astype(BF16)
    a_lin = [_dot(misc_ref[0, r, :].astype(BF16), wa_ref[...]) + ba_ref[...] for r in rows]
    la = [(jnp.minimum(a, 0.0) - jnp.log1p(jnp.exp(-jnp.abs(a)))) * (1.0 / GLA_TAU) for a in a_lin]
    q = [qk_ref[0, r, 0:128] * (GLA_DK ** -0.5) for r in rows]
    k = [qk_ref[0, r, 128:256] for r in rows]
    if n_valid is not None:
        live = [(crow + c * chunk) < n_valid for c in cs]
        la = [jnp.where(live[c], la[c], 0.0) for c in cs]
        k = [jnp.where(live[c], k[c], 0.0) for c in cs]
    v = [v_ref[0, r, :] for r in rows]
    vb = [x.astype(BF16) for x in v]
    b = [_cumsum_rows(x) for x in la]
    b_hi = [x.astype(BF16) for x in b]
    r1 = [b[c] - b_hi[c].astype(F32) for c in cs]
    b_mid = [x.astype(BF16) for x in r1]
    b_lo = [(r1[c] - b_mid[c].astype(F32)).astype(BF16) for c in cs]
    gsel = gsel_ref[...]
    ref_rows = [_dot(gsel, b_hi[c]) + _dot(gsel, b_mid[c]) + _dot(gsel, b_lo[c]) for c in cs]
    att = [jnp.zeros((GLA_HEADS * chunk, chunk), F32) for _ in cs]
    for j in range(n_levels):
        rj = [ref_rows[c][j * chunk:(j + 1) * chunk] for c in cs]
        qt = [q[c] * jnp.exp((b[c] - rj[c]) + negq_ref[j]) for c in cs]
        kh = [k[c] * jnp.exp((rj[c] - b[c]) + negk_ref[j]) for c in cs]
        qs = [jnp.concatenate([jnp.where(lane_h == h, qt[c], 0.0) for h in range(GLA_HEADS)], axis=0) for c in cs]
        att = [att[c] + _dot_nt(qs[c].astype(BF16), kh[c].astype(BF16)) * pair_ref[j] for c in cs]
    x = [_dot(att[c].astype(BF16), vb[c]) for c in cs]
    o = [_dot((q[c] * k[c]).astype(BF16), bdb) * v[c] for c in cs]
    for h in range(GLA_HEADS):
        o = [o[c] + jnp.where(vlane_h == h, x[c][h * chunk:(h + 1) * chunk], 0.0) for c in cs]
    bT = [x.T for x in b]
    kT = [x.T for x in k]
    bendT = [x[:, chunk - 1:chunk] for x in bT]
    upd = [_dot((kT[c] * jnp.exp(bendT[c] - bT[c])).astype(BF16), vb[c]) * bd for c in cs]
    decay = [jnp.exp(x) for x in bendT]
    qe = [(q[c] * jnp.exp(b[c])).astype(BF16) for c in cs]
    S = s_ref[...]
    for c in cs:
        o[c] = o[c] + _dot(qe[c], S.astype(BF16))
        S = decay[c] * S + upd[c]
    s_ref[...] = S
    ms = [_group_mean64(x * x, ones_ref[...]) for x in o]
    for c in cs:
        o_ref[0, rows[c], :] = o[c] * lax.rsqrt(ms[c] + EPS) * gn_ref[...] * _silu(g_ref[0, rows[c], :])

    @pl.when(i == pl.num_programs(1) - 1)
    def _():
        for h in range(GLA_HEADS):
            sfin_ref[0, h] = s_ref[32 * h:32 * h + 32, 64 * h:64 * h + 64]


def _gla_tables(chunk):
    t = jnp.arange(chunk)
    gsel, negq, negk, pair = [], [], [], []
    m = chunk // 2
    while m >= 1:
        blk, second = t // (2 * m), (t % (2 * m)) >= m
        gsel.append((t[None, :] == (blk * 2 * m + m - 1)[:, None]).astype(BF16))
        negq.append(jnp.broadcast_to(jnp.where(second, 0.0, -jnp.inf)[:, None], (chunk, 128)))
        negk.append(jnp.broadcast_to(jnp.where(second, -jnp.inf, 0.0)[:, None], (chunk, 128)))
        pair.append(jnp.tile((blk[:, None] == blk[None, :]).astype(F32), (GLA_HEADS, 1)))
        m //= 2
    return (jnp.concatenate(gsel, axis=0), jnp.stack(negq).astype(F32), jnp.stack(negk).astype(F32),
            jnp.stack(pair))


def _gla(gqk, gv, gg, misc, gw, l, s0, ones_bd, bd_mask, tm, chunk, n_valid):
    G, T, _ = gqk.shape
    wspec = lambda shape: pl.BlockSpec((None,) + shape, lambda b, i: (l,) + (0,) * len(shape))
    tables = _gla_tables(chunk)
    const = lambda a: pl.BlockSpec(a.shape, lambda b, i: (0,) * a.ndim)
    return pl.pallas_call(
        functools.partial(_gla_kernel, tm=tm, chunk=chunk, n_valid=n_valid),
        grid=(G, T // tm),
        in_specs=[pl.BlockSpec((1, tm, 256), lambda b, i: (b, i, 0)),
                  pl.BlockSpec((1, tm, 256), lambda b, i: (b, i, 0)),
                  pl.BlockSpec((1, tm, 256), lambda b, i: (b, i, 0)),
                  pl.BlockSpec((1, tm, 128), lambda b, i: (b, i, 0)),
                  wspec((128, 128)), wspec((1, 128)), wspec((1, 256)),
                  pl.BlockSpec((256, 256), lambda b, i: (0, 0)),
                  pl.BlockSpec((128, 256), lambda b, i: (0, 0)),
                  pl.BlockSpec((1, GLA_HEADS, GLA_DK, GLA_DV), lambda b, i: (b, 0, 0, 0))]
                 + [const(a) for a in tables],
        out_specs=[pl.BlockSpec((1, tm, 256), lambda b, i: (b, i, 0)),
                   pl.BlockSpec((1, GLA_HEADS, GLA_DK, GLA_DV), lambda b, i: (b, 0, 0, 0))],
        out_shape=[jax.ShapeDtypeStruct((G, T, 256), F32),
                   jax.ShapeDtypeStruct((G, GLA_HEADS, GLA_DK, GLA_DV), F32)],
        scratch_shapes=[pltpu.VMEM((128, 256), F32)],
        compiler_params=_cparams("arbitrary", "arbitrary"),
        name="gla",
    )(gqk, gv, gg, misc, gw["wa"], gw["ba"], gw["gn"], ones_bd, bd_mask, s0, *tables)


def _ffn_kernel(*refs, tm, seq, final, nb):
    (x_ref, oa_ref, ob_ref, oc_ref, od_ref, g1_ref, sh2_ref, sc2_ref, g2_ref, n2_ref,
     wout_ref, wg_ref, wu_ref, wd_ref, cw_ref, cb_ref, buf_ref, fng_ref, y_ref, cs_ref) = refs[:20]
    x = x_ref[0]
    mix = _dot(oa_ref[0].astype(BF16), wout_ref[0:256, :])
    mix = mix + _dot(ob_ref[0].astype(BF16), wout_ref[256:512, :])
    mix = mix + _dot(oc_ref[0].astype(BF16), wout_ref[512:768, :])
    mix = mix + _dot(od_ref[0].astype(BF16), wout_ref[768:1024, :])
    x1 = x + g1_ref[0] * mix
    h2 = _rms(x1, n2_ref[...]) * (1.0 + sc2_ref[0]) + sh2_ref[0]
    hb = h2.astype(BF16)
    gate = _dot(hb, wg_ref[...])
    w = cw_ref[...]
    if seq:
        cbuf_ref = refs[20]

        @pl.when(pl.program_id(1) == 0)
        def _():
            cbuf_ref[...] = buf_ref[0]

        prev8 = cbuf_ref[...]
        z = cb_ref[...] + gate * w[2:3] + _shift_rows(gate, 1, prev8) * w[1:2] + _shift_rows(gate, 2, prev8) * w[0:1]
        cbuf_ref[...] = gate[tm - 8:]
        cs_ref[0] = gate[tm - 8:]
    else:
        xx = jnp.concatenate([buf_ref[0], gate], axis=0)
        z = cb_ref[...] + xx[0:tm] * w[0:1] + xx[nb:nb + tm] * w[1:2] + xx[2 * nb:2 * nb + tm] * w[2:3]
        cs_ref[0] = xx[tm:tm + 2 * nb]
    up = _dot(hb, wu_ref[...])
    y = _dot((_silu(z) * up).astype(BF16), wd_ref[...])
    x2 = x1 + g2_ref[0] * y
    if final:
        x2 = _rms(x2, fng_ref[...])
    y_ref[0] = x2


def _ffn(x, o4, mods, fw, l, buf, final_g, tm, seq, final, nb=0):
    G, T, D = x.shape
    per_row = mods[0].shape[1] != 1
    mod_spec = (pl.BlockSpec((1, tm, D), lambda b, i: (b, i, 0)) if per_row
                else pl.BlockSpec((1, 1, D), lambda b, i: (b, 0, 0)))
    o_spec = pl.BlockSpec((1, tm, 256), lambda b, i: (b, i, 0))
    once = pl.Buffered(1)
    wspec = lambda shape: pl.BlockSpec((None,) + shape, lambda b, i: (l,) + (0,) * len(shape), pipeline_mode=once)
    nbuf = buf.shape[1]
    return pl.pallas_call(
        functools.partial(_ffn_kernel, tm=tm, seq=seq, final=final, nb=nb),
        grid=(G, T // tm),
        in_specs=[pl.BlockSpec((1, tm, D), lambda b, i: (b, i, 0)), o_spec, o_spec, o_spec, o_spec,
                  mod_spec, mod_spec, mod_spec, mod_spec,
                  wspec((1, D)), wspec((D, D)), wspec((D, D_FF)), wspec((D, D_FF)), wspec((D_FF, D)),
                  wspec((FFN_CONV, D_FF)), wspec((1, D_FF)),
                  pl.BlockSpec((1, nbuf, D_FF), lambda b, i: (b, 0, 0)),
                  pl.BlockSpec((1, D), lambda b, i: (0, 0))],
        out_specs=[pl.BlockSpec((1, tm, D), lambda b, i: (b, i, 0)),
                   pl.BlockSpec((1, nbuf, D_FF), lambda b, i: (b, 0, 0))],
        out_shape=[jax.ShapeDtypeStruct((G, T, D), F32), jax.ShapeDtypeStruct((G, nbuf, D_FF), F32)],
        scratch_shapes=[pltpu.VMEM((8, D_FF), F32)] if seq else [],
        compiler_params=_cparams("arbitrary", "arbitrary"),
        name="out_ffn",
    )(x, *o4, *mods, fw["n2"], fw["wout"], fw["wg"], fw["wu"], fw["wd"], fw["cw"], fw["cb"], buf, final_g)


def _diff_dec_kernel(pt_ref, lam_ref, q_ref, *refs, n_new, out_scale, nps):
    kT_refs, vT_refs = refs[:nps], refs[nps:2 * nps]
    kn_ref, vnT_ref, g_ref, ones_ref, o_ref, qs_ref, m_ref, l_ref, acc_ref, osc_ref = refs[2 * nps:]
    p = pl.program_id(1)
    lane = lax.broadcasted_iota(I32, (8, 256), 1)

    @pl.when(p == 0)
    def _():
        q = q_ref[0]
        for hm in range(8):
            qs_ref[8 * hm:8 * hm + 8, :] = jnp.where((lane // 32) == hm, q, 0.0).astype(BF16)
        m_ref[...] = jnp.full(m_ref.shape, NEG_BIG, F32)
        l_ref[...] = jnp.zeros(l_ref.shape, F32)
        acc_ref[...] = jnp.zeros(acc_ref.shape, F32)

    def update(s, v_contract):
        m_prev = m_ref[...]
        m_new = jnp.maximum(m_prev, jnp.max(s, axis=1, keepdims=True))
        alpha = jnp.exp(m_prev - m_new)
        pr = jnp.exp(s - m_new)
        l_ref[...] = alpha * l_ref[...] + jnp.sum(pr, axis=1, keepdims=True)
        acc_ref[...] = alpha * acc_ref[...] + v_contract(pr.astype(BF16))
        m_ref[...] = m_new

    qs = qs_ref[...]
    kcat = jnp.concatenate([r[0, 0].astype(BF16) for r in kT_refs], axis=1)
    vcat = jnp.concatenate([r[0, 0].astype(BF16) for r in vT_refs], axis=1)
    update(_dot(qs, kcat) * (DIFF_QK ** -0.5), lambda pr: _dot_nt(pr, vcat))

    @pl.when(p == pl.num_programs(1) - 1)
    def _():
        s = _dot_nt(qs, kn_ref[0].astype(BF16)) * (DIFF_QK ** -0.5)
        t = lax.broadcasted_iota(I32, s.shape, 0) % 8
        c = lax.broadcasted_iota(I32, s.shape, 1)
        s = jnp.where((c <= t) & (c < n_new), s, NEG_BIG)
        vnT = vnT_ref[0].astype(BF16)
        update(s, lambda pr: _dot_nt(pr, vnT))
        o_ref[0] = _diff_finalize(acc_ref, l_ref, lam_ref[0, 0], osc_ref, g_ref, ones_ref, out_scale, 8, True)


def _page_specs(l, nps, rows):
    return [pl.BlockSpec((1, 1, rows, PAGE), lambda b, p, pt, i=i: (l, pt[b, p * nps + i], 0, 0))
            for i in range(nps)]


def _diff_dec(page_table, lam, q8, ckT, cvT, kn, vnT, g_tiled, ones_bd, l, out_scale, n_new):
    B, NP = page_table.shape
    nps = math.gcd(NP, PAGES_PER_STEP)
    gs = pltpu.PrefetchScalarGridSpec(
        num_scalar_prefetch=1, grid=(B, NP // nps),
        in_specs=[pl.BlockSpec(memory_space=pltpu.SMEM),
                  pl.BlockSpec((1, 8, 256), lambda b, p, pt: (b, 0, 0))]
                 + _page_specs(l, nps, 256) + _page_specs(l, nps, 256)
                 + [pl.BlockSpec((1, 128, 256), lambda b, p, pt: (b, 0, 0)),
                    pl.BlockSpec((1, 256, 128), lambda b, p, pt: (b, 0, 0)),
                    pl.BlockSpec((None, 1, 256), lambda b, p, pt: (l, 0, 0)),
                    pl.BlockSpec((256, 256), lambda b, p, pt: (0, 0))],
        out_specs=pl.BlockSpec((1, 8, 256), lambda b, p, pt: (b, 0, 0)),
        scratch_shapes=[pltpu.VMEM((64, 256), BF16), pltpu.VMEM((64, 1), F32), pltpu.VMEM((64, 1), F32),
                        pltpu.VMEM((64, 256), F32), pltpu.VMEM((8, 256), F32)])
    return pl.pallas_call(
        functools.partial(_diff_dec_kernel, n_new=n_new, out_scale=out_scale, nps=nps),
        grid_spec=gs, out_shape=jax.ShapeDtypeStruct((B, 8, 256), F32),
        compiler_params=_cparams("arbitrary", "arbitrary"),
        name="diff_decode",
    )(page_table, lam, q8, *([ckT] * nps), *([cvT] * nps), kn, vnT, g_tiled, ones_bd)


def _idx_dec_kernel(pt_ref, iq_ref, w_ref, *refs, n_new, nps):
    ikT_refs = refs[:nps]
    iknT_ref, keys_ref, keysn_ref = refs[nps:]
    p = pl.program_id(1)
    iq = iq_ref[0].astype(BF16)
    w = w_ref[0]

    def scores(ikT):
        sc = _dot(iq, ikT.astype(BF16)) * (IDX_DIM ** -0.5)
        tot = jnp.sum((w * jnp.maximum(sc, 0.0)).reshape(8, 8, PAGE), axis=1)
        return jnp.where(tot == 0.0, 0.0, tot)

    for i in range(nps):
        keys_ref[0, i] = scores(ikT_refs[i][0, 0])

    @pl.when(p == pl.num_programs(1) - 1)
    def _():
        tot = scores(iknT_ref[0])
        t = lax.broadcasted_iota(I32, tot.shape, 0)
        c = lax.broadcasted_iota(I32, tot.shape, 1)
        keysn_ref[0] = jnp.where((c <= t) & (c < n_new), tot, -jnp.inf)


def _idx_dec(page_table, iq64, w64, cikT, iknT, l, n_new):
    B, NP = page_table.shape
    nps = math.gcd(NP, PAGES_PER_STEP)
    gs = pltpu.PrefetchScalarGridSpec(
        num_scalar_prefetch=1, grid=(B, NP // nps),
        in_specs=[pl.BlockSpec((1, 64, IDX_DIM), lambda b, p, pt: (b, 0, 0)),
                  pl.BlockSpec((1, 64, 1), lambda b, p, pt: (b, 0, 0))]
                 + _page_specs(l, nps, IDX_DIM)
                 + [pl.BlockSpec((1, IDX_DIM, 128), lambda b, p, pt: (b, 0, 0))],
        out_specs=[pl.BlockSpec((1, nps, 8, PAGE), lambda b, p, pt: (b, p, 0, 0)),
                   pl.BlockSpec((1, 8, 128), lambda b, p, pt: (b, 0, 0))])
    return pl.pallas_call(
        functools.partial(_idx_dec_kernel, n_new=n_new, nps=nps),
        grid_spec=gs,
        out_shape=[jax.ShapeDtypeStruct((B, NP, 8, PAGE), F32), jax.ShapeDtypeStruct((B, 8, 128), F32)],
        compiler_params=_cparams("arbitrary", "arbitrary"),
        name="idx_decode",
    )(page_table, iq64, w64, *([cikT] * nps), iknT)


def _dec_thr_kernel(keys_ref, keysn_ref, thr_ref, g_ref, *, topk, n_pages, n_col_bits):
    nb = keys_ref.shape[0]
    new_col = lax.broadcasted_iota(I32, (nb, 8, PAGE), 2) + n_pages * PAGE

    def count_ge(c):
        acc = (jnp.sum(jnp.where(keys_ref[...] >= c[:, None], 1.0, 0.0), axis=1)
               + jnp.where(keysn_ref[...] >= c, 1.0, 0.0))
        return jnp.sum(acc, axis=2, keepdims=True)

    def count_tie_below(tau, c):
        page_col = (lax.broadcasted_iota(I32, (nb, n_pages, 8, PAGE), 1) * PAGE
                    + lax.broadcasted_iota(I32, (nb, n_pages, 8, PAGE), 3))
        hit = jnp.where(keys_ref[...] == tau[:, None], jnp.where(page_col < c[:, None], 1.0, 0.0), 0.0)
        acc = (jnp.sum(hit, axis=1)
               + jnp.where(keysn_ref[...] == tau, jnp.where(new_col < c, 1.0, 0.0), 0.0))
        return jnp.sum(acc, axis=2, keepdims=True)

    t0, t1, g = _topk_threshold(count_ge, count_tie_below, topk, n_col_bits, (nb, 8, 1))
    thr_ref[...] = jnp.zeros(thr_ref.shape, F32)
    thr_ref[:, :, 0:1] = t0
    thr_ref[:, :, 1:2] = t1
    g_ref[...] = jnp.broadcast_to(g, g_ref.shape)


def _dec_thr(keys, keysn, topk):
    B, NP = keys.shape[:2]
    nb = math.gcd(B, 16)
    n_keys = NP * PAGE + 128
    return pl.pallas_call(
        functools.partial(_dec_thr_kernel, topk=topk, n_pages=NP, n_col_bits=(n_keys - 1).bit_length()),
        grid=(B // nb,),
        in_specs=[pl.BlockSpec((nb, NP, 8, PAGE), lambda b: (b, 0, 0, 0)),
                  pl.BlockSpec((nb, 8, 128), lambda b: (b, 0, 0))],
        out_specs=[pl.BlockSpec((nb, 8, 128), lambda b: (b, 0, 0)), pl.BlockSpec((nb, 8, 128), lambda b: (b, 0, 0))],
        out_shape=[jax.ShapeDtypeStruct((B, 8, 128), F32), jax.ShapeDtypeStruct((B, 8, 128), I32)],
        compiler_params=_cparams("arbitrary"),
        name="decode_threshold",
    )(keys, keysn)


def _dsa_dec_kernel(pt_ref, q_ref, keys_ref, keysn_ref, thr_ref, g_ref, *refs, n_pages, nps):
    kT_refs, vT_refs = refs[:nps], refs[nps:2 * nps]
    kn_ref, vnT_ref, o_ref, qs_ref, m_ref, l_ref, acc_ref = refs[2 * nps:]
    p = pl.program_id(1)
    lane = lax.broadcasted_iota(I32, (8, 256), 1)
    colp = lax.broadcasted_iota(I32, (8, PAGE), 1)

    @pl.when(p == 0)
    def _():
        q = q_ref[0]
        for h in range(DSA_HEADS):
            qs_ref[8 * h:8 * h + 8, :] = jnp.where((lane // 64) == h, q, 0.0).astype(BF16)
        m_ref[...] = jnp.full(m_ref.shape, NEG_BIG, F32)
        l_ref[...] = jnp.zeros(l_ref.shape, F32)
        acc_ref[...] = jnp.zeros(acc_ref.shape, F32)

    t0 = thr_ref[0, :, 0:1]
    t1 = thr_ref[0, :, 1:2]
    g = g_ref[0, :, 0:1]

    def update(s, key8, thr8, v_contract):
        sel = jnp.concatenate([key8] * DSA_HEADS, axis=0) >= jnp.concatenate([thr8] * DSA_HEADS, axis=0)
        m_prev = m_ref[...]
        m_new = jnp.maximum(m_prev, jnp.max(jnp.where(sel, s, NEG_BIG), axis=1, keepdims=True))
        alpha = jnp.exp(m_prev - m_new)
        pr = jnp.where(sel, jnp.exp(s - m_new), 0.0)
        l_ref[...] = alpha * l_ref[...] + jnp.sum(pr, axis=1, keepdims=True)
        acc_ref[...] = alpha * acc_ref[...] + v_contract(pr.astype(BF16))
        m_ref[...] = m_new

    qs = qs_ref[...]
    kcat = jnp.concatenate([r[0, 0].astype(BF16) for r in kT_refs], axis=1)
    vcat = jnp.concatenate([r[0, 0].astype(BF16) for r in vT_refs], axis=1)
    s = _dot(qs, kcat) * (DSA_HD ** -0.5)
    key8 = jnp.concatenate([keys_ref[0, p * nps + i] for i in range(nps)], axis=1)
    col = lax.broadcasted_iota(I32, key8.shape, 1) + p * (nps * PAGE)
    thr8 = jnp.where(col > g, t1, t0)

    update(s, key8, thr8, lambda pr: _dot_nt(pr, vcat))

    @pl.when(p == pl.num_programs(1) - 1)
    def _():
        thrn = jnp.where(colp + n_pages * PAGE > g, t1, t0)
        s = _dot_nt(qs, kn_ref[0].astype(BF16)) * (DSA_HD ** -0.5)
        vnT = vnT_ref[0].astype(BF16)
        update(s, keysn_ref[0], thrn, lambda pr: _dot_nt(pr, vnT))
        for h in range(DSA_HEADS):
            o_ref[0, :, 64 * h:64 * h + 64] = (acc_ref[8 * h:8 * h + 8, 64 * h:64 * h + 64]
                                               / l_ref[8 * h:8 * h + 8, :])


def _dsa_dec(page_table, q8, keys, keysn, ckT, cvT, kn, vnT, l, topk):
    B, NP = page_table.shape
    nps = math.gcd(NP, PAGES_PER_STEP)
    thr, gcol = _dec_thr(keys, keysn, topk)
    gs = pltpu.PrefetchScalarGridSpec(
        num_scalar_prefetch=1, grid=(B, NP // nps),
        in_specs=[pl.BlockSpec((1, 8, 256), lambda b, p, pt: (b, 0, 0)),
                  pl.BlockSpec((1, NP, 8, PAGE), lambda b, p, pt: (b, 0, 0, 0)),
                  pl.BlockSpec((1, 8, 128), lambda b, p, pt: (b, 0, 0)),
                  pl.BlockSpec((1, 8, 128), lambda b, p, pt: (b, 0, 0)),
                  pl.BlockSpec((1, 8, 128), lambda b, p, pt: (b, 0, 0))]
                 + _page_specs(l, nps, 256) + _page_specs(l, nps, 256)
                 + [pl.BlockSpec((1, 128, 256), lambda b, p, pt: (b, 0, 0)),
                    pl.BlockSpec((1, 256, 128), lambda b, p, pt: (b, 0, 0))],
        out_specs=pl.BlockSpec((1, 8, 256), lambda b, p, pt: (b, 0, 0)),
        scratch_shapes=[pltpu.VMEM((32, 256), BF16), pltpu.VMEM((32, 1), F32),
                        pltpu.VMEM((32, 1), F32), pltpu.VMEM((32, 256), F32)])
    return pl.pallas_call(
        functools.partial(_dsa_dec_kernel, n_pages=NP, nps=nps),
        grid_spec=gs, out_shape=jax.ShapeDtypeStruct((B, 8, 256), F32),
        compiler_params=_cparams("arbitrary", "arbitrary"),
        name="dsa_decode",
    )(page_table, q8, keys, keysn, thr, gcol, *([ckT] * nps), *([cvT] * nps), kn, vnT)


def _rope_tables(pos, half):
    inv = ROPE_THETA ** (-jnp.arange(half, dtype=F32) / half)
    ang = pos.astype(F32)[:, None] * inv[None, :]
    cos, sin = jnp.cos(ang), jnp.sin(ang)
    reps = 128 // (2 * half)
    return (jnp.tile(jnp.concatenate([cos, cos], axis=1), (1, reps)),
            jnp.tile(jnp.concatenate([-sin, sin], axis=1), (1, reps)))


def _pack_w_in(w_in):
    names = (("lru_x", 256), ("lru_gate", 256), ("diff_q", 256), ("diff_k", 256), ("diff_v", 256),
             ("dsa_q", 256), ("dsa_k", 256), ("dsa_v", 256), ("idx_q", 256), ("idx_k", 32), ("idx_w", 8),
             ("gla_q", 128), ("gla_k", 128), ("gla_v", 256), ("gla_g", 256), ("gla_a", 16))
    seg, off = {}, 0
    for n, w in names:
        seg[n] = w_in[..., off:off + w]
        off += w
    pad = jnp.zeros(w_in.shape[:-1] + (128 - 24,), w_in.dtype)
    cols = [seg["lru_x"], seg["lru_gate"], seg["diff_q"], seg["diff_k"],
            seg["dsa_q"], seg["dsa_k"], seg["idx_q"]] + [seg["idx_k"]] * IDX_HEADS + [
            seg["gla_q"], seg["gla_k"], seg["gla_v"], seg["gla_g"], seg["idx_w"], seg["gla_a"], pad]
    wvT = jnp.concatenate([seg["diff_v"], seg["dsa_v"]], axis=-1).transpose(0, 2, 1)
    return jnp.concatenate(cols, axis=-1).astype(BF16), wvT.astype(BF16)


def _block_diag(w):
    depth, nb, bw, _ = w.shape
    eye = jnp.eye(nb, dtype=w.dtype)
    return jnp.einsum("lnij,nm->lnimj", w, eye).reshape(depth, nb * bw, nb * bw)


def _group_ones(n, group):
    r = jnp.arange(n) // group
    return (r[:, None] == r[None, :]).astype(BF16)


def _prompt_layer(x, mod, l, W, tabs, final):
    G, T, _ = x.shape
    tm, tq = min(SEQ_TILE, T), min(ATTN_TILE, T)
    td = min(DENSE_TILE, T)
    assert T % tm == 0 and T % tq == 0 and T % td == 0, "sequence length must be a multiple of every tile"
    sh1, sc1, g1, sh2, sc2, g2 = [m[:, None, :] for m in jnp.split(mod, 6, axis=-1)]
    (lru, qb, kb, vbT, qc, kc, vcT, iq, ikt, gqk, gv, gg, misc) = _proj(
        x, sh1, sc1, W["norm1"], W["w_in"], W["wvT"], l, tabs, td)
    zeros = lambda *s: jnp.zeros(s, F32)
    o_a, h_last, cs8 = _lru_seq(lru, W["lru"], l, zeros(G, 1, 256), zeros(G, 8, 256), tm)
    o_b = _diff_attn(W["lam"][l], qb, kb, vbT, W["subln"], l, W["out_scale"][l], tq)
    o_c = _dsa_attn(qc, kc, vcT, iq, ikt, misc, min(TOPK_MAX, T // 4), tq)
    from_T = lambda a: a.reshape(G, 4, 64, T).transpose(0, 3, 1, 2)
    o_d, s_last = _gla(gqk, gv, gg, misc, W["gla"], l, zeros(G, GLA_HEADS, GLA_DK, GLA_DV),
                       W["ones64"], W["bd_mask"], tm, GLA_CHUNK, None)
    y, fcs8 = _ffn(x, (o_a, o_b, o_c, o_d), (g1, sh2, sc2, g2), W["ffn"], l, zeros(G, 8, D_FF),
                   W["final_g"], td, True, final)
    new = dict(
        diff_k=kb.reshape(G, T, DIFF_HEADS, 2, DIFF_QK), diff_v=from_T(vbT),
        dsa_k=kc.reshape(G, T, DSA_HEADS, DSA_HD), dsa_v=from_T(vcT),
        idx_k=ikt[:, :, :IDX_DIM], lru_h=h_last[:, 0], lru_conv=cs8[:, 8 - (LRU_CONV - 1):],
        gla=s_last, ffn_conv=fcs8[:, 8 - (FFN_CONV - 1):])
    return y, new


def _pad_rows(a, n):
    return jnp.pad(a, ((0, 0), (0, n - a.shape[1])) + ((0, 0),) * (a.ndim - 2))


def _sample_layer(xr, mod, l, W, tabs, caches, states, page_table, final, B, Tn):
    n = Tn * B
    tm_major = lambda a: jnp.tile(a, (Tn, 1))[None]
    sh1, sc1, g1, sh2, sc2, g2 = [tm_major(m) for m in jnp.split(mod, 6, axis=-1)]
    outs = _proj(xr, sh1, sc1, W["norm1"], W["w_in"], W["wvT"], l, tabs, n)
    (lru, qb, kb, vbT, qc, kc, vcT, iq, ikt, gqk, gv, gg, misc) = [o[0] for o in outs]
    to_b = lambda a: a.reshape(Tn, B, -1).transpose(1, 0, 2)
    to_t = lambda a: a.transpose(1, 0, 2).reshape(n, -1)
    to_bT = lambda a: a.reshape(256, Tn, B).transpose(2, 0, 1)
    pad_lanes = lambda a: jnp.pad(a, ((0, 0), (0, 0), (0, 128 - a.shape[2])))

    conv_prev = states["lru_conv"][l].transpose(1, 0, 2).reshape((LRU_CONV - 1) * B, 256)
    xx = jnp.concatenate([conv_prev, lru[:, :256]], axis=0)
    o_a, h_last = _lru_rows(xx, lru[:, 256:], W["lru"], l, states["lru_h"][l], B, Tn)
    new_conv = xx[Tn * B:].reshape(LRU_CONV - 1, B, 256).transpose(1, 0, 2)

    kb_b, vb_bT = to_b(kb), to_bT(vbT)
    o_b = _diff_dec(page_table, W["lam"][l], _pad_rows(to_b(qb), 8), caches["diff_kT"], caches["diff_vT"],
                    _pad_rows(kb_b, 128), pad_lanes(vb_bT), W["subln"], W["ones64"], l,
                    W["out_scale"][l], Tn)[:, :Tn]

    past_len = page_table.shape[1] * PAGE
    topk = min(TOPK_MAX, (past_len + Tn) // 4)
    kc_b, vc_bT = to_b(kc), to_bT(vcT)
    ik_b = to_b(ikt)[:, :, :IDX_DIM]
    iq64 = _pad_rows(to_b(iq), 8).reshape(B, 8 * IDX_HEADS, IDX_DIM)
    w64 = (_pad_rows(to_b(misc)[:, :, :IDX_HEADS], 8) * (IDX_HEADS ** -0.5)).reshape(B, 8 * IDX_HEADS, 1)
    iknT = _pad_rows(ik_b, 128).transpose(0, 2, 1)
    keys, keysn = _idx_dec(page_table, iq64, w64, caches["idx_kT"], iknT, l, Tn)
    o_c = _dsa_dec(page_table, _pad_rows(to_b(qc), 8), keys, keysn, caches["dsa_kT"], caches["dsa_vT"],
                   _pad_rows(kc_b, 128), pad_lanes(vc_bT), l, topk)[:, :Tn]

    padc = lambda a: _pad_rows(to_b(a), GLA_CHUNK)
    o_d, s_last = _gla(padc(gqk), padc(gv), padc(gg), padc(misc), W["gla"], l, states["gla"][l],
                       W["ones64"], W["bd_mask"], GLA_CHUNK, GLA_CHUNK, Tn)
    o_d = o_d[:, :Tn]

    ffn_prev = states["ffn_conv"][l].transpose(1, 0, 2).reshape(1, (FFN_CONV - 1) * B, D_FF)
    o4 = (o_a[None], to_t(o_b)[None], to_t(o_c)[None], to_t(o_d)[None])
    y, fcs = _ffn(xr, o4, (g1, sh2, sc2, g2), W["ffn"], l, ffn_prev, W["final_g"], n, False, final, nb=B)
    new = dict(
        diff_k=kb_b.reshape(B, Tn, DIFF_HEADS, 2, DIFF_QK),
        diff_v=vb_bT.transpose(0, 2, 1).reshape(B, Tn, DIFF_HEADS, 64),
        dsa_k=kc_b.reshape(B, Tn, DSA_HEADS, DSA_HD),
        dsa_v=vc_bT.transpose(0, 2, 1).reshape(B, Tn, DSA_HEADS, DSA_HD),
        idx_k=ik_b, lru_h=h_last, lru_conv=new_conv, gla=s_last,
        ffn_conv=fcs[0].reshape(FFN_CONV - 1, B, D_FF).transpose(1, 0, 2))
    return y, new


def kernel(x_prompt, x_sample, cache_diff_k, cache_diff_v, cache_dsa_k, cache_dsa_v, cache_idx_k, state_lru_h, state_lru_conv, state_gla, state_ffn_conv, page_table, c_prompt, c_sample, w_ada, b_ada, norm1_g, w_in, lru_conv_w, lru_conv_b, lru_wa, lru_ba, lru_wx, lru_bx, lru_lambda, diff_lq1, diff_lk1, diff_lq2, diff_lk2, diff_subln_g, gla_wa2, gla_ba, gla_norm_g, w_out, norm2_g, ffn_w_gate, ffn_w_up, ffn_conv_w, ffn_conv_b, ffn_w_down, final_norm_g):
    depth = w_in.shape[0]
    Bp, T, D = x_prompt.shape
    Bs, Tn, _ = x_sample.shape
    n_pool = cache_diff_k.shape[1]
    past_len = page_table.shape[1] * PAGE

    lam_init = jnp.asarray([0.8 - 0.6 * math.exp(-0.3 * l) for l in range(depth)], F32)
    lam = (jnp.exp(jnp.sum(diff_lq1 * diff_lk1, axis=-1)) - jnp.exp(jnp.sum(diff_lq2 * diff_lk2, axis=-1))
           + lam_init).reshape(depth, 1, 1)
    row3 = lambda a: a.reshape(depth, 1, a.shape[-1])
    wa2_pad = jnp.zeros((depth, 128, 128), F32).at[:, IDX_HEADS:IDX_HEADS + gla_wa2.shape[1], :].set(gla_wa2)
    w_packed, wvT = _pack_w_in(w_in)
    W = dict(
        norm1=row3(norm1_g), w_in=w_packed, wvT=wvT,
        lru=dict(conv_w=lru_conv_w, conv_b=row3(lru_conv_b), wa=_block_diag(lru_wa).astype(BF16),
                 wx=_block_diag(lru_wx).astype(BF16), ba=row3(lru_ba), bx=row3(lru_bx), lam=row3(lru_lambda)),
        lam=lam, out_scale=[1.0 - (0.8 - 0.6 * math.exp(-0.3 * l)) for l in range(depth)],
        subln=row3(jnp.tile(diff_subln_g, (1, DIFF_HEADS))), ones64=_group_ones(256, 64),
        bd_mask=(jnp.arange(128)[:, None] // GLA_DK == jnp.arange(256)[None, :] // GLA_DV).astype(F32),
        gla=dict(wa=wa2_pad.astype(BF16), ba=row3(gla_ba), gn=row3(jnp.tile(gla_norm_g, (1, GLA_HEADS)))),
        ffn=dict(n2=row3(norm2_g), wout=w_out.astype(BF16), wg=ffn_w_gate.astype(BF16),
                 wu=ffn_w_up.astype(BF16), wd=ffn_w_down.astype(BF16), cw=ffn_conv_w, cb=row3(ffn_conv_b)),
        final_g=final_norm_g.reshape(1, D))

    mods = _ada(jnp.concatenate([c_prompt, c_sample], axis=0), w_ada, b_ada)

    pos_p = jnp.arange(T, dtype=I32)
    pos_s = jnp.repeat(past_len + jnp.arange(Tn, dtype=I32), Bs)
    tabs_p = _rope_tables(pos_p, 16) + _rope_tables(pos_p, 32)
    tabs_s = _rope_tables(pos_s, 16) + _rope_tables(pos_s, 32)

    caches = dict(
        diff_kT=cache_diff_k.transpose(0, 1, 3, 4, 5, 2).reshape(depth, n_pool, 256, PAGE),
        diff_vT=cache_diff_v.transpose(0, 1, 3, 4, 2).reshape(depth, n_pool, 256, PAGE),
        dsa_kT=cache_dsa_k.transpose(0, 1, 3, 4, 2).reshape(depth, n_pool, 256, PAGE),
        dsa_vT=cache_dsa_v.transpose(0, 1, 3, 4, 2).reshape(depth, n_pool, 256, PAGE),
        idx_kT=cache_idx_k.transpose(0, 1, 3, 2))
    states = dict(lru_h=state_lru_h, lru_conv=state_lru_conv, gla=state_gla, ffn_conv=state_ffn_conv)

    xp = x_prompt
    xs = x_sample.transpose(1, 0, 2).reshape(1, Tn * Bs, D)
    st_p, st_s = [], []
    for l in range(depth):
        final = l == depth - 1
        xp, new_p = _prompt_layer(xp, mods[l, :Bp], l, W, tabs_p, final)
        xs, new_s = _sample_layer(xs, mods[l, Bp:], l, W, tabs_s, caches, states, page_table, final, Bs, Tn)
        st_p.append(new_p)
        st_s.append(new_s)
    y_prompt = xp
    y_sample = xs.reshape(Tn, Bs, D).transpose(1, 0, 2)
    keys = ("diff_k", "diff_v", "dsa_k", "dsa_v", "idx_k", "lru_h", "lru_conv", "gla", "ffn_conv")
    P = [jnp.stack([s[k] for s in st_p]) for k in keys]
    S = [jnp.stack([s[k] for s in st_s]) for k in keys]
    return (y_prompt, y_sample, *P, *S)
```
